```python
import math
import jax
import jax.numpy as jnp
from jax import lax
import numpy as np

D_MODEL = 2048
BATCH = 16
SEQ = 2048
DEPTH = 4

GRID_W = 64
CTX_LEN = 256

N_MIXERS = 3
N_ATTN_LAYERS = (DEPTH + 2) // 3
N_DN_LAYERS = (DEPTH + 1) // 3
N_HG_LAYERS = DEPTH // 3
N_MOD = 6

DEEPNORM_ALPHA = (2.0 * DEPTH) ** 0.25
DEEPNORM_BETA = (8.0 * DEPTH) ** -0.25
NORM_EPS = 1e-6

ATTN_HD = 128
ATTN_HQ = D_MODEL // ATTN_HD
ATTN_HKV = ATTN_HQ // 4
ATTN_GROUP = ATTN_HQ // ATTN_HKV
ATTN_Q_BLOCK = 128
ATTN_IN = (ATTN_HQ + 2 * ATTN_HKV) * ATTN_HD
ROPE_BASE = 10000.0
ROPE_FREQS = ATTN_HD // 4

DN_DK = 128
DN_DV = 128
DN_HQK = D_MODEL // DN_DK
DN_HV = 2 * DN_HQK
DN_QK_W = DN_HQK * DN_DK
DN_V_W = DN_HV * DN_DV
DN_CONV_CH = 2 * DN_QK_W + DN_V_W
DN_IN = DN_CONV_CH + DN_V_W + 4 * DN_HV
DN_CONV_W = 5
DN_CHUNK = 64

HG_DK = 128
HG_DV = 128
HG_H = D_MODEL // HG_DV
HG_W = HG_H * HG_DK
HG_VW = HG_H * HG_DV
HG_IN = 3 * HG_W + 2 * HG_VW
HG_CHUNK = 16

N_EXPERTS = 16
N_GROUPS = 4
EXPERTS_PER_GROUP = N_EXPERTS // N_GROUPS
TOP_K = 2
D_EXPERT = 1024
EXPERT_BLOCK = 256

kernel_name = 'hybrid_interleaved_dit_moe'


def _layer_norm(x, g, b):
    xf = x.astype(jnp.float32)
    xc = xf - xf.mean(-1, keepdims=True)
    var = jnp.mean(xc * xc, -1, keepdims=True)
    y = xc * lax.rsqrt(var + NORM_EPS) * g.astype(jnp.float32) + b.astype(jnp.float32)
    return y.astype(x.dtype)


def _rms_norm(x, g):
    xf = x.astype(jnp.float32)
    y = xf * lax.rsqrt(jnp.mean(xf * xf, -1, keepdims=True) + NORM_EPS) * g.astype(jnp.float32)
    return y.astype(x.dtype)


def _l2_normalize(x):
    return x * lax.rsqrt(jnp.sum(x * x, -1, keepdims=True) + NORM_EPS)


def _rev(t, on):
    return t[:, ::-1] if on else t


def _to_chunks(t, chunk):
    b, l, h = t.shape[:3]
    t = t.reshape(b, l // chunk, chunk, h, *t.shape[3:])
    return jnp.moveaxis(t, (1, 3), (0, 2))


def _from_chunks(t):
    n, b, h, c = t.shape[:4]
    return jnp.moveaxis(t, (0, 2), (1, 3)).reshape(b, n * c, h, *t.shape[4:])


def _axial_rope_tables(n_tokens):
    rows = n_tokens // GRID_W
    row = jnp.repeat(jnp.arange(rows), GRID_W).astype(jnp.float32)
    col = (jnp.arange(rows * GRID_W) % GRID_W).astype(jnp.float32)
    inv_freq = ROPE_BASE ** (-jnp.arange(ROPE_FREQS, dtype=jnp.float32) / ROPE_FREQS)
    ang = jnp.stack([row[:, None] * inv_freq, col[:, None] * inv_freq], axis=1)
    return jnp.cos(ang), jnp.sin(ang)


def _apply_axial_rope(x, cos, sin):
    b, l, h, _ = x.shape
    xr = x.astype(jnp.float32).reshape(b, l, h, 2, 2, ROPE_FREQS)
    x1, x2 = xr[..., 0, :], xr[..., 1, :]
    cs, sn = cos[None, :, None], sin[None, :, None]
    out = jnp.stack([x1 * cs - x2 * sn, x2 * cs + x1 * sn], axis=-2)
    return out.reshape(x.shape).astype(x.dtype)


def _gqa_attend(qb, k, v):
    s = jnp.einsum('bqkgd,bskd->bkgqs', qb, k).astype(jnp.float32) * (ATTN_HD ** -0.5)
    p = jax.nn.softmax(s, axis=-1).astype(v.dtype)
    return jnp.einsum('bkgqs,bskd->bqkgd', p, v)


def attention_mixer(h_lat, h_ctx, w_in, q_g, k_g, w_out, ctx_out):
    def project(h):
        b, l, _ = h.shape
        q, k, v = jnp.split(h @ w_in, [ATTN_HQ * ATTN_HD, (ATTN_HQ + ATTN_HKV) * ATTN_HD], axis=-1)
        q = _rms_norm(q.reshape(b, l, ATTN_HQ, ATTN_HD), q_g)
        k = _rms_norm(k.reshape(b, l, ATTN_HKV, ATTN_HD), k_g)
        return q, k, v.reshape(b, l, ATTN_HKV, ATTN_HD)

    q_l, k_l, v_l = project(h_lat)
    q_c, k_c, v_c = project(h_ctx)
    b, l, _ = h_lat.shape
    lc = h_ctx.shape[1]
    cos, sin = _axial_rope_tables(l)
    q_l = _apply_axial_rope(q_l, cos, sin)
    k_l = _apply_axial_rope(k_l, cos, sin)
    k_all = jnp.concatenate([k_l, k_c], axis=1)
    v_all = jnp.concatenate([v_l, v_c], axis=1)
    n_blk = l // ATTN_Q_BLOCK
    q_blocks = q_l.reshape(b, n_blk, ATTN_Q_BLOCK, ATTN_HKV, ATTN_GROUP, ATTN_HD).swapaxes(0, 1)
    o_l = lax.map(lambda qb: _gqa_attend(qb, k_all, v_all), q_blocks)
    y_l = o_l.swapaxes(0, 1).reshape(b, l, ATTN_HQ * ATTN_HD) @ w_out
    y_c = None
    if ctx_out:
        o_c = _gqa_attend(q_c.reshape(b, lc, ATTN_HKV, ATTN_GROUP, ATTN_HD), k_c, v_c)
        y_c = o_c.reshape(b, lc, ATTN_HQ * ATTN_HD) @ w_out
    return y_l, y_c


def _centred_depthwise_conv(x, w):
    width = w.shape[0]
    pad = width // 2
    l = x.shape[1]
    xp = jnp.pad(x, ((0, 0), (pad, pad), (0, 0)))
    return sum(xp[:, j:j + l] * w[j] for j in range(width))


def _gated_delta_rule(q, k, v, g, beta, s0):
    c = DN_CHUNK
    dv = v.shape[-1]
    qc, kc, vc = _to_chunks(q, c), _to_chunks(k, c), _to_chunks(v, c)
    gc = jnp.cumsum(_to_chunks(g, c), axis=-1)
    bc = _to_chunks(beta, c)
    idx = jnp.arange(c)
    lower_incl = idx[:, None] >= idx[None, :]
    strict = idx[:, None] > idx[None, :]
    decay = jnp.exp(jnp.where(lower_incl, gc[..., :, None] - gc[..., None, :], -jnp.inf))
    kb = kc * bc[..., None]
    m = jnp.where(strict, jnp.einsum('nbhid,nbhjd->nbhij', kb, kc) * decay, 0.0)
    a = m + jnp.eye(c, dtype=m.dtype)
    rhs = jnp.concatenate([vc * bc[..., None], kb * jnp.exp(gc)[..., None]], axis=-1)
    sol = lax.linalg.triangular_solve(a, rhs, left_side=True, lower=True, unit_diagonal=True)
    u, w = sol[..., :dv], sol[..., dv:]
    qk = jnp.where(lower_incl, jnp.einsum('nbhid,nbhjd->nbhij', qc, kc) * decay, 0.0)
    q_dec = qc * jnp.exp(gc)[..., None]
    k_dec = kc * jnp.exp(gc[..., -1:] - gc)[..., None]
    g_last = jnp.exp(gc[..., -1])

    def step(s, xs):
        u_n, w_n, qk_n, qd_n, kd_n, gl_n = xs
        v_new = u_n - jnp.einsum('bhck,bhkv->bhcv', w_n, s)
        o_n = jnp.einsum('bhck,bhkv->bhcv', qd_n, s) + jnp.einsum('bhij,bhjv->bhiv', qk_n, v_new)
        s = s * gl_n[..., None, None] + jnp.einsum('bhck,bhcv->bhkv', kd_n, v_new)
        return s, o_n

    s_final, o = lax.scan(step, s0, (u, w, qk, q_dec, k_dec, g_last))
    return _from_chunks(o), s_final


def deltanet_mixer(h_lat, h_ctx, w_in, conv_w, a_log, dt_bias, norm_g, w_out, ctx_out):
    def project(h):
        b, l, _ = h.shape
        p = h @ w_in
        qkv = jax.nn.silu(_centred_depthwise_conv(p[..., :DN_CONV_CH], conv_w)).astype(jnp.float32)
        q = qkv[..., :DN_QK_W].reshape(b, l, DN_HQK, DN_DK)
        k = qkv[..., DN_QK_W:2 * DN_QK_W].reshape(b, l, DN_HQK, DN_DK)
        v = qkv[..., 2 * DN_QK_W:].reshape(b, l, DN_HV, DN_DV)
        q = jnp.repeat(_l2_normalize(q) * (DN_DK ** -0.5), DN_HV // DN_HQK, axis=2)
        k = jnp.repeat(_l2_normalize(k), DN_HV // DN_HQK, axis=2)
        z = p[..., DN_CONV_CH:DN_CONV_CH + DN_V_W]
        ba = p[..., DN_CONV_CH + DN_V_W:].astype(jnp.float32).reshape(b, l, 2, 2, DN_HV)
        return q, k, v, z, ba

    q_l, k_l, v_l, z_l, ba_l = project(h_lat)
    q_c, k_c, v_c, z_c, ba_c = project(h_ctx)
    b = h_lat.shape[0]
    a_log = a_log.astype(jnp.float32)
    dt_bias = dt_bias.astype(jnp.float32)

    def gates(ba, d):
        beta = jax.nn.sigmoid(ba[:, :, 0, d])
        g = -jnp.exp(a_log[d]) * jax.nn.softplus(ba[:, :, 1, d] + dt_bias[d])
        return g, beta

    outs_l, outs_c = [], []
    for d in range(2):
        r = d == 1
        g_c, b_c = gates(ba_c, d)
        g_l, b_l = gates(ba_l, d)
        s0 = jnp.zeros((b, DN_HV, DN_DK, DN_DV), jnp.float32)
        o_c, s_ctx = _gated_delta_rule(_rev(q_c, r), _rev(k_c, r), _rev(v_c, r), _rev(g_c, r), _rev(b_c, r), s0)
        o_l, _ = _gated_delta_rule(_rev(q_l, r), _rev(k_l, r), _rev(v_l, r), _rev(g_l, r), _rev(b_l, r), s_ctx)
        outs_c.append(_rev(o_c, r))
        outs_l.append(_rev(o_l, r))

    def readout(o, z):
        bb, l = o.shape[:2]
        o = _rms_norm(o, norm_g) * jax.nn.silu(z.astype(jnp.float32)).reshape(bb, l, DN_HV, DN_DV)
        return o.reshape(bb, l, DN_V_W).astype(w_out.dtype) @ w_out

    y_l = readout(outs_l[0] + outs_l[1], z_l)
    y_c = readout(outs_c[0] + outs_c[1], z_c) if ctx_out else None
    return y_l, y_c


def _gla_chunked(q, k, v, logf, s0):
    c = HG_CHUNK
    qc, kc, vc = _to_chunks(q, c), _to_chunks(k, c), _to_chunks(v, c)
    gcum = jnp.cumsum(_to_chunks(logf, c), axis=-2)
    q_dec = qc * jnp.exp(gcum)
    k_inv = kc * jnp.exp(-gcum)
    idx = jnp.arange(c)
    lower_incl = idx[:, None] >= idx[None, :]
    intra = jnp.where(lower_incl, jnp.einsum('nbhik,nbhjk->nbhij', q_dec, k_inv), 0.0)
    o_intra = jnp.einsum('nbhij,nbhjv->nbhiv', intra, vc)
    k_dec = kc * jnp.exp(gcum[..., -1:, :] - gcum)
    f_last = jnp.exp(gcum[..., -1, :])

    def step(s, xs):
        oi, qd, kd, vn, fl = xs
        o = oi + jnp.einsum('bhck,bhkv->bhcv', qd, s)
        s = s * fl[..., None] + jnp.einsum('bhck,bhcv->bhkv', kd, vn)
        return s, o

    s_final, o = lax.scan(step, s0, (o_intra, q_dec, k_dec, vc, f_last))
    return _from_chunks(o), s_final


def hgrn2_mixer(h_lat, h_ctx, w_in, lower_bound, norm_g, w_out, ctx_out):
    lb = lower_bound.astype(jnp.float32)

    def project(h):
        b, l, _ = h.shape
        p = (h @ w_in).astype(jnp.float32)
        q, i, gate, f = jnp.split(p, [HG_W, HG_W + HG_VW, HG_W + 2 * HG_VW], axis=-1)
        fg = lb + (1.0 - lb) * jax.nn.sigmoid(f.reshape(b, l, 2, HG_W))
        keys = (1.0 - fg).reshape(b, l, 2, HG_H, HG_DK)
        logf = jnp.log(fg).reshape(b, l, 2, HG_H, HG_DK)
        return q.reshape(b, l, HG_H, HG_DK), i.reshape(b, l, HG_H, HG_DV), gate, keys, logf

    q_l, i_l, gt_l, k_l, lf_l = project(h_lat)
    q_c, i_c, gt_c, k_c, lf_c = project(h_ctx)
    b = h_lat.shape[0]
    outs_l, outs_c = [], []
    for d in range(2):
        r = d == 1
        s0 = jnp.zeros((b, HG_H, HG_DK, HG_DV), jnp.float32)
        o_c, s_ctx = _gla_chunked(_rev(q_c, r), _rev(k_c[:, :, d], r), _rev(i_c, r), _rev(lf_c[:, :, d], r), s0)
        o_l, _ = _gla_chunked(_rev(q_l, r), _rev(k_l[:, :, d], r), _rev(i_l, r), _rev(lf_l[:, :, d], r), s_ctx)
        outs_c.append(_rev(o_c, r))
        outs_l.append(_rev(o_l, r))

    def readout(o, gate):
        bb, l = o.shape[:2]
        o = _rms_norm(o, norm_g).reshape(bb, l, HG_VW) * jax.nn.sigmoid(gate)
        return o.astype(w_out.dtype) @ w_out

    y_l = readout(outs_l[0] + outs_l[1], gt_l)
    y_c = readout(outs_c[0] + outs_c[1], gt_c) if ctx_out else None
    return y_l, y_c


def _grouped_moe(h, router_w, router_b, w_gate, w_up, w_down):
    n_tok, d = h.shape
    scores = jax.nn.sigmoid(h.astype(jnp.float32) @ router_w.astype(jnp.float32))
    sel = (scores + router_b.astype(jnp.float32)).reshape(n_tok, N_GROUPS, EXPERTS_PER_GROUP)
    group_score = lax.top_k(sel, TOP_K)[0].sum(-1)
    group = jnp.argmax(group_score, axis=-1)
    in_group = jnp.take_along_axis(sel, group[:, None, None], axis=1)[:, 0]
    local = lax.top_k(in_group, TOP_K)[1]
    expert = (group[:, None] * EXPERTS_PER_GROUP + local).astype(jnp.int32)
    gate = jnp.take_along_axis(scores, expert, axis=1)
    gate = (gate / gate.sum(-1, keepdims=True)).astype(h.dtype)

    n_pairs = n_tok * TOP_K
    e_flat = expert.reshape(-1)
    order = jnp.argsort(e_flat)
    e_sorted = e_flat[order]
    tok_sorted = (order // TOP_K).astype(jnp.int32)
    gate_sorted = gate.reshape(-1)[order]
    counts = jnp.bincount(e_flat, length=N_EXPERTS)
    padded = (counts + EXPERT_BLOCK - 1) // EXPERT_BLOCK * EXPERT_BLOCK
    start = jnp.cumsum(counts) - counts
    pend = jnp.cumsum(padded)
    pstart = pend - padded
    dest = pstart[e_sorted] + jnp.arange(n_pairs) - start[e_sorted]
    n_blocks = -(-(n_pairs + N_EXPERTS * (EXPERT_BLOCK - 1)) // EXPERT_BLOCK)
    n_rows = n_blocks * EXPERT_BLOCK
    tok_buf = jnp.full((n_rows,), n_tok, jnp.int32).at[dest].set(tok_sorted)
    gate_buf = jnp.zeros((n_rows,), h.dtype).at[dest].set(gate_sorted)
    block_expert = jnp.minimum(jnp.searchsorted(pend, jnp.arange(n_blocks) * EXPERT_BLOCK, side='right'), N_EXPERTS - 1)
    h_pad = jnp.concatenate([h, jnp.zeros((1, d), h.dtype)], axis=0)
    xb = h_pad[tok_buf].reshape(n_blocks, EXPERT_BLOCK, d)

    def expert_ffn(args):
        xe, e = args
        return (jax.nn.silu(xe @ w_gate[e]) * (xe @ w_up[e])) @ w_down[e]

    yb = lax.map(expert_ffn, (xb, block_expert)).reshape(n_rows, d)
    out = jnp.zeros((n_tok + 1, d), h.dtype).at[tok_buf].add(yb * gate_buf[:, None])
    return out[:n_tok]


def setup_inputs(seed: int = 0) -> dict:
    key = jax.random.key(seed)
    keys = iter(jax.random.split(key, 40))
    f32 = jnp.float32

    def nrm(shape, scale):
        return jax.random.normal(next(keys), shape, f32) * scale

    def gain(shape):
        return 1.0 + nrm(shape, 0.1)

    a_decay = jax.random.uniform(next(keys), (N_DN_LAYERS, 2, DN_HV), f32, 1.0, 16.0)
    dt = jnp.exp(jax.random.uniform(next(keys), (N_DN_LAYERS, 2, DN_HV), f32, math.log(1e-3), math.log(1e-1)))
    return {
        'x': nrm((BATCH, SEQ, D_MODEL), 1.0),
        'c': nrm((BATCH, D_MODEL), 1.0),
        'ctx': nrm((BATCH, CTX_LEN, D_MODEL), 1.0),
        'c_ctx': nrm((D_MODEL,), 1.0),
        'ada_w': nrm((DEPTH, D_MODEL, N_MOD * D_MODEL), 0.5 * D_MODEL ** -0.5),
        'ada_b': nrm((DEPTH, N_MOD * D_MODEL), 0.02),
        'ln_g': gain((DEPTH, 2, D_MODEL)),
        'ln_b': nrm((DEPTH, 2, D_MODEL), 0.02),
        'attn_w_in': nrm((N_ATTN_LAYERS, D_MODEL, ATTN_IN), D_MODEL ** -0.5),
        'attn_q_g': gain((N_ATTN_LAYERS, ATTN_HD)),
        'attn_k_g': gain((N_ATTN_LAYERS, ATTN_HD)),
        'attn_w_out': nrm((N_ATTN_LAYERS, ATTN_HQ * ATTN_HD, D_MODEL), DEEPNORM_BETA * (ATTN_HQ * ATTN_HD) ** -0.5),
        'dn_w_in': nrm((N_DN_LAYERS, D_MODEL, DN_IN), D_MODEL ** -0.5),
        'dn_conv': nrm((N_DN_LAYERS, DN_CONV_W, DN_CONV_CH), DN_CONV_W ** -0.5),
        'dn_a_log': jnp.log(a_decay),
        'dn_dt_bias': dt + jnp.log(-jnp.expm1(-dt)),
        'dn_norm_g': gain((N_DN_LAYERS, DN_DV)),
        'dn_w_out': nrm((N_DN_LAYERS, DN_V_W, D_MODEL), DEEPNORM_BETA * DN_V_W ** -0.5),
        'hg_w_in': nrm((N_HG_LAYERS, D_MODEL, HG_IN), D_MODEL ** -0.5),
        'hg_lb': gain((2, DEPTH, HG_W)),
        'hg_norm_g': gain((N_HG_LAYERS, HG_DV)),
        'hg_w_out': nrm((N_HG_LAYERS, HG_VW, D_MODEL), DEEPNORM_BETA * HG_VW ** -0.5),
        'router_w': nrm((D_MODEL, N_EXPERTS), D_MODEL ** -0.5),
        'router_b': nrm((N_EXPERTS,), 0.01),
        'moe_w_gate': nrm((DEPTH, N_EXPERTS, D_MODEL, D_EXPERT), D_MODEL ** -0.5),
        'moe_w_up': nrm((DEPTH, N_EXPERTS, D_MODEL, D_EXPERT), D_MODEL ** -0.5),
        'moe_w_down': nrm((DEPTH, N_EXPERTS, D_EXPERT, D_MODEL), DEEPNORM_BETA * D_EXPERT ** -0.5),
    }


def reference(x, c, ctx, c_ctx, ada_w, ada_b, ln_g, ln_b, attn_w_in, attn_q_g, attn_k_g, attn_w_out,
              dn_w_in, dn_conv, dn_a_log, dn_dt_bias, dn_norm_g, dn_w_out,
              hg_w_in, hg_lb, hg_norm_g, hg_w_out, router_w, router_b, moe_w_gate, moe_w_up, moe_w_down):
    b, l, d = x.shape
    lc = ctx.shape[1]
    lb_cum = jnp.cumsum(jax.nn.softmax(hg_lb.astype(jnp.float32), axis=1), axis=1)
    lb_cum = lb_cum - lb_cum[:, :1]
    for i in range(DEPTH):
        last = i == DEPTH - 1
        mod_l = (jax.nn.silu(c) @ ada_w[i] + ada_b[i]).reshape(b, N_MOD, 1, d)
        mod_c = (jax.nn.silu(c_ctx) @ ada_w[i] + ada_b[i]).reshape(N_MOD, d)
        h_l = x * (1.0 + mod_l[:, 1]) + mod_l[:, 0]
        h_c = ctx * (1.0 + mod_c[1]) + mod_c[0]
        kind, j = i % N_MIXERS, i // N_MIXERS
        if kind == 0:
            y_l, y_c = attention_mixer(h_l, h_c, attn_w_in[j], attn_q_g[j], attn_k_g[j], attn_w_out[j], not last)
        elif kind == 1:
            y_l, y_c = deltanet_mixer(h_l, h_c, dn_w_in[j], dn_conv[j], dn_a_log[j], dn_dt_bias[j],
                                      dn_norm_g[j], dn_w_out[j], not last)
        else:
            y_l, y_c = hgrn2_mixer(h_l, h_c, hg_w_in[j], lb_cum[:, i], hg_norm_g[j], hg_w_out[j], not last)
        x = _layer_norm(DEEPNORM_ALPHA * x + mod_l[:, 2] * y_l, ln_g[i, 0], ln_b[i, 0])
        h2_l = x * (1.0 + mod_l[:, 4]) + mod_l[:, 3]
        if last:
            f_l = _grouped_moe(h2_l.reshape(b * l, d), router_w, router_b,
                               moe_w_gate[i], moe_w_up[i], moe_w_down[i]).reshape(b, l, d)
        else:
            ctx = _layer_norm(DEEPNORM_ALPHA * ctx + mod_c[2] * y_c, ln_g[i, 0], ln_b[i, 0])
            h2_c = ctx * (1.0 + mod_c[4]) + mod_c[3]
            tokens = jnp.concatenate([h2_c.reshape(b * lc, d), h2_l.reshape(b * l, d)], axis=0)
            f = _grouped_moe(tokens, router_w, router_b, moe_w_gate[i], moe_w_up[i], moe_w_down[i])
            f_c = f[:b * lc].reshape(b, lc, d)
            f_l = f[b * lc:].reshape(b, l, d)
            ctx = _layer_norm(DEEPNORM_ALPHA * ctx + mod_c[5] * f_c, ln_g[i, 1], ln_b[i, 1])
        x = _layer_norm(DEEPNORM_ALPHA * x + mod_l[:, 5] * f_l, ln_g[i, 1], ln_b[i, 1])
    return x
```

```python
import functools
import math

import jax
import jax.numpy as jnp
from jax import lax
from jax.experimental import pallas as pl
from jax.experimental.pallas import tpu as pltpu

F32 = jnp.float32
BF16 = jnp.bfloat16

NORM_EPS = 1e-6
HEAD_DIM = 128
GRID_W = 64
ROPE_BASE = 10000.0
ROPE_FREQS = HEAD_DIM // 4
ATTN_GROUP = 4
DN_CONV_W = 5
DN_CHUNK = 64
HG_CHUNK = 16
N_EXPERTS = 16
N_GROUPS = 4
EXPERTS_PER_GROUP = N_EXPERTS // N_GROUPS
TOP_K = 2
EXPERT_BLOCK = 256
N_MOD = 6
VMEM_LIMIT = 56 * 1024 * 1024


def _cparams(*sem):
    return pltpu.CompilerParams(dimension_semantics=sem, vmem_limit_bytes=VMEM_LIMIT)


def _pick(n, prefs):
    for p in prefs:
        if n % p == 0:
            return p
    return n


def _bdot(a, b):
    return jnp.dot(a.astype(BF16), b.astype(BF16), preferred_element_type=F32)


def _bdot_nt(a, b):
    return lax.dot_general(a.astype(BF16), b.astype(BF16), (((1,), (1,)), ((), ())),
                           preferred_element_type=F32)


def _bdot_tn(a, b):
    return lax.dot_general(a.astype(BF16), b.astype(BF16), (((0,), (0,)), ((), ())),
                           preferred_element_type=F32)


def _silu(x):
    return x * jax.nn.sigmoid(x)


def _softplus(x):
    return jnp.maximum(x, 0.0) + jnp.log1p(jnp.exp(-jnp.abs(x)))


def _row_to_col(row, n):
    eye = lax.broadcasted_iota(jnp.int32, (n, n), 0) == lax.broadcasted_iota(jnp.int32, (n, n), 1)
    return jnp.sum(jnp.where(eye, row, 0.0), axis=1, keepdims=True)


def _ada_kernel(cc_ref, w_ref, b_ref, o_ref):
    a = _silu(cc_ref[...])
    o_ref[0] = _bdot(a, w_ref[0]) + b_ref[0]


def _ada_mod(cc, ada_w, ada_b):
    depth, d, n = ada_w.shape
    mp = cc.shape[0]
    tn = _pick(n, (1024, 512, 256, 128))
    return pl.pallas_call(
        _ada_kernel,
        grid=(depth, n // tn),
        in_specs=[pl.BlockSpec((mp, d), lambda i, j: (0, 0)),
                  pl.BlockSpec((1, d, tn), lambda i, j: (i, 0, j)),
                  pl.BlockSpec((1, 1, tn), lambda i, j: (i, 0, j))],
        out_specs=pl.BlockSpec((1, mp, tn), lambda i, j: (i, 0, j)),
        out_shape=jax.ShapeDtypeStruct((depth, mp, n), F32),
        compiler_params=_cparams("parallel", "parallel"),
        name="ada_mod",
    )(cc, ada_w, ada_b.reshape(depth, 1, n))


def _modulate_kernel(x_ref, mod_ref, h_ref, *, shift, scale):
    m = mod_ref[0]
    h_ref[...] = (x_ref[...] * (1.0 + m[scale:scale + 1]) + m[shift:shift + 1]).astype(h_ref.dtype)


def _mod_index(tiles_per_seq, ctx_tiles):
    def index(i):
        return ((i // tiles_per_seq) * 2 + jnp.where(i % tiles_per_seq < ctx_tiles, 0, 1), 0, 0)
    return index


def _modulate(x, modtab, tm, tiles_per_seq, ctx_tiles, shift, scale, dtype):
    m, d = x.shape
    midx = _mod_index(tiles_per_seq, ctx_tiles)
    return pl.pallas_call(
        functools.partial(_modulate_kernel, shift=shift, scale=scale),
        grid=(m // tm,),
        in_specs=[pl.BlockSpec((tm, d), lambda i: (i, 0)),
                  pl.BlockSpec((1, N_MOD, d), midx)],
        out_specs=pl.BlockSpec((tm, d), lambda i: (i, 0)),
        out_shape=jax.ShapeDtypeStruct((m, d), dtype),
        compiler_params=_cparams("parallel"),
        name="modulate",
    )(x, modtab)


def _proj_kernel(a_ref, w_ref, o_ref, wb_ref):
    @pl.when(pl.program_id(1) == 0)
    def _():
        wb_ref[...] = w_ref[...].astype(BF16)

    o_ref[...] = jnp.dot(a_ref[...], wb_ref[...], preferred_element_type=F32).astype(o_ref.dtype)


def _proj(a, w, col0, ncols, out_dtype=F32):
    m, k = a.shape
    tn = _pick(ncols, (1024, 512, 256, 128))
    assert col0 % tn == 0
    tm = _pick(m, (512, 256, 128, 64))
    c0 = col0 // tn
    return pl.pallas_call(
        _proj_kernel,
        grid=(ncols // tn, m // tm),
        in_specs=[pl.BlockSpec((tm, k), lambda j, i: (i, 0)),
                  pl.BlockSpec((k, tn), lambda j, i: (0, c0 + j))],
        out_specs=pl.BlockSpec((tm, tn), lambda j, i: (i, j)),
        out_shape=jax.ShapeDtypeStruct((m, ncols), out_dtype),
        scratch_shapes=[pltpu.VMEM((k, tn), BF16)],
        compiler_params=_cparams("parallel", "arbitrary"),
        name="proj",
    )(a, w)


def _ln_epilogue(y, x, mod, modn, lng, lnb, alpha, gate, shift, scale):
    r = alpha * x + mod[gate:gate + 1] * y
    rc = r - jnp.mean(r, axis=-1, keepdims=True)
    var = jnp.mean(rc * rc, axis=-1, keepdims=True)
    xn = rc * lax.rsqrt(var + NORM_EPS) * lng + lnb
    h = xn * (1.0 + modn[scale:scale + 1]) + modn[shift:shift + 1]
    return xn, h


def _out_ln_kernel(o_ref, w_ref, x_ref, mod_ref, modn_ref, lng_ref, lnb_ref, xn_ref, h_ref, acc_ref,
                   *, alpha, gate, shift, scale):
    kk = pl.program_id(1)

    @pl.when(kk == 0)
    def _():
        acc_ref[...] = jnp.zeros_like(acc_ref)

    acc_ref[...] += jnp.dot(o_ref[...], w_ref[...], preferred_element_type=F32)

    @pl.when(kk == pl.num_programs(1) - 1)
    def _():
        xn, h = _ln_epilogue(acc_ref[...], x_ref[...], mod_ref[0], modn_ref[0], lng_ref[...], lnb_ref[...],
                             alpha, gate, shift, scale)
        xn_ref[...] = xn
        h_ref[...] = h.astype(h_ref.dtype)


def _out_ln(o, w, x, mod, modn, lng, lnb, tm, tiles_per_seq, ctx_tiles, alpha, gate, shift, scale, h_dtype):
    m, k = o.shape
    d = w.shape[1]
    tk = _pick(k, (512, 256, 128))
    midx = _mod_index(tiles_per_seq, ctx_tiles)
    return pl.pallas_call(
        functools.partial(_out_ln_kernel, alpha=alpha, gate=gate, shift=shift, scale=scale),
        grid=(m // tm, k // tk),
        in_specs=[pl.BlockSpec((tm, tk), lambda i, j: (i, j)),
                  pl.BlockSpec((tk, d), lambda i, j: (j, 0)),
                  pl.BlockSpec((tm, d), lambda i, j: (i, 0)),
                  pl.BlockSpec((1, N_MOD, d), lambda i, j: midx(i)),
                  pl.BlockSpec((1, N_MOD, d), lambda i, j: midx(i)),
                  pl.BlockSpec((1, d), lambda i, j: (0, 0)),
                  pl.BlockSpec((1, d), lambda i, j: (0, 0))],
        out_specs=[pl.BlockSpec((tm, d), lambda i, j: (i, 0)),
                   pl.BlockSpec((tm, d), lambda i, j: (i, 0))],
        out_shape=[jax.ShapeDtypeStruct((m, d), F32), jax.ShapeDtypeStruct((m, d), h_dtype)],
        scratch_shapes=[pltpu.VMEM((tm, d), F32)],
        compiler_params=_cparams("parallel", "arbitrary"),
        name="out_ln",
    )(o, w, x, mod, modn, lng, lnb)


def _comb_ln_kernel(f_ref, x_ref, mod_ref, modn_ref, lng_ref, lnb_ref, xn_ref, h_ref,
                    *, alpha, gate, shift, scale):
    y = f_ref[0] + f_ref[1]
    xn, h = _ln_epilogue(y, x_ref[...], mod_ref[0], modn_ref[0], lng_ref[...], lnb_ref[...],
                         alpha, gate, shift, scale)
    xn_ref[...] = xn
    h_ref[...] = h.astype(h_ref.dtype)


def _comb_ln(f, x, mod, modn, lng, lnb, tm, tiles_per_seq, ctx_tiles, alpha, gate, shift, scale, h_dtype):
    m, d = x.shape
    midx = _mod_index(tiles_per_seq, ctx_tiles)
    return pl.pallas_call(
        functools.partial(_comb_ln_kernel, alpha=alpha, gate=gate, shift=shift, scale=scale),
        grid=(m // tm,),
        in_specs=[pl.BlockSpec((2, tm, d), lambda i: (0, i, 0)),
                  pl.BlockSpec((tm, d), lambda i: (i, 0)),
                  pl.BlockSpec((1, N_MOD, d), midx),
                  pl.BlockSpec((1, N_MOD, d), midx),
                  pl.BlockSpec((1, d), lambda i: (0, 0)),
                  pl.BlockSpec((1, d), lambda i: (0, 0))],
        out_specs=[pl.BlockSpec((tm, d), lambda i: (i, 0)),
                   pl.BlockSpec((tm, d), lambda i: (i, 0))],
        out_shape=[jax.ShapeDtypeStruct((m, d), F32), jax.ShapeDtypeStruct((m, d), h_dtype)],
        compiler_params=_cparams("parallel"),
        name="comb_ln",
    )(f, x, mod, modn, lng, lnb)


def _rope_tables(lc, l):
    pos = jnp.arange(l)
    row = (pos // GRID_W).astype(F32)
    col = (pos % GRID_W).astype(F32)
    inv_freq = ROPE_BASE ** (-jnp.arange(ROPE_FREQS, dtype=F32) / ROPE_FREQS)
    ar = row[:, None] * inv_freq
    ac = col[:, None] * inv_freq
    cos = jnp.concatenate([jnp.cos(ar), jnp.cos(ar), jnp.cos(ac), jnp.cos(ac)], axis=1)
    sin = jnp.concatenate([-jnp.sin(ar), jnp.sin(ar), -jnp.sin(ac), jnp.sin(ac)], axis=1)
    cos = jnp.concatenate([jnp.ones((lc, HEAD_DIM), F32), cos], axis=0)
    sin = jnp.concatenate([jnp.zeros((lc, HEAD_DIM), F32), sin], axis=0)
    return cos, sin


def _norm_rope(x, g, cos, sin):
    y = x * lax.rsqrt(jnp.mean(x * x, axis=-1, keepdims=True) + NORM_EPS) * g
    lane = lax.broadcasted_iota(jnp.int32, y.shape, 1)
    partner = jnp.where(lane % (2 * ROPE_FREQS) < ROPE_FREQS,
                        pltpu.roll(y, HEAD_DIM - ROPE_FREQS, 1), pltpu.roll(y, ROPE_FREQS, 1))
    return y * cos + partner * sin


def _attn_kernel(q_ref, k_ref, v_ref, qg_ref, kg_ref, cos_ref, sin_ref, o_ref, kb_ref, vb_ref,
                 *, lc, tq, t):
    j = pl.program_id(2)

    @pl.when(j == 0)
    def _():
        kb_ref[...] = _norm_rope(k_ref[0], kg_ref[...], cos_ref[...], sin_ref[...]).astype(BF16)
        vb_ref[...] = v_ref[0].astype(BF16)

    r0 = pl.multiple_of(j * tq, tq)
    cos = cos_ref[pl.ds(r0, tq), :]
    sin = sin_ref[pl.ds(r0, tq), :]
    scale = HEAD_DIM ** -0.5

    def attend(n_keys):
        for g in range(ATTN_GROUP):
            qh = _norm_rope(q_ref[0, :, g * HEAD_DIM:(g + 1) * HEAD_DIM], qg_ref[...], cos, sin)
            s = _bdot_nt(qh, kb_ref[0:n_keys, :])
            e = jnp.exp((s - jnp.max(s, axis=-1, keepdims=True)) * scale)
            den = jnp.sum(e, axis=-1, keepdims=True)
            o = jnp.dot(e.astype(BF16), vb_ref[0:n_keys, :], preferred_element_type=F32) / den
            o_ref[0, :, g * HEAD_DIM:(g + 1) * HEAD_DIM] = o.astype(o_ref.dtype)

    @pl.when(j < lc // tq)
    def _():
        attend(lc)

    @pl.when(j >= lc // tq)
    def _():
        attend(t)


def _attention(p, qg, kg, cos, sin, lc, tq):
    b, t, n = p.shape
    hkv = n // HEAD_DIM // (ATTN_GROUP + 2)
    hq = hkv * ATTN_GROUP
    gw = ATTN_GROUP * HEAD_DIM
    return pl.pallas_call(
        functools.partial(_attn_kernel, lc=lc, tq=tq, t=t),
        grid=(b, hkv, t // tq),
        in_specs=[pl.BlockSpec((1, tq, gw), lambda bi, h, j: (bi, j, h)),
                  pl.BlockSpec((1, t, HEAD_DIM), lambda bi, h, j: (bi, 0, hq + h)),
                  pl.BlockSpec((1, t, HEAD_DIM), lambda bi, h, j: (bi, 0, hq + hkv + h)),
                  pl.BlockSpec((1, HEAD_DIM), lambda bi, h, j: (0, 0)),
                  pl.BlockSpec((1, HEAD_DIM), lambda bi, h, j: (0, 0)),
                  pl.BlockSpec((t, HEAD_DIM), lambda bi, h, j: (0, 0)),
                  pl.BlockSpec((t, HEAD_DIM), lambda bi, h, j: (0, 0))],
        out_specs=pl.BlockSpec((1, tq, gw), lambda bi, h, j: (bi, j, h)),
        out_shape=jax.ShapeDtypeStruct((b, t, hq * HEAD_DIM), BF16),
        scratch_shapes=[pltpu.VMEM((t, HEAD_DIM), BF16), pltpu.VMEM((t, HEAD_DIM), BF16)],
        compiler_params=_cparams("parallel", "parallel", "arbitrary"),
        name="attention",
    )(p, p, p, qg, kg, cos, sin)


def _dn_prep_kernel(p_ref, cw_ref, o_ref, *, lc, t, n_qk_tiles):
    c = pl.program_id(1)
    x = p_ref[0]
    row = lax.broadcasted_iota(jnp.int32, x.shape, 0)
    seg = row >= lc
    acc = jnp.zeros_like(x)
    for jj in range(DN_CONV_W):
        off = jj - DN_CONV_W // 2
        if off == 0:
            sh = x
        else:
            src = row + off
            ok = (src >= 0) & (src < t) & ((src >= lc) == seg)
            sh = jnp.where(ok, pltpu.roll(x, (-off) % t, 0), 0.0)
        acc = acc + sh * cw_ref[jj:jj + 1, :]
    y = _silu(acc)
    nrm = lax.rsqrt(jnp.sum(y * y, axis=-1, keepdims=True) + NORM_EPS)
    mult = jnp.where(c < n_qk_tiles, nrm * (HEAD_DIM ** -0.5), jnp.where(c < 2 * n_qk_tiles, nrm, 1.0))
    o_ref[0] = y * mult


def _dn_prep(p, conv_w, lc, n_qk_tiles):
    b, t, _ = p.shape
    nch = conv_w.shape[1]
    return pl.pallas_call(
        functools.partial(_dn_prep_kernel, lc=lc, t=t, n_qk_tiles=n_qk_tiles),
        grid=(b, nch // HEAD_DIM),
        in_specs=[pl.BlockSpec((1, t, HEAD_DIM), lambda bi, c: (bi, 0, c)),
                  pl.BlockSpec((DN_CONV_W, HEAD_DIM), lambda bi, c: (0, c))],
        out_specs=pl.BlockSpec((1, t, HEAD_DIM), lambda bi, c: (bi, 0, c)),
        out_shape=jax.ShapeDtypeStruct((b, t, nch), F32),
        compiler_params=_cparams("parallel", "parallel"),
        name="dn_prep",
    )(p, conv_w)


def _dn_gates_kernel(ba_ref, alog_ref, dtb_ref, o_ref, *, t, hv):
    x = ba_ref[0]
    lane = lax.broadcasted_iota(jnp.int32, (DN_CHUNK, 4 * hv), 1)
    g = -jnp.exp(alog_ref[...]) * _softplus(x + dtb_ref[...])
    ii = lax.broadcasted_iota(jnp.int32, (DN_CHUNK, DN_CHUNK), 0)
    jj = lax.broadcasted_iota(jnp.int32, (DN_CHUNK, DN_CHUNK), 1)
    lower = (ii >= jj).astype(F32)
    upper = (ii <= jj).astype(F32)
    for c in range(t // DN_CHUNK):
        rows = slice(c * DN_CHUNK, (c + 1) * DN_CHUNK)
        gch = g[rows]
        pre = jnp.dot(lower, gch, precision=lax.Precision.HIGHEST, preferred_element_type=F32)
        suf = jnp.dot(upper, gch, precision=lax.Precision.HIGHEST, preferred_element_type=F32)
        o_ref[0, rows, :] = jnp.where(lane < 2 * hv, jax.nn.sigmoid(x[rows]),
                                      jnp.where(lane < 3 * hv, pre, suf))


def _dn_gates(ba, a_log, dt_bias):
    b, t, n = ba.shape
    hv = n // 4
    zeros = jnp.zeros((1, 2 * hv), F32)
    alog = jnp.concatenate([zeros, a_log.reshape(1, 2 * hv)], axis=1)
    dtb = jnp.concatenate([zeros, dt_bias.reshape(1, 2 * hv)], axis=1)
    return pl.pallas_call(
        functools.partial(_dn_gates_kernel, t=t, hv=hv),
        grid=(b,),
        in_specs=[pl.BlockSpec((1, t, n), lambda bi: (bi, 0, 0)),
                  pl.BlockSpec((1, n), lambda bi: (0, 0)),
                  pl.BlockSpec((1, n), lambda bi: (0, 0))],
        out_specs=pl.BlockSpec((1, t, n), lambda bi: (bi, 0, 0)),
        out_shape=jax.ShapeDtypeStruct((b, t, n), F32),
        compiler_params=_cparams("parallel"),
        name="dn_gates",
    )(ba, alog, dtb)


def _dn_chunk(qc, kc, vc, beta_r, gc_r, s, reverse):
    c = DN_CHUNK
    ii = lax.broadcasted_iota(jnp.int32, (c, c), 0)
    jj = lax.broadcasted_iota(jnp.int32, (c, c), 1)
    incl = (ii <= jj) if reverse else (ii >= jj)
    strict = (ii < jj) if reverse else (ii > jj)
    beta_c = _row_to_col(beta_r, c)
    gc_c = _row_to_col(gc_r, c)
    g_end = gc_r[:, 0:1] if reverse else gc_r[:, c - 1:c]
    decay = jnp.exp(jnp.where(incl, gc_c - gc_r, -jnp.inf))
    kk = _bdot_nt(kc, kc)
    m = jnp.where(strict, beta_c * kk * decay, 0.0)
    eg = jnp.exp(gc_c)
    x = jnp.concatenate([vc * beta_c, kc * (beta_c * eg)], axis=1)
    x = x - _bdot(m, x)
    pw = m
    for _ in range(int(math.log2(c)) - 1):
        pw = _bdot(pw, pw)
        x = x + _bdot(pw, x)
    u, w = x[:, :HEAD_DIM], x[:, HEAD_DIM:]
    qk = jnp.where(incl, _bdot_nt(qc, kc) * decay, 0.0)
    v_new = u - _bdot(w, s)
    o = _bdot(qc * eg, s) + _bdot(qk, v_new)
    s = s * jnp.exp(g_end) + _bdot_tn(kc * jnp.exp(g_end - gc_c), v_new)
    return o, s


def _dn_core_kernel(q_ref, k_ref, v_ref, z_ref, gt_ref, ng_ref, o_ref, acc_ref, *, lc, t, hv):
    qh = pl.program_id(1)
    nc = t // DN_CHUNK
    ncc = lc // DN_CHUNK
    for sub in range(2):
        head = qh * 2 + sub
        cols = slice(sub * HEAD_DIM, (sub + 1) * HEAD_DIM)
        for d in range(2):
            def body(n, s, d=d, cols=cols, head=head):
                if d == 0:
                    c = n
                else:
                    c = jnp.where(n < ncc, ncc - 1 - n, nc - 1 - (n - ncc))
                r0 = pl.multiple_of(c * DN_CHUNK, DN_CHUNK)
                rows = pl.ds(r0, DN_CHUNK)
                beta_r = gt_ref[0, d * hv + head, pl.ds(c, 1), :]
                gc_r = gt_ref[0, (2 + d) * hv + head, pl.ds(c, 1), :]
                o, s = _dn_chunk(q_ref[0, rows, :], k_ref[0, rows, :], v_ref[0, rows, cols],
                                 beta_r, gc_r, s, reverse=(d == 1))
                if d == 0:
                    acc_ref[rows, cols] = o
                else:
                    acc_ref[rows, cols] += o
                return s

            lax.fori_loop(0, nc, body, jnp.zeros((HEAD_DIM, HEAD_DIM), F32))
    for sub in range(2):
        cols = slice(sub * HEAD_DIM, (sub + 1) * HEAD_DIM)
        o = acc_ref[:, cols]
        o = o * lax.rsqrt(jnp.mean(o * o, axis=-1, keepdims=True) + NORM_EPS) * ng_ref[...]
        o_ref[0, :, cols] = (o * _silu(z_ref[0, :, cols])).astype(o_ref.dtype)


def _dn_core(qkv, pz, gt, norm_g, lc, n_qk_heads, z_col0):
    b, t, _ = qkv.shape
    hv = 2 * n_qk_heads
    vw = 2 * HEAD_DIM
    nc = t // DN_CHUNK
    return pl.pallas_call(
        functools.partial(_dn_core_kernel, lc=lc, t=t, hv=hv),
        grid=(b, n_qk_heads),
        in_specs=[pl.BlockSpec((1, t, HEAD_DIM), lambda bi, h: (bi, 0, h)),
                  pl.BlockSpec((1, t, HEAD_DIM), lambda bi, h: (bi, 0, n_qk_heads + h)),
                  pl.BlockSpec((1, t, vw), lambda bi, h: (bi, 0, n_qk_heads + h)),
                  pl.BlockSpec((1, t, vw), lambda bi, h: (bi, 0, z_col0 // vw + h)),
                  pl.BlockSpec((1, 4 * hv, nc, DN_CHUNK), lambda bi, h: (bi, 0, 0, 0)),
                  pl.BlockSpec((1, HEAD_DIM), lambda bi, h: (0, 0))],
        out_specs=pl.BlockSpec((1, t, vw), lambda bi, h: (bi, 0, h)),
        out_shape=jax.ShapeDtypeStruct((b, t, hv * HEAD_DIM), BF16),
        scratch_shapes=[pltpu.VMEM((t, vw), F32)],
        compiler_params=_cparams("parallel", "parallel"),
        name="dn_core",
    )(qkv, qkv, qkv, pz, gt, norm_g)


def _chunk_scan(x, reverse):
    n = x.shape[0]
    pos = lax.broadcasted_iota(jnp.int32, x.shape, 0) % HG_CHUNK
    sh = 1
    while sh < HG_CHUNK:
        if reverse:
            x = x + jnp.where(pos < HG_CHUNK - sh, pltpu.roll(x, n - sh, 0), 0.0)
        else:
            x = x + jnp.where(pos >= sh, pltpu.roll(x, sh, 0), 0.0)
        sh *= 2
    return x


def _hg_kernel(q_ref, i_ref, gate_ref, ff_ref, fb_ref, lb_ref, ng_ref, o_ref,
               qd_ref, ki_ref, kd_ref, fl_ref, acc_ref, *, lc, t):
    c16 = HG_CHUNK
    nc = t // c16
    ncc = lc // c16
    ii = lax.broadcasted_iota(jnp.int32, (c16, c16), 0)
    jj = lax.broadcasted_iota(jnp.int32, (c16, c16), 1)
    for d in range(2):
        f_ref = ff_ref if d == 0 else fb_ref
        lb = lb_ref[d:d + 1, :]
        fg = lb + (1.0 - lb) * jax.nn.sigmoid(f_ref[0])
        keys = 1.0 - fg
        logf = jnp.log(fg)
        pre = _chunk_scan(logf, False)
        suf = _chunk_scan(logf, True)
        gcum = suf if d == 1 else pre
        total = pre + suf - logf
        qd_ref[...] = q_ref[0] * jnp.exp(gcum)
        ki_ref[...] = keys * jnp.exp(-gcum)
        kd_ref[...] = keys * jnp.exp(total - gcum)
        fl_ref[...] = jnp.exp(total)
        incl = (ii <= jj) if d == 1 else (ii >= jj)

        def body(n, st, d=d, incl=incl):
            if d == 0:
                c = n
            else:
                c = jnp.where(n < ncc, ncc - 1 - n, nc - 1 - (n - ncc))
            r0 = pl.multiple_of(c * c16, c16)
            rows = pl.ds(r0, c16)
            qd = qd_ref[rows, :]
            vv = i_ref[0, rows, :]
            intra = jnp.where(incl, _bdot_nt(qd, ki_ref[rows, :]), 0.0)
            o = _bdot(intra, vv) + _bdot_nt(qd, st)
            if d == 0:
                acc_ref[rows, :] = o
            else:
                acc_ref[rows, :] += o
            return st * fl_ref[pl.ds(r0, 1), :] + _bdot_tn(vv, kd_ref[rows, :])

        lax.fori_loop(0, nc, body, jnp.zeros((HEAD_DIM, HEAD_DIM), F32))
    o = acc_ref[...]
    o = o * lax.rsqrt(jnp.mean(o * o, axis=-1, keepdims=True) + NORM_EPS) * ng_ref[...]
    o_ref[0] = (o * jax.nn.sigmoid(gate_ref[0])).astype(o_ref.dtype)


def _hgrn2(p, lb, norm_g, lc, n_heads):
    b, t, _ = p.shape
    blk = lambda k: pl.BlockSpec((1, t, HEAD_DIM), lambda bi, h, k=k: (bi, 0, k * n_heads + h))
    return pl.pallas_call(
        functools.partial(_hg_kernel, lc=lc, t=t),
        grid=(b, n_heads),
        in_specs=[blk(0), blk(1), blk(2), blk(3), blk(4),
                  pl.BlockSpec((2, HEAD_DIM), lambda bi, h: (0, h)),
                  pl.BlockSpec((1, HEAD_DIM), lambda bi, h: (0, 0))],
        out_specs=pl.BlockSpec((1, t, HEAD_DIM), lambda bi, h: (bi, 0, h)),
        out_shape=jax.ShapeDtypeStruct((b, t, n_heads * HEAD_DIM), BF16),
        scratch_shapes=[pltpu.VMEM((t, HEAD_DIM), F32) for _ in range(5)],
        compiler_params=_cparams("parallel", "parallel"),
        name="hgrn2",
    )(p, p, p, p, p, lb, norm_g)


def _split_bf16(x):
    hi = x.astype(BF16)
    return hi, (x - hi.astype(F32)).astype(BF16)


def _router_kernel(h_ref, w_ref, b_ref, e_ref, g_ref):
    h_hi, h_lo = _split_bf16(h_ref[...])
    w_hi, w_lo = _split_bf16(w_ref[...])
    nt = (((1,), (1,)), ((), ()))
    logits = (lax.dot_general(w_hi, h_hi, nt, preferred_element_type=F32)
              + lax.dot_general(w_hi, h_lo, nt, preferred_element_type=F32)
              + lax.dot_general(w_lo, h_hi, nt, preferred_element_type=F32))
    scores = jax.nn.sigmoid(logits)
    sel = scores + b_ref[...]
    epg = EXPERTS_PER_GROUP
    rows = lambda a, e: a[e:e + 1, :]
    best_gs = None
    for g in range(N_GROUPS):
        gs = None
        for a in range(epg):
            for c in range(a + 1, epg):
                pair = rows(sel, g * epg + a) + rows(sel, g * epg + c)
                gs = pair if gs is None else jnp.maximum(gs, pair)
        if best_gs is None:
            best_gs, group = gs, jnp.zeros(gs.shape, jnp.int32)
        else:
            upd = gs > best_gs
            best_gs = jnp.where(upd, gs, best_gs)
            group = jnp.where(upd, g, group)

    def in_group(a, l):
        v = rows(a, l)
        for g in range(1, N_GROUPS):
            v = jnp.where(group == g, rows(a, g * epg + l), v)
        return v

    sel_l = [in_group(sel, l) for l in range(epg)]
    sc_l = [in_group(scores, l) for l in range(epg)]

    def top1(exclude):
        best, idx, sc = None, None, None
        for l in range(epg):
            v = sel_l[l] if exclude is None else jnp.where(exclude == l, -jnp.inf, sel_l[l])
            if best is None:
                best, idx, sc = v, jnp.zeros(v.shape, jnp.int32), sc_l[l]
            else:
                upd = v > best
                best = jnp.where(upd, v, best)
                idx = jnp.where(upd, l, idx)
                sc = jnp.where(upd, sc_l[l], sc)
        return idx, sc

    l0, s0 = top1(None)
    l1, s1 = top1(l0)
    e_ref[0:1, :] = group * epg + l0
    e_ref[1:2, :] = group * epg + l1
    den = s0 + s1
    g_ref[0:1, :] = s0 / den
    g_ref[1:2, :] = s1 / den


def _router(h, router_w, router_b):
    m, d = h.shape
    tm = _pick(m, (512, 256, 128))
    ne = router_w.shape[1]
    return pl.pallas_call(
        _router_kernel,
        grid=(m // tm,),
        in_specs=[pl.BlockSpec((tm, d), lambda i: (i, 0)),
                  pl.BlockSpec((ne, d), lambda i: (0, 0)),
                  pl.BlockSpec((ne, 1), lambda i: (0, 0))],
        out_specs=[pl.BlockSpec((TOP_K, tm), lambda i: (0, i)),
                   pl.BlockSpec((TOP_K, tm), lambda i: (0, i))],
        out_shape=[jax.ShapeDtypeStruct((TOP_K, m), jnp.int32), jax.ShapeDtypeStruct((TOP_K, m), F32)],
        compiler_params=_cparams("parallel"),
        name="router",
    )(h, router_w.T, router_b.reshape(ne, 1))


def _moe_ffn_kernel(bexp_ref, bact_ref, code_ref, h_hbm, gate_ref, wg_ref, wu_ref, wd_ref, f_hbm,
                    xbuf, ybuf, sem_in, sem_out, *, n_tok):
    i = pl.program_id(0)
    base = i * EXPERT_BLOCK

    def row_in(r, tok):
        return pltpu.make_async_copy(h_hbm.at[pl.ds(tok, 1)], xbuf.at[pl.ds(r, 1)], sem_in)

    def row_out(r, dst):
        return pltpu.make_async_copy(ybuf.at[pl.ds(r, 1)], f_hbm.at[pl.ds(dst, 1)], sem_out)

    @pl.when(bact_ref[i] != 0)
    def _():
        def start_in(r, carry):
            code = code_ref[base + r]

            @pl.when(code >= 0)
            def _():
                row_in(r, code // TOP_K).start()

            @pl.when(code < 0)
            def _():
                xbuf[pl.ds(r, 1), :] = jnp.zeros((1, xbuf.shape[1]), xbuf.dtype)

            return carry

        lax.fori_loop(0, EXPERT_BLOCK, start_in, 0)

        def wait_in(r, carry):
            code = code_ref[base + r]

            @pl.when(code >= 0)
            def _():
                row_in(r, code // TOP_K).wait()

            return carry

        lax.fori_loop(0, EXPERT_BLOCK, wait_in, 0)

        x = xbuf[...].astype(BF16)
        a = _silu(jnp.dot(x, wg_ref[0], preferred_element_type=F32)) \
            * jnp.dot(x, wu_ref[0], preferred_element_type=F32)
        y = jnp.dot(a.astype(BF16), wd_ref[0], preferred_element_type=F32)
        ybuf[...] = y * gate_ref[:, 0:1]

        def start_out(r, carry):
            code = code_ref[base + r]

            @pl.when(code >= 0)
            def _():
                row_out(r, (code % TOP_K) * n_tok + code // TOP_K).start()

            return carry

        lax.fori_loop(0, EXPERT_BLOCK, start_out, 0)

        def wait_out(r, carry):
            code = code_ref[base + r]

            @pl.when(code >= 0)
            def _():
                row_out(r, (code % TOP_K) * n_tok + code // TOP_K).wait()

            return carry

        lax.fori_loop(0, EXPERT_BLOCK, wait_out, 0)


def _moe_ffn(h, block_expert, block_active, code_buf, gate_buf, wg, wu, wd):
    m, d = h.shape
    de = wg.shape[2]
    nb = block_expert.shape[0]
    grid_spec = pltpu.PrefetchScalarGridSpec(
        num_scalar_prefetch=3,
        grid=(nb,),
        in_specs=[pl.BlockSpec(memory_space=pl.ANY),
                  pl.BlockSpec((EXPERT_BLOCK, HEAD_DIM), lambda i, be, ba, cb: (i, 0)),
                  pl.BlockSpec((1, d, de), lambda i, be, ba, cb: (be[i], 0, 0)),
                  pl.BlockSpec((1, d, de), lambda i, be, ba, cb: (be[i], 0, 0)),
                  pl.BlockSpec((1, de, d), lambda i, be, ba, cb: (be[i], 0, 0))],
        out_specs=pl.BlockSpec(memory_space=pl.ANY),
        scratch_shapes=[pltpu.VMEM((EXPERT_BLOCK, d), F32), pltpu.VMEM((EXPERT_BLOCK, d), F32),
                        pltpu.SemaphoreType.DMA, pltpu.SemaphoreType.DMA],
    )
    return pl.pallas_call(
        functools.partial(_moe_ffn_kernel, n_tok=m),
        grid_spec=grid_spec,
        out_shape=jax.ShapeDtypeStruct((TOP_K * m, d), F32),
        compiler_params=_cparams("arbitrary"),
        name="moe_ffn",
    )(block_expert, block_active, code_buf, h, gate_buf, wg, wu, wd)


def _moe_dispatch(expert, gate):
    m = expert.shape[1]
    n_pairs = m * TOP_K
    e_flat = expert.T.reshape(-1)
    onehot = (e_flat[:, None] == jnp.arange(N_EXPERTS)[None, :]).astype(jnp.int32)
    csum = jnp.cumsum(onehot, axis=0)
    rank = jnp.take_along_axis(csum, e_flat[:, None], axis=1)[:, 0] - 1
    counts = csum[-1]
    padded = (counts + EXPERT_BLOCK - 1) // EXPERT_BLOCK * EXPERT_BLOCK
    pend = jnp.cumsum(padded)
    pstart = pend - padded
    dest = pstart[e_flat] + rank
    n_blocks = -(-(n_pairs + N_EXPERTS * (EXPERT_BLOCK - 1)) // EXPERT_BLOCK)
    n_rows = n_blocks * EXPERT_BLOCK
    code_buf = jnp.full((n_rows,), -1, jnp.int32).at[dest].set(jnp.arange(n_pairs, dtype=jnp.int32))
    gate_buf = jnp.zeros((n_rows,), F32).at[dest].set(gate.T.reshape(-1))
    blk_start = jnp.arange(n_blocks) * EXPERT_BLOCK
    block_expert = jnp.minimum(jnp.searchsorted(pend, blk_start, side='right'), N_EXPERTS - 1).astype(jnp.int32)
    block_active = (blk_start < pend[-1]).astype(jnp.int32)
    gate_buf = jnp.broadcast_to(gate_buf[:, None], (n_rows, HEAD_DIM))
    return block_expert, block_active, code_buf, gate_buf


def _moe(h2, router_w, router_b, wg, wu, wd):
    expert, gate = _router(h2, router_w, router_b)
    block_expert, block_active, code_buf, gate_buf = _moe_dispatch(expert, gate)
    f = _moe_ffn(h2, block_expert, block_active, code_buf, gate_buf, wg, wu, wd)
    return f.reshape(TOP_K, h2.shape[0], h2.shape[1])


def kernel(x, c, ctx, c_ctx, ada_w, ada_b, ln_g, ln_b, attn_w_in, attn_q_g, attn_k_g, attn_w_out, dn_w_in, dn_conv, dn_a_log, dn_dt_bias, dn_norm_g, dn_w_out, hg_w_in, hg_lb, hg_norm_g, hg_w_out, router_w, router_b, moe_w_gate, moe_w_up, moe_w_down):
    b, l, d = x.shape
    lc = ctx.shape[1]
    t = lc + l
    m = b * t
    depth = ada_w.shape[0]
    alpha = (2.0 * depth) ** 0.25
    tm = _pick(math.gcd(lc, l), (256, 128, 64))
    tiles_per_seq, ctx_tiles = t // tm, lc // tm
    n_heads = d // HEAD_DIM

    mp = -(-(b + 1) // 8) * 8
    cc = jnp.concatenate([c, c_ctx[None, :], jnp.zeros((mp - b - 1, d), F32)], axis=0)
    mod = _ada_mod(cc, ada_w, ada_b)
    mod_l = mod[:, :b].reshape(depth, b, 1, N_MOD, d)
    mod_c = jnp.broadcast_to(mod[:, b].reshape(depth, 1, 1, N_MOD, d), (depth, b, 1, N_MOD, d))
    modtab = jnp.concatenate([mod_c, mod_l], axis=2).reshape(depth, 2 * b, N_MOD, d)

    lb_cum = jnp.cumsum(jax.nn.softmax(hg_lb.astype(F32), axis=1), axis=1)
    lb_cum = lb_cum - lb_cum[:, :1]

    xs = jnp.concatenate([ctx, x], axis=1).reshape(m, d)
    h = _modulate(xs, modtab[0], tm, tiles_per_seq, ctx_tiles, 0, 1, BF16)
    cos, sin = _rope_tables(lc, l)
    rows = (tm, tiles_per_seq, ctx_tiles)

    for i in range(depth):
        kind, j = i % 3, i // 3
        if kind == 0:
            p = _proj(h, attn_w_in[j], 0, attn_w_in.shape[2])
            o = _attention(p.reshape(b, t, -1), attn_q_g[j][None, :], attn_k_g[j][None, :], cos, sin, lc, tm)
            w_out = attn_w_out[j]
        elif kind == 1:
            n_conv = dn_conv.shape[2]
            hv = dn_a_log.shape[2]
            n_main = n_conv + hv * HEAD_DIM
            p = _proj(h, dn_w_in[j], 0, n_main).reshape(b, t, n_main)
            ba = _proj(h, dn_w_in[j], n_main, 4 * hv).reshape(b, t, 4 * hv)
            qkv = _dn_prep(p, dn_conv[j], lc, hv // 2)
            gates = _dn_gates(ba, dn_a_log[j], dn_dt_bias[j])
            gt = jnp.swapaxes(gates, 1, 2).reshape(b, 4 * hv, t // DN_CHUNK, DN_CHUNK)
            o = _dn_core(qkv, p, gt, dn_norm_g[j][None, :], lc, hv // 2, n_conv)
            w_out = dn_w_out[j]
        else:
            p = _proj(h, hg_w_in[j], 0, hg_w_in.shape[2]).reshape(b, t, -1)
            o = _hgrn2(p, lb_cum[:, i], hg_norm_g[j][None, :], lc, n_heads)
            w_out = hg_w_out[j]
        xs, h2 = _out_ln(o.reshape(m, -1), w_out.astype(BF16), xs, modtab[i], modtab[i],
                         ln_g[i, 0][None, :], ln_b[i, 0][None, :], *rows, alpha, 2, 3, 4, F32)
        f = _moe(h2, router_w, router_b, moe_w_gate[i].astype(BF16), moe_w_up[i].astype(BF16),
                 moe_w_down[i].astype(BF16))
        nxt = modtab[min(i + 1, depth - 1)]
        xs, h = _comb_ln(f, xs, modtab[i], nxt, ln_g[i, 1][None, :], ln_b[i, 1][None, :],
                         *rows, alpha, 5, 0, 1, BF16)
    return xs.reshape(b, t, d)[:, lc:]
```

```python
import functools
import math

import jax
import jax.numpy as jnp
from jax import lax
from jax.experimental import pallas as pl
from jax.experimental.pallas import tpu as pltpu

F32 = jnp.float32
BF16 = jnp.bfloat16

NORM_EPS = 1e-6
HEAD_DIM = 128
GRID_W = 64
ROPE_BASE = 10000.0
ROPE_FREQS = HEAD_DIM // 4
ATTN_GROUP = 4
DN_CONV_W = 5
DN_CHUNK = 64
HG_CHUNK = 16
HG_SUPER = 64
N_EXPERTS = 16
N_GROUPS = 4
EXPERTS_PER_GROUP = N_EXPERTS // N_GROUPS
TOP_K = 2
EXPERT_BLOCK = 256
CODE_K_SHIFT = 16
CODE_K_UNIT = 1 << CODE_K_SHIFT
N_MOD = 6
VMEM_LIMIT = 56 * 1024 * 1024


def _cparams(*sem):
    return pltpu.CompilerParams(dimension_semantics=sem, vmem_limit_bytes=VMEM_LIMIT)


def _pick(n, prefs):
    for p in prefs:
        if n % p == 0:
            return p
    return n


def _bdot(a, b):
    return jnp.dot(a.astype(BF16), b.astype(BF16), preferred_element_type=F32)


def _bdot_nt(a, b):
    return lax.dot_general(a.astype(BF16), b.astype(BF16), (((1,), (1,)), ((), ())),
                           preferred_element_type=F32)


def _bdot_tn(a, b):
    return lax.dot_general(a.astype(BF16), b.astype(BF16), (((0,), (0,)), ((), ())),
                           preferred_element_type=F32)


def _silu(x):
    return x * jax.nn.sigmoid(x)


def _softplus(x):
    return jnp.maximum(x, 0.0) + jnp.log1p(jnp.exp(-jnp.abs(x)))


def _row_to_col(row, n):
    eye = lax.broadcasted_iota(jnp.int32, (n, n), 0) == lax.broadcasted_iota(jnp.int32, (n, n), 1)
    return jnp.sum(jnp.where(eye, row, 0.0), axis=1, keepdims=True)


def _ada_kernel(cc_ref, w_ref, b_ref, o_ref):
    a = _silu(cc_ref[...])
    o_ref[0] = _bdot(a, w_ref[0]) + b_ref[0]


def _ada_mod(cc, ada_w, ada_b):
    depth, d, n = ada_w.shape
    mp = cc.shape[0]
    tn = _pick(n, (1024, 512, 256, 128))
    return pl.pallas_call(
        _ada_kernel,
        grid=(depth, n // tn),
        in_specs=[pl.BlockSpec((mp, d), lambda i, j: (0, 0)),
                  pl.BlockSpec((1, d, tn), lambda i, j: (i, 0, j)),
                  pl.BlockSpec((1, 1, tn), lambda i, j: (i, 0, j))],
        out_specs=pl.BlockSpec((1, mp, tn), lambda i, j: (i, 0, j)),
        out_shape=jax.ShapeDtypeStruct((depth, mp, n), F32),
        compiler_params=_cparams("parallel", "parallel"),
        name="ada_mod",
    )(cc, ada_w, ada_b.reshape(depth, 1, n))


def _modulate_kernel(x_ref, mod_ref, h_ref, *, shift, scale):
    m = mod_ref[0]
    h_ref[...] = (x_ref[...] * (1.0 + m[scale:scale + 1]) + m[shift:shift + 1]).astype(h_ref.dtype)


def _mod_index(tiles_per_seq, ctx_tiles):
    def index(i):
        return ((i // tiles_per_seq) * 2 + jnp.where(i % tiles_per_seq < ctx_tiles, 0, 1), 0, 0)
    return index


def _modulate(x, modtab, tm, tiles_per_seq, ctx_tiles, shift, scale, dtype):
    m, d = x.shape
    midx = _mod_index(tiles_per_seq, ctx_tiles)
    return pl.pallas_call(
        functools.partial(_modulate_kernel, shift=shift, scale=scale),
        grid=(m // tm,),
        in_specs=[pl.BlockSpec((tm, d), lambda i: (i, 0)),
                  pl.BlockSpec((1, N_MOD, d), midx)],
        out_specs=pl.BlockSpec((tm, d), lambda i: (i, 0)),
        out_shape=jax.ShapeDtypeStruct((m, d), dtype),
        compiler_params=_cparams("parallel"),
        name="modulate",
    )(x, modtab)


def _proj_kernel(a_ref, w_ref, o_ref, wb_ref):
    @pl.when(pl.program_id(1) == 0)
    def _():
        wb_ref[...] = w_ref[...].astype(BF16)

    o_ref[...] = jnp.dot(a_ref[...], wb_ref[...], preferred_element_type=F32).astype(o_ref.dtype)


def _proj(a, w, col0, ncols, out_dtype=F32):
    m, k = a.shape
    tn = _pick(ncols, (1024, 512, 256, 128))
    assert col0 % tn == 0
    tm = _pick(m, (512, 256, 128, 64))
    c0 = col0 // tn
    return pl.pallas_call(
        _proj_kernel,
        grid=(ncols // tn, m // tm),
        in_specs=[pl.BlockSpec((tm, k), lambda j, i: (i, 0)),
                  pl.BlockSpec((k, tn), lambda j, i: (0, c0 + j))],
        out_specs=pl.BlockSpec((tm, tn), lambda j, i: (i, j)),
        out_shape=jax.ShapeDtypeStruct((m, ncols), out_dtype),
        scratch_shapes=[pltpu.VMEM((k, tn), BF16)],
        compiler_params=_cparams("parallel", "arbitrary"),
        name="proj",
    )(a, w)


def _ln_epilogue(y, x, mod, modn, lng, lnb, alpha, gate, shift, scale):
    r = alpha * x + mod[gate:gate + 1] * y
    rc = r - jnp.mean(r, axis=-1, keepdims=True)
    var = jnp.mean(rc * rc, axis=-1, keepdims=True)
    xn = rc * lax.rsqrt(var + NORM_EPS) * lng + lnb
    h = xn * (1.0 + modn[scale:scale + 1]) + modn[shift:shift + 1]
    return xn, h


def _out_ln_kernel(o_ref, w_ref, x_ref, mod_ref, modn_ref, lng_ref, lnb_ref, xn_ref, h_ref, acc_ref,
                   *, alpha, gate, shift, scale):
    kk = pl.program_id(1)

    @pl.when(kk == 0)
    def _():
        acc_ref[...] = jnp.zeros_like(acc_ref)

    acc_ref[...] += jnp.dot(o_ref[...], w_ref[...], preferred_element_type=F32)

    @pl.when(kk == pl.num_programs(1) - 1)
    def _():
        xn, h = _ln_epilogue(acc_ref[...], x_ref[...], mod_ref[0], modn_ref[0], lng_ref[...], lnb_ref[...],
                             alpha, gate, shift, scale)
        xn_ref[...] = xn
        h_ref[...] = h.astype(h_ref.dtype)


def _out_ln(o, w, x, mod, modn, lng, lnb, tm, tiles_per_seq, ctx_tiles, alpha, gate, shift, scale, h_dtype):
    m, k = o.shape
    d = w.shape[1]
    tk = _pick(k, (512, 256, 128))
    midx = _mod_index(tiles_per_seq, ctx_tiles)
    return pl.pallas_call(
        functools.partial(_out_ln_kernel, alpha=alpha, gate=gate, shift=shift, scale=scale),
        grid=(m // tm, k // tk),
        in_specs=[pl.BlockSpec((tm, tk), lambda i, j: (i, j)),
                  pl.BlockSpec((tk, d), lambda i, j: (j, 0)),
                  pl.BlockSpec((tm, d), lambda i, j: (i, 0)),
                  pl.BlockSpec((1, N_MOD, d), lambda i, j: midx(i)),
                  pl.BlockSpec((1, N_MOD, d), lambda i, j: midx(i)),
                  pl.BlockSpec((1, d), lambda i, j: (0, 0)),
                  pl.BlockSpec((1, d), lambda i, j: (0, 0))],
        out_specs=[pl.BlockSpec((tm, d), lambda i, j: (i, 0)),
                   pl.BlockSpec((tm, d), lambda i, j: (i, 0))],
        out_shape=[jax.ShapeDtypeStruct((m, d), F32), jax.ShapeDtypeStruct((m, d), h_dtype)],
        scratch_shapes=[pltpu.VMEM((tm, d), F32)],
        compiler_params=_cparams("parallel", "arbitrary"),
        name="out_ln",
    )(o, w, x, mod, modn, lng, lnb)


def _comb_ln_kernel(f0_ref, f1_ref, x_ref, mod_ref, modn_ref, lng_ref, lnb_ref, xn_ref, h_ref,
                    *, alpha, gate, shift, scale):
    y = f0_ref[...] + f1_ref[...]
    xn, h = _ln_epilogue(y, x_ref[...], mod_ref[0], modn_ref[0], lng_ref[...], lnb_ref[...],
                         alpha, gate, shift, scale)
    xn_ref[...] = xn
    h_ref[...] = h.astype(h_ref.dtype)


def _comb_ln(f, x, mod, modn, lng, lnb, tm, tiles_per_seq, ctx_tiles, alpha, gate, shift, scale, h_dtype):
    m, d = x.shape
    midx = _mod_index(tiles_per_seq, ctx_tiles)
    nt = m // tm
    return pl.pallas_call(
        functools.partial(_comb_ln_kernel, alpha=alpha, gate=gate, shift=shift, scale=scale),
        grid=(nt,),
        in_specs=[pl.BlockSpec((tm, d), lambda i: (i, 0)),
                  pl.BlockSpec((tm, d), lambda i: (nt + i, 0)),
                  pl.BlockSpec((tm, d), lambda i: (i, 0)),
                  pl.BlockSpec((1, N_MOD, d), midx),
                  pl.BlockSpec((1, N_MOD, d), midx),
                  pl.BlockSpec((1, d), lambda i: (0, 0)),
                  pl.BlockSpec((1, d), lambda i: (0, 0))],
        out_specs=[pl.BlockSpec((tm, d), lambda i: (i, 0)),
                   pl.BlockSpec((tm, d), lambda i: (i, 0))],
        out_shape=[jax.ShapeDtypeStruct((m, d), F32), jax.ShapeDtypeStruct((m, d), h_dtype)],
        compiler_params=_cparams("parallel"),
        name="comb_ln",
    )(f, f, x, mod, modn, lng, lnb)


def _rope_tables(lc, l):
    pos = jnp.arange(l)
    row = (pos // GRID_W).astype(F32)
    col = (pos % GRID_W).astype(F32)
    inv_freq = ROPE_BASE ** (-jnp.arange(ROPE_FREQS, dtype=F32) / ROPE_FREQS)
    ar = row[:, None] * inv_freq
    ac = col[:, None] * inv_freq
    cos = jnp.concatenate([jnp.cos(ar), jnp.cos(ar), jnp.cos(ac), jnp.cos(ac)], axis=1)
    sin = jnp.concatenate([-jnp.sin(ar), jnp.sin(ar), -jnp.sin(ac), jnp.sin(ac)], axis=1)
    cos = jnp.concatenate([jnp.ones((lc, HEAD_DIM), F32), cos], axis=0)
    sin = jnp.concatenate([jnp.zeros((lc, HEAD_DIM), F32), sin], axis=0)
    return cos, sin


def _norm_rope(x, g, cos, sin):
    y = x * lax.rsqrt(jnp.mean(x * x, axis=-1, keepdims=True) + NORM_EPS) * g
    lane = lax.broadcasted_iota(jnp.int32, y.shape, 1)
    partner = jnp.where(lane % (2 * ROPE_FREQS) < ROPE_FREQS,
                        pltpu.roll(y, HEAD_DIM - ROPE_FREQS, 1), pltpu.roll(y, ROPE_FREQS, 1))
    return y * cos + partner * sin


def _attn_kernel(q_ref, k_ref, v_ref, qg_ref, kg_ref, cos_ref, sin_ref, o_ref, kb_ref, vb_ref,
                 *, lc, tq, t):
    j = pl.program_id(2)

    @pl.when(j == 0)
    def _():
        kb_ref[...] = _norm_rope(k_ref[0], kg_ref[...], cos_ref[...], sin_ref[...]).astype(BF16)
        vb_ref[...] = v_ref[0].astype(BF16)

    r0 = pl.multiple_of(j * tq, tq)
    cos = cos_ref[pl.ds(r0, tq), :]
    sin = sin_ref[pl.ds(r0, tq), :]
    scale = HEAD_DIM ** -0.5

    def attend(n_keys):
        for g in range(ATTN_GROUP):
            qh = _norm_rope(q_ref[0, :, g * HEAD_DIM:(g + 1) * HEAD_DIM], qg_ref[...], cos, sin)
            s = _bdot_nt(qh, kb_ref[0:n_keys, :])
            e = jnp.exp((s - jnp.max(s, axis=-1, keepdims=True)) * scale)
            den = jnp.sum(e, axis=-1, keepdims=True)
            o = jnp.dot(e.astype(BF16), vb_ref[0:n_keys, :], preferred_element_type=F32) / den
            o_ref[0, :, g * HEAD_DIM:(g + 1) * HEAD_DIM] = o.astype(o_ref.dtype)

    @pl.when(j < lc // tq)
    def _():
        attend(lc)

    @pl.when(j >= lc // tq)
    def _():
        attend(t)


def _attention(p, qg, kg, cos, sin, lc, tq):
    b, t, n = p.shape
    hkv = n // HEAD_DIM // (ATTN_GROUP + 2)
    hq = hkv * ATTN_GROUP
    gw = ATTN_GROUP * HEAD_DIM
    return pl.pallas_call(
        functools.partial(_attn_kernel, lc=lc, tq=tq, t=t),
        grid=(b, hkv, t // tq),
        in_specs=[pl.BlockSpec((1, tq, gw), lambda bi, h, j: (bi, j, h)),
                  pl.BlockSpec((1, t, HEAD_DIM), lambda bi, h, j: (bi, 0, hq + h)),
                  pl.BlockSpec((1, t, HEAD_DIM), lambda bi, h, j: (bi, 0, hq + hkv + h)),
                  pl.BlockSpec((1, HEAD_DIM), lambda bi, h, j: (0, 0)),
                  pl.BlockSpec((1, HEAD_DIM), lambda bi, h, j: (0, 0)),
                  pl.BlockSpec((t, HEAD_DIM), lambda bi, h, j: (0, 0)),
                  pl.BlockSpec((t, HEAD_DIM), lambda bi, h, j: (0, 0))],
        out_specs=pl.BlockSpec((1, tq, gw), lambda bi, h, j: (bi, j, h)),
        out_shape=jax.ShapeDtypeStruct((b, t, hq * HEAD_DIM), BF16),
        scratch_shapes=[pltpu.VMEM((t, HEAD_DIM), BF16), pltpu.VMEM((t, HEAD_DIM), BF16)],
        compiler_params=_cparams("parallel", "parallel", "arbitrary"),
        name="attention",
    )(p, p, p, qg, kg, cos, sin)


def _dn_prep_kernel(p_ref, cw_ref, o_ref, *, lc, t, n_qk_tiles):
    c = pl.program_id(1)
    x = p_ref[0]
    row = lax.broadcasted_iota(jnp.int32, x.shape, 0)
    seg = row >= lc
    acc = jnp.zeros_like(x)
    for jj in range(DN_CONV_W):
        off = jj - DN_CONV_W // 2
        if off == 0:
            sh = x
        else:
            src = row + off
            ok = (src >= 0) & (src < t) & ((src >= lc) == seg)
            sh = jnp.where(ok, pltpu.roll(x, (-off) % t, 0), 0.0)
        acc = acc + sh * cw_ref[jj:jj + 1, :]
    y = _silu(acc)
    nrm = lax.rsqrt(jnp.sum(y * y, axis=-1, keepdims=True) + NORM_EPS)
    mult = jnp.where(c < n_qk_tiles, nrm * (HEAD_DIM ** -0.5), jnp.where(c < 2 * n_qk_tiles, nrm, 1.0))
    o_ref[0] = y * mult


def _dn_prep(p, conv_w, lc, n_qk_tiles):
    b, t, _ = p.shape
    nch = conv_w.shape[1]
    return pl.pallas_call(
        functools.partial(_dn_prep_kernel, lc=lc, t=t, n_qk_tiles=n_qk_tiles),
        grid=(b, nch // HEAD_DIM),
        in_specs=[pl.BlockSpec((1, t, HEAD_DIM), lambda bi, c: (bi, 0, c)),
                  pl.BlockSpec((DN_CONV_W, HEAD_DIM), lambda bi, c: (0, c))],
        out_specs=pl.BlockSpec((1, t, HEAD_DIM), lambda bi, c: (bi, 0, c)),
        out_shape=jax.ShapeDtypeStruct((b, t, nch), F32),
        compiler_params=_cparams("parallel", "parallel"),
        name="dn_prep",
    )(p, conv_w)


def _dn_gates_kernel(ba_ref, alog_ref, dtb_ref, o_ref, *, t, hv):
    x = ba_ref[0]
    lane = lax.broadcasted_iota(jnp.int32, (DN_CHUNK, 4 * hv), 1)
    g = -jnp.exp(alog_ref[...]) * _softplus(x + dtb_ref[...])
    ii = lax.broadcasted_iota(jnp.int32, (DN_CHUNK, DN_CHUNK), 0)
    jj = lax.broadcasted_iota(jnp.int32, (DN_CHUNK, DN_CHUNK), 1)
    lower = (ii >= jj).astype(F32)
    upper = (ii <= jj).astype(F32)
    for c in range(t // DN_CHUNK):
        rows = slice(c * DN_CHUNK, (c + 1) * DN_CHUNK)
        gch = g[rows]
        pre = jnp.dot(lower, gch, precision=lax.Precision.HIGHEST, preferred_element_type=F32)
        suf = jnp.dot(upper, gch, precision=lax.Precision.HIGHEST, preferred_element_type=F32)
        o_ref[0, rows, :] = jnp.where(lane < 2 * hv, jax.nn.sigmoid(x[rows]),
                                      jnp.where(lane < 3 * hv, pre, suf))


def _dn_gates(ba, a_log, dt_bias):
    b, t, n = ba.shape
    hv = n // 4
    zeros = jnp.zeros((1, 2 * hv), F32)
    alog = jnp.concatenate([zeros, a_log.reshape(1, 2 * hv)], axis=1)
    dtb = jnp.concatenate([zeros, dt_bias.reshape(1, 2 * hv)], axis=1)
    return pl.pallas_call(
        functools.partial(_dn_gates_kernel, t=t, hv=hv),
        grid=(b,),
        in_specs=[pl.BlockSpec((1, t, n), lambda bi: (bi, 0, 0)),
                  pl.BlockSpec((1, n), lambda bi: (0, 0)),
                  pl.BlockSpec((1, n), lambda bi: (0, 0))],
        out_specs=pl.BlockSpec((1, t, n), lambda bi: (bi, 0, 0)),
        out_shape=jax.ShapeDtypeStruct((b, t, n), F32),
        compiler_params=_cparams("parallel"),
        name="dn_gates",
    )(ba, alog, dtb)


def _dn_chunk_prep(kk, qk_raw, qc, kc, vc, beta_r, gc_r, reverse):
    c = DN_CHUNK
    ii = lax.broadcasted_iota(jnp.int32, (c, c), 0)
    jj = lax.broadcasted_iota(jnp.int32, (c, c), 1)
    incl = (ii <= jj) if reverse else (ii >= jj)
    strict = (ii < jj) if reverse else (ii > jj)
    beta_c = _row_to_col(beta_r, c)
    gc_c = _row_to_col(gc_r, c)
    g_end = gc_r[:, 0:1] if reverse else gc_r[:, c - 1:c]
    decay = jnp.exp(jnp.where(incl, gc_c - gc_r, -jnp.inf))
    m = jnp.where(strict, beta_c * kk * decay, 0.0)
    eg = jnp.exp(gc_c)
    x = jnp.concatenate([vc * beta_c, kc * (beta_c * eg)], axis=1)
    qk = jnp.where(incl, qk_raw * decay, 0.0)
    return (m, x, qk.astype(BF16), (qc * eg).astype(BF16),
            (kc * jnp.exp(g_end - gc_c)).astype(BF16), jnp.exp(g_end))


def _unit_triangular_solve(ms, xs):
    xs = [x - _bdot(m, x) for m, x in zip(ms, xs)]
    pws = ms
    for _ in range(int(math.log2(DN_CHUNK)) - 1):
        pws = [_bdot(p, p) for p in pws]
        xs = [x + _bdot(p, x) for p, x in zip(pws, xs)]
    return xs


def _dn_core_kernel(q_ref, k_ref, v_ref, z_ref, gt_ref, ng_ref, o_ref,
                    u_ref, wq_ref, qk_ref, kd_ref, ge_ref, s_ref, acc_ref, *, lc, t, hv):
    qh = pl.program_id(1)
    c64 = DN_CHUNK
    nc = t // c64
    ncc = lc // c64
    combos = [(sub, d) for sub in range(2) for d in range(2)]

    cpi = 2 if nc % 2 == 0 else 1

    def phase_a(it, carry):
        prepped, where = [], []
        for j in range(cpi):
            c = it * cpi + j
            r0 = pl.multiple_of(c * c64, c64)
            rows = pl.ds(r0, c64)
            qc, kc = q_ref[0, rows, :], k_ref[0, rows, :]
            gram = _bdot_nt(jnp.concatenate([kc, qc], axis=0), kc)
            kk, qk_raw = gram[:c64], gram[c64:]
            for ci, (sub, d) in enumerate(combos):
                head = qh * 2 + sub
                beta_r = gt_ref[0, d * hv + head, pl.ds(c, 1), :]
                gc_r = gt_ref[0, (2 + d) * hv + head, pl.ds(c, 1), :]
                vc = v_ref[0, rows, sub * HEAD_DIM:(sub + 1) * HEAD_DIM]
                prepped.append(_dn_chunk_prep(kk, qk_raw, qc, kc, vc, beta_r, gc_r, d == 1))
                where.append((ci, c, r0, rows))
        sols = _unit_triangular_solve([p[0] for p in prepped], [p[1] for p in prepped])
        for (ci, c, r0, rows), x, (_, _, qk, qd, kd, ge) in zip(where, sols, prepped):
            u_ref[ci, rows, :] = x[:, :HEAD_DIM]
            wq_ref[ci, pl.ds(pl.multiple_of(2 * r0, 2 * c64), c64), :] = x[:, HEAD_DIM:].astype(BF16)
            wq_ref[ci, pl.ds(pl.multiple_of(2 * r0 + c64, c64), c64), :] = qd
            qk_ref[ci, rows, :] = qk
            kd_ref[ci, rows, :] = kd
            ge_ref[ci, pl.ds(c, 1), :] = jnp.broadcast_to(ge, (1, HEAD_DIM))
        return carry

    lax.fori_loop(0, nc // cpi, phase_a, 0)

    s_ref[...] = jnp.zeros_like(s_ref)

    def phase_b(n, carry):
        cs = [n if d == 0 else jnp.where(n < ncc, ncc - 1 - n, nc - 1 - (n - ncc)) for _, d in combos]
        r0s = [pl.multiple_of(c * c64, c64) for c in cs]
        ss = [s_ref[ci] for ci in range(4)]
        wss = [jnp.dot(wq_ref[ci, pl.ds(pl.multiple_of(2 * r0s[ci], 2 * c64), 2 * c64), :],
                       ss[ci].astype(BF16), preferred_element_type=F32) for ci in range(4)]
        vns = [(u_ref[ci, pl.ds(r0s[ci], c64), :] - wss[ci][:c64]).astype(BF16) for ci in range(4)]
        upd = [lax.dot_general(kd_ref[ci, pl.ds(r0s[ci], c64), :], vns[ci], (((0,), (0,)), ((), ())),
                               preferred_element_type=F32) for ci in range(4)]
        outs = [wss[ci][c64:] + jnp.dot(qk_ref[ci, pl.ds(r0s[ci], c64), :], vns[ci],
                                        preferred_element_type=F32) for ci in range(4)]
        for ci, (sub, d) in enumerate(combos):
            s_ref[ci] = ss[ci] * ge_ref[ci, pl.ds(cs[ci], 1), :][:, 0:1] + upd[ci]
            acc_ref[d, pl.ds(r0s[ci], c64), sub * HEAD_DIM:(sub + 1) * HEAD_DIM] = outs[ci]
        return carry

    lax.fori_loop(0, nc, phase_b, 0)

    for sub in range(2):
        cols = slice(sub * HEAD_DIM, (sub + 1) * HEAD_DIM)
        o = acc_ref[0, :, cols] + acc_ref[1, :, cols]
        o = o * lax.rsqrt(jnp.mean(o * o, axis=-1, keepdims=True) + NORM_EPS) * ng_ref[...]
        o_ref[0, :, cols] = (o * _silu(z_ref[0, :, cols])).astype(o_ref.dtype)


def _dn_core(qkv, pz, gt, norm_g, lc, n_qk_heads, z_col0):
    b, t, _ = qkv.shape
    hv = 2 * n_qk_heads
    vw = 2 * HEAD_DIM
    nc = t // DN_CHUNK
    return pl.pallas_call(
        functools.partial(_dn_core_kernel, lc=lc, t=t, hv=hv),
        grid=(b, n_qk_heads),
        in_specs=[pl.BlockSpec((1, t, HEAD_DIM), lambda bi, h: (bi, 0, h)),
                  pl.BlockSpec((1, t, HEAD_DIM), lambda bi, h: (bi, 0, n_qk_heads + h)),
                  pl.BlockSpec((1, t, vw), lambda bi, h: (bi, 0, n_qk_heads + h)),
                  pl.BlockSpec((1, t, vw), lambda bi, h: (bi, 0, z_col0 // vw + h)),
                  pl.BlockSpec((1, 4 * hv, nc, DN_CHUNK), lambda bi, h: (bi, 0, 0, 0)),
                  pl.BlockSpec((1, HEAD_DIM), lambda bi, h: (0, 0))],
        out_specs=pl.BlockSpec((1, t, vw), lambda bi, h: (bi, 0, h)),
        out_shape=jax.ShapeDtypeStruct((b, t, hv * HEAD_DIM), BF16),
        scratch_shapes=[pltpu.VMEM((4, t, HEAD_DIM), F32),
                        pltpu.VMEM((4, 2 * t, HEAD_DIM), BF16),
                        pltpu.VMEM((4, t, DN_CHUNK), BF16),
                        pltpu.VMEM((4, t, HEAD_DIM), BF16),
                        pltpu.VMEM((4, -(-nc // 8) * 8, HEAD_DIM), F32),
                        pltpu.VMEM((4, HEAD_DIM, HEAD_DIM), F32),
                        pltpu.VMEM((2, t, vw), F32)],
        compiler_params=_cparams("parallel", "parallel"),
        name="dn_core",
    )(qkv, qkv, qkv, pz, gt, norm_g)


def _chunk_scan(x, reverse, chunk):
    n = x.shape[0]
    pos = lax.broadcasted_iota(jnp.int32, x.shape, 0) % chunk
    sh = 1
    while sh < chunk:
        if reverse:
            x = x + jnp.where(pos < chunk - sh, pltpu.roll(x, n - sh, 0), 0.0)
        else:
            x = x + jnp.where(pos >= sh, pltpu.roll(x, sh, 0), 0.0)
        sh *= 2
    return x


def _hg_kernel(q_ref, i_ref, gate_ref, ff_ref, fb_ref, lb_ref, ng_ref, o_ref,
               g_ref, k_ref, fl_ref, qa_ref, qd_ref, kd_ref, st_ref, acc_ref, *, lc, t):
    cs, c16 = HG_SUPER, HG_CHUNK
    nsub = cs // c16
    nc = t // cs
    ncc = lc // cs
    ii = lax.broadcasted_iota(jnp.int32, (cs, cs), 0)
    jj = lax.broadcasted_iota(jnp.int32, (cs, cs), 1)
    krow = lax.broadcasted_iota(jnp.int32, (cs, HEAD_DIM), 0)
    for d in range(2):
        f_ref = ff_ref if d == 0 else fb_ref
        lb = lb_ref[d:d + 1, :]
        fg = lb + (1.0 - lb) * jax.nn.sigmoid(f_ref[0])
        keys = 1.0 - fg
        logf = jnp.log(fg)
        pre = _chunk_scan(logf, False, cs)
        suf = _chunk_scan(logf, True, cs)
        gcum = suf if d == 1 else pre
        total = pre + suf - logf
        g_ref[d] = gcum
        k_ref[d] = keys
        fl_ref[d] = jnp.exp(total)
        qa_ref[d] = (q_ref[0] * jnp.exp(_chunk_scan(logf, d == 1, c16))).astype(BF16)
        qd_ref[d] = (q_ref[0] * jnp.exp(gcum)).astype(BF16)
        kd_ref[d] = (keys * jnp.exp(total - gcum)).astype(BF16)
    st_ref[...] = jnp.zeros_like(st_ref)

    def body(n, carry):
        r0s = [pl.multiple_of(n * cs, cs),
               pl.multiple_of(jnp.where(n < ncc, ncc - 1 - n, nc - 1 - (n - ncc)) * cs, cs)]
        vvs = [i_ref[0, pl.ds(r0, cs), :].astype(BF16) for r0 in r0s]
        sts = [st_ref[d] for d in range(2)]
        inter = [_bdot_nt(qd_ref[d, pl.ds(r0s[d], cs), :], sts[d]) for d in range(2)]
        upd = [_bdot_tn(vvs[d], kd_ref[d, pl.ds(r0s[d], cs), :]) for d in range(2)]
        scaled = []
        for d in range(2):
            gc = g_ref[d, pl.ds(r0s[d], cs), :]
            kc = k_ref[d, pl.ds(r0s[d], cs), :]
            for a in range(nsub):
                if d == 0:
                    ref = gc[a * c16 - 1:a * c16, :] if a > 0 else None
                    allowed = krow < (a + 1) * c16
                else:
                    ref = gc[(a + 1) * c16:(a + 1) * c16 + 1, :] if a < nsub - 1 else None
                    allowed = krow >= a * c16
                expo = -gc if ref is None else ref - gc
                scaled.append(kc * jnp.exp(jnp.where(allowed, expo, -jnp.inf)))
        parts = [_bdot_nt(qa_ref[d, pl.ds(pl.multiple_of(r0s[d] + a * c16, c16), c16), :], scaled[d * nsub + a])
                 for d in range(2) for a in range(nsub)]
        ps = [jnp.where((ii <= jj) if d == 1 else (ii >= jj),
                        jnp.concatenate(parts[d * nsub:(d + 1) * nsub], axis=0), 0.0) for d in range(2)]
        intra = [_bdot(ps[d], vvs[d]) for d in range(2)]
        for d in range(2):
            acc_ref[d, pl.ds(r0s[d], cs), :] = intra[d] + inter[d]
            st_ref[d] = sts[d] * fl_ref[d, pl.ds(r0s[d], 1), :] + upd[d]
        return carry

    lax.fori_loop(0, nc, body, 0)
    o = acc_ref[0] + acc_ref[1]
    o = o * lax.rsqrt(jnp.mean(o * o, axis=-1, keepdims=True) + NORM_EPS) * ng_ref[...]
    o_ref[0] = (o * jax.nn.sigmoid(gate_ref[0])).astype(o_ref.dtype)


def _hgrn2(p, lb, norm_g, lc, n_heads):
    b, t, _ = p.shape
    assert lc % HG_SUPER == 0 and t % HG_SUPER == 0
    blk = lambda k: pl.BlockSpec((1, t, HEAD_DIM), lambda bi, h, k=k: (bi, 0, k * n_heads + h))
    seq = lambda dt: pltpu.VMEM((2, t, HEAD_DIM), dt)
    return pl.pallas_call(
        functools.partial(_hg_kernel, lc=lc, t=t),
        grid=(b, n_heads),
        in_specs=[blk(0), blk(1), blk(2), blk(3), blk(4),
                  pl.BlockSpec((2, HEAD_DIM), lambda bi, h: (0, h)),
                  pl.BlockSpec((1, HEAD_DIM), lambda bi, h: (0, 0))],
        out_specs=pl.BlockSpec((1, t, HEAD_DIM), lambda bi, h: (bi, 0, h)),
        out_shape=jax.ShapeDtypeStruct((b, t, n_heads * HEAD_DIM), BF16),
        scratch_shapes=[seq(F32), seq(F32), seq(F32), seq(BF16), seq(BF16), seq(BF16),
                        pltpu.VMEM((2, HEAD_DIM, HEAD_DIM), F32), seq(F32)],
        compiler_params=_cparams("parallel", "parallel"),
        name="hgrn2",
    )(p, p, p, p, p, lb, norm_g)


def _split_bf16(x):
    hi = x.astype(BF16)
    return hi, (x - hi.astype(F32)).astype(BF16)


def _router_kernel(h_ref, w_ref, b_ref, e_ref, g_ref):
    h_hi, h_lo = _split_bf16(h_ref[...])
    w_hi, w_lo = _split_bf16(w_ref[...])
    nt = (((1,), (1,)), ((), ()))
    logits = (lax.dot_general(w_hi, h_hi, nt, preferred_element_type=F32)
              + lax.dot_general(w_hi, h_lo, nt, preferred_element_type=F32)
              + lax.dot_general(w_lo, h_hi, nt, preferred_element_type=F32))
    scores = jax.nn.sigmoid(logits)
    sel = scores + b_ref[...]
    epg = EXPERTS_PER_GROUP
    rows = lambda a, e: a[e:e + 1, :]
    best_gs = None
    for g in range(N_GROUPS):
        gs = None
        for a in range(epg):
            for c in range(a + 1, epg):
                pair = rows(sel, g * epg + a) + rows(sel, g * epg + c)
                gs = pair if gs is None else jnp.maximum(gs, pair)
        if best_gs is None:
            best_gs, group = gs, jnp.zeros(gs.shape, jnp.int32)
        else:
            upd = gs > best_gs
            best_gs = jnp.where(upd, gs, best_gs)
            group = jnp.where(upd, g, group)

    def in_group(a, l):
        v = rows(a, l)
        for g in range(1, N_GROUPS):
            v = jnp.where(group == g, rows(a, g * epg + l), v)
        return v

    sel_l = [in_group(sel, l) for l in range(epg)]
    sc_l = [in_group(scores, l) for l in range(epg)]

    def top1(exclude):
        best, idx, sc = None, None, None
        for l in range(epg):
            v = sel_l[l] if exclude is None else jnp.where(exclude == l, -jnp.inf, sel_l[l])
            if best is None:
                best, idx, sc = v, jnp.zeros(v.shape, jnp.int32), sc_l[l]
            else:
                upd = v > best
                best = jnp.where(upd, v, best)
                idx = jnp.where(upd, l, idx)
                sc = jnp.where(upd, sc_l[l], sc)
        return idx, sc

    l0, s0 = top1(None)
    l1, s1 = top1(l0)
    e_ref[0:1, :] = group * epg + l0
    e_ref[1:2, :] = group * epg + l1
    den = s0 + s1
    g_ref[0:1, :] = s0 / den
    g_ref[1:2, :] = s1 / den


def _router(h, router_w, router_b):
    m, d = h.shape
    tm = _pick(m, (512, 256, 128))
    ne = router_w.shape[1]
    return pl.pallas_call(
        _router_kernel,
        grid=(m // tm,),
        in_specs=[pl.BlockSpec((tm, d), lambda i: (i, 0)),
                  pl.BlockSpec((ne, d), lambda i: (0, 0)),
                  pl.BlockSpec((ne, 1), lambda i: (0, 0))],
        out_specs=[pl.BlockSpec((TOP_K, tm), lambda i: (0, i)),
                   pl.BlockSpec((TOP_K, tm), lambda i: (0, i))],
        out_shape=[jax.ShapeDtypeStruct((TOP_K, m), jnp.int32), jax.ShapeDtypeStruct((TOP_K, m), F32)],
        compiler_params=_cparams("parallel"),
        name="router",
    )(h, router_w.T, router_b.reshape(ne, 1))


def _moe_ffn_kernel(bexp_ref, nact_ref, code_ref, h_hbm, gate_ref, wg_ref, wu_ref, wd_ref, f_hbm,
                    xbuf, ybuf, sem_in, sem_out, *, n_tok):
    i = pl.program_id(0)
    n_act = nact_ref[0]
    slot = i % 2
    eb = EXPERT_BLOCK

    def gather(blk, sl, r):
        tok = code_ref[blk * eb + r] & (CODE_K_UNIT - 1)
        return pltpu.make_async_copy(h_hbm.at[pl.ds(tok, 1)], xbuf.at[sl, pl.ds(r, 1)], sem_in.at[sl])

    def scatter(blk, sl, r):
        code = code_ref[blk * eb + r]
        dst = lax.shift_right_logical(code, CODE_K_SHIFT) * n_tok + (code & (CODE_K_UNIT - 1))
        return pltpu.make_async_copy(ybuf.at[sl, pl.ds(r, 1)], f_hbm.at[pl.ds(dst, 1)], sem_out.at[sl])

    def for_rows(fn):
        def body(r, carry):
            fn(r)
            return carry
        lax.fori_loop(0, eb, body, 0, unroll=8)

    @pl.when(i == 0)
    def _():
        ybuf[...] = jnp.zeros_like(ybuf)
        for sl in range(2):
            spare = pltpu.make_async_copy(ybuf.at[sl], f_hbm.at[pl.ds(TOP_K * n_tok + sl * eb, eb)],
                                          sem_out.at[sl])
            spare.start()
            spare.wait()

    @pl.when((i == 0) & (n_act > 0))
    def _():
        for_rows(lambda r: gather(0, 0, r).start())

    @pl.when(i + 1 < n_act)
    def _():
        for_rows(lambda r: gather(i + 1, 1 - slot, r).start())

    @pl.when(i < n_act)
    def _():
        for_rows(lambda r: gather(i, slot, r).wait())
        x = xbuf[slot].astype(BF16)
        a = _silu(jnp.dot(x, wg_ref[0], preferred_element_type=F32)) \
            * jnp.dot(x, wu_ref[0], preferred_element_type=F32)
        y = jnp.dot(a.astype(BF16), wd_ref[0], preferred_element_type=F32)
        ybuf[slot] = y * gate_ref[:, 0:1]
        for_rows(lambda r: scatter(i, slot, r).start())

    @pl.when((i >= 1) & (i - 1 < n_act))
    def _():
        for_rows(lambda r: scatter(i - 1, 1 - slot, r).wait())

    @pl.when((i == pl.num_programs(0) - 1) & (i < n_act))
    def _():
        for_rows(lambda r: scatter(i, slot, r).wait())


def _moe_ffn(h, block_expert, n_active, code_buf, gate_buf, wg, wu, wd):
    m, d = h.shape
    de = wg.shape[2]
    nb = block_expert.shape[0]
    grid_spec = pltpu.PrefetchScalarGridSpec(
        num_scalar_prefetch=3,
        grid=(nb,),
        in_specs=[pl.BlockSpec(memory_space=pl.ANY),
                  pl.BlockSpec((EXPERT_BLOCK, HEAD_DIM), lambda i, be, na, cb: (i, 0)),
                  pl.BlockSpec((1, d, de), lambda i, be, na, cb: (be[i], 0, 0)),
                  pl.BlockSpec((1, d, de), lambda i, be, na, cb: (be[i], 0, 0)),
                  pl.BlockSpec((1, de, d), lambda i, be, na, cb: (be[i], 0, 0))],
        out_specs=pl.BlockSpec(memory_space=pl.ANY),
        scratch_shapes=[pltpu.VMEM((2, EXPERT_BLOCK, d), F32), pltpu.VMEM((2, EXPERT_BLOCK, d), F32),
                        pltpu.SemaphoreType.DMA((2,)), pltpu.SemaphoreType.DMA((2,))],
    )
    return pl.pallas_call(
        functools.partial(_moe_ffn_kernel, n_tok=m),
        grid_spec=grid_spec,
        out_shape=jax.ShapeDtypeStruct((TOP_K * m + 2 * EXPERT_BLOCK, d), F32),
        compiler_params=_cparams("arbitrary"),
        name="moe_ffn",
    )(block_expert, n_active, code_buf, h, gate_buf, wg, wu, wd)


def _moe_dispatch(expert, gate):
    m = expert.shape[1]
    n_pairs = m * TOP_K
    e_flat = expert.T.reshape(-1)
    onehot = (e_flat[:, None] == jnp.arange(N_EXPERTS)[None, :]).astype(jnp.int32)
    csum = jnp.cumsum(onehot, axis=0)
    rank = jnp.take_along_axis(csum, e_flat[:, None], axis=1)[:, 0] - 1
    counts = csum[-1]
    padded = (counts + EXPERT_BLOCK - 1) // EXPERT_BLOCK * EXPERT_BLOCK
    pend = jnp.cumsum(padded)
    pstart = pend - padded
    dest = pstart[e_flat] + rank
    n_blocks = -(-(n_pairs + N_EXPERTS * (EXPERT_BLOCK - 1)) // EXPERT_BLOCK)
    n_rows = n_blocks * EXPERT_BLOCK
    assert 2 * EXPERT_BLOCK <= m <= CODE_K_UNIT
    pair = jnp.arange(n_pairs, dtype=jnp.int32)
    row = jnp.arange(n_rows, dtype=jnp.int32)
    pad_code = TOP_K * CODE_K_UNIT + row % (2 * EXPERT_BLOCK)
    code_buf = pad_code.at[dest].set((pair % TOP_K) * CODE_K_UNIT + pair // TOP_K)
    gate_buf = jnp.zeros((n_rows,), F32).at[dest].set(gate.T.reshape(-1))
    blk_start = jnp.arange(n_blocks) * EXPERT_BLOCK
    block_expert = jnp.minimum(jnp.searchsorted(pend, blk_start, side='right'), N_EXPERTS - 1).astype(jnp.int32)
    n_active = (pend[-1:] // EXPERT_BLOCK).astype(jnp.int32)
    gate_buf = jnp.broadcast_to(gate_buf[:, None], (n_rows, HEAD_DIM))
    return block_expert, n_active, code_buf, gate_buf


def _moe(h2, router_w, router_b, wg, wu, wd):
    expert, gate = _router(h2, router_w, router_b)
    block_expert, n_active, code_buf, gate_buf = _moe_dispatch(expert, gate)
    return _moe_ffn(h2, block_expert, n_active, code_buf, gate_buf, wg, wu, wd)


def kernel(x, c, ctx, c_ctx, ada_w, ada_b, ln_g, ln_b, attn_w_in, attn_q_g, attn_k_g, attn_w_out, dn_w_in, dn_conv, dn_a_log, dn_dt_bias, dn_norm_g, dn_w_out, hg_w_in, hg_lb, hg_norm_g, hg_w_out, router_w, router_b, moe_w_gate, moe_w_up, moe_w_down):
    b, l, d = x.shape
    lc = ctx.shape[1]
    t = lc + l
    m = b * t
    depth = ada_w.shape[0]
    alpha = (2.0 * depth) ** 0.25
    tm = _pick(math.gcd(lc, l), (256, 128, 64))
    tiles_per_seq, ctx_tiles = t // tm, lc // tm
    n_heads = d // HEAD_DIM

    mp = -(-(b + 1) // 8) * 8
    cc = jnp.concatenate([c, c_ctx[None, :], jnp.zeros((mp - b - 1, d), F32)], axis=0)
    mod = _ada_mod(cc, ada_w, ada_b)
    mod_l = mod[:, :b].reshape(depth, b, 1, N_MOD, d)
    mod_c = jnp.broadcast_to(mod[:, b].reshape(depth, 1, 1, N_MOD, d), (depth, b, 1, N_MOD, d))
    modtab = jnp.concatenate([mod_c, mod_l], axis=2).reshape(depth, 2 * b, N_MOD, d)

    lb_cum = jnp.cumsum(jax.nn.softmax(hg_lb.astype(F32), axis=1), axis=1)
    lb_cum = lb_cum - lb_cum[:, :1]

    xs = jnp.concatenate([ctx, x], axis=1).reshape(m, d)
    h = _modulate(xs, modtab[0], tm, tiles_per_seq, ctx_tiles, 0, 1, BF16)
    cos, sin = _rope_tables(lc, l)
    rows = (tm, tiles_per_seq, ctx_tiles)

    for i in range(depth):
        kind, j = i % 3, i // 3
        if kind == 0:
            p = _proj(h, attn_w_in[j], 0, attn_w_in.shape[2])
            o = _attention(p.reshape(b, t, -1), attn_q_g[j][None, :], attn_k_g[j][None, :], cos, sin, lc, tm)
            w_out = attn_w_out[j]
        elif kind == 1:
            n_conv = dn_conv.shape[2]
            hv = dn_a_log.shape[2]
            n_main = n_conv + hv * HEAD_DIM
            p = _proj(h, dn_w_in[j], 0, n_main).reshape(b, t, n_main)
            ba = _proj(h, dn_w_in[j], n_main, 4 * hv).reshape(b, t, 4 * hv)
            qkv = _dn_prep(p, dn_conv[j], lc, hv // 2)
            gates = _dn_gates(ba, dn_a_log[j], dn_dt_bias[j])
            gt = jnp.swapaxes(gates, 1, 2).reshape(b, 4 * hv, t // DN_CHUNK, DN_CHUNK)
            o = _dn_core(qkv, p, gt, dn_norm_g[j][None, :], lc, hv // 2, n_conv)
            w_out = dn_w_out[j]
        else:
            p = _proj(h, hg_w_in[j], 0, hg_w_in.shape[2]).reshape(b, t, -1)
            o = _hgrn2(p, lb_cum[:, i], hg_norm_g[j][None, :], lc, n_heads)
            w_out = hg_w_out[j]
        xs, h2 = _out_ln(o.reshape(m, -1), w_out.astype(BF16), xs, modtab[i], modtab[i],
                         ln_g[i, 0][None, :], ln_b[i, 0][None, :], *rows, alpha, 2, 3, 4, F32)
        f = _moe(h2, router_w, router_b, moe_w_gate[i].astype(BF16), moe_w_up[i].astype(BF16),
                 moe_w_down[i].astype(BF16))
        nxt = modtab[min(i + 1, depth - 1)]
        xs, h = _comb_ln(f, xs, modtab[i], nxt, ln_g[i, 1][None, :], ln_b[i, 1][None, :],
                         *rows, alpha, 5, 0, 1, BF16)
    return xs.reshape(b, t, d)[:, lc:]
```

```python
import functools
import math

import jax
import jax.numpy as jnp
from jax import lax
from jax.experimental import pallas as pl
from jax.experimental.pallas import tpu as pltpu

F32 = jnp.float32
BF16 = jnp.bfloat16

NORM_EPS = 1e-6
HEAD_DIM = 128
GRID_W = 64
ROPE_BASE = 10000.0
ROPE_FREQS = HEAD_DIM // 4
ATTN_GROUP = 4
DN_CONV_W = 5
DN_CHUNK = 64
HG_CHUNK = 16
HG_SUPER = 64
N_EXPERTS = 16
N_GROUPS = 4
EXPERTS_PER_GROUP = N_EXPERTS // N_GROUPS
TOP_K = 2
EXPERT_BLOCK = 256
CODE_K_SHIFT = 16
CODE_K_UNIT = 1 << CODE_K_SHIFT
N_MOD = 6
VMEM_LIMIT = 56 * 1024 * 1024


def _cparams(*sem):
    return pltpu.CompilerParams(dimension_semantics=sem, vmem_limit_bytes=VMEM_LIMIT)


def _pick(n, prefs):
    for p in prefs:
        if n % p == 0:
            return p
    return n


def _bdot(a, b):
    return jnp.dot(a.astype(BF16), b.astype(BF16), preferred_element_type=F32)


def _bdot_nt(a, b):
    return lax.dot_general(a.astype(BF16), b.astype(BF16), (((1,), (1,)), ((), ())),
                           preferred_element_type=F32)


def _bdot_tn(a, b):
    return lax.dot_general(a.astype(BF16), b.astype(BF16), (((0,), (0,)), ((), ())),
                           preferred_element_type=F32)


def _silu(x):
    return x * jax.nn.sigmoid(x)


def _softplus(x):
    return jnp.maximum(x, 0.0) + jnp.log1p(jnp.exp(-jnp.abs(x)))


def _row_to_col(row, n):
    eye = lax.broadcasted_iota(jnp.int32, (n, n), 0) == lax.broadcasted_iota(jnp.int32, (n, n), 1)
    return jnp.sum(jnp.where(eye, row, 0.0), axis=1, keepdims=True)


def _ada_kernel(cc_ref, w_ref, b_ref, o_ref):
    a = _silu(cc_ref[...])
    o_ref[0] = _bdot(a, w_ref[0]) + b_ref[0]


def _ada_mod(cc, ada_w, ada_b):
    depth, d, n = ada_w.shape
    mp = cc.shape[0]
    tn = _pick(n, (1024, 512, 256, 128))
    return pl.pallas_call(
        _ada_kernel,
        grid=(depth, n // tn),
        in_specs=[pl.BlockSpec((mp, d), lambda i, j: (0, 0)),
                  pl.BlockSpec((1, d, tn), lambda i, j: (i, 0, j)),
                  pl.BlockSpec((1, 1, tn), lambda i, j: (i, 0, j))],
        out_specs=pl.BlockSpec((1, mp, tn), lambda i, j: (i, 0, j)),
        out_shape=jax.ShapeDtypeStruct((depth, mp, n), F32),
        compiler_params=_cparams("parallel", "parallel"),
        name="ada_mod",
    )(cc, ada_w, ada_b.reshape(depth, 1, n))


def _modulate_kernel(x_ref, mod_ref, h_ref, *, shift, scale):
    m = mod_ref[0]
    h_ref[...] = (x_ref[...] * (1.0 + m[scale:scale + 1]) + m[shift:shift + 1]).astype(h_ref.dtype)


def _mod_index(tiles_per_seq, ctx_tiles):
    def index(i):
        return ((i // tiles_per_seq) * 2 + jnp.where(i % tiles_per_seq < ctx_tiles, 0, 1), 0, 0)
    return index


def _modulate(x, modtab, tm, tiles_per_seq, ctx_tiles, shift, scale, dtype):
    m, d = x.shape
    midx = _mod_index(tiles_per_seq, ctx_tiles)
    return pl.pallas_call(
        functools.partial(_modulate_kernel, shift=shift, scale=scale),
        grid=(m // tm,),
        in_specs=[pl.BlockSpec((tm, d), lambda i: (i, 0)),
                  pl.BlockSpec((1, N_MOD, d), midx)],
        out_specs=pl.BlockSpec((tm, d), lambda i: (i, 0)),
        out_shape=jax.ShapeDtypeStruct((m, d), dtype),
        compiler_params=_cparams("parallel"),
        name="modulate",
    )(x, modtab)


def _proj_kernel(a_ref, w_ref, o_ref, wb_ref):
    @pl.when(pl.program_id(1) == 0)
    def _():
        wb_ref[...] = w_ref[...].astype(BF16)

    o_ref[...] = jnp.dot(a_ref[...], wb_ref[...], preferred_element_type=F32).astype(o_ref.dtype)


def _proj(a, w, col0, ncols, out_dtype=F32):
    m, k = a.shape
    tn = _pick(ncols, (1024, 512, 256, 128))
    assert col0 % tn == 0
    tm = _pick(m, (512, 256, 128, 64))
    c0 = col0 // tn
    return pl.pallas_call(
        _proj_kernel,
        grid=(ncols // tn, m // tm),
        in_specs=[pl.BlockSpec((tm, k), lambda j, i: (i, 0)),
                  pl.BlockSpec((k, tn), lambda j, i: (0, c0 + j))],
        out_specs=pl.BlockSpec((tm, tn), lambda j, i: (i, j)),
        out_shape=jax.ShapeDtypeStruct((m, ncols), out_dtype),
        scratch_shapes=[pltpu.VMEM((k, tn), BF16)],
        compiler_params=_cparams("parallel", "arbitrary"),
        name="proj",
    )(a, w)


def _ln_epilogue(y, x, mod, modn, lng, lnb, alpha, gate, shift, scale):
    r = alpha * x + mod[gate:gate + 1] * y
    rc = r - jnp.mean(r, axis=-1, keepdims=True)
    var = jnp.mean(rc * rc, axis=-1, keepdims=True)
    xn = rc * lax.rsqrt(var + NORM_EPS) * lng + lnb
    h = xn * (1.0 + modn[scale:scale + 1]) + modn[shift:shift + 1]
    return xn, h


def _out_ln_kernel(o_ref, w_ref, x_ref, mod_ref, modn_ref, lng_ref, lnb_ref, xn_ref, h_ref, acc_ref,
                   *, alpha, gate, shift, scale, nk):
    kk = pl.program_id(1)
    part = jnp.dot(o_ref[...], w_ref[...], preferred_element_type=F32)

    def finish(y):
        xn, h = _ln_epilogue(y, x_ref[...], mod_ref[0], modn_ref[0], lng_ref[...], lnb_ref[...],
                             alpha, gate, shift, scale)
        xn_ref[...] = xn
        h_ref[...] = h.astype(h_ref.dtype)

    if nk == 1:
        finish(part)
    else:
        @pl.when(kk == 0)
        def _():
            acc_ref[...] = part

        @pl.when((kk > 0) & (kk < nk - 1))
        def _():
            acc_ref[...] += part

        @pl.when(kk == nk - 1)
        def _():
            finish(acc_ref[...] + part)


def _out_ln(o, w, x, mod, modn, lng, lnb, tm, tiles_per_seq, ctx_tiles, alpha, gate, shift, scale, h_dtype):
    m, k = o.shape
    d = w.shape[1]
    tk = _pick(k, (2048, 1024, 512, 256, 128))
    midx = _mod_index(tiles_per_seq, ctx_tiles)
    return pl.pallas_call(
        functools.partial(_out_ln_kernel, alpha=alpha, gate=gate, shift=shift, scale=scale, nk=k // tk),
        grid=(m // tm, k // tk),
        in_specs=[pl.BlockSpec((tm, tk), lambda i, j: (i, j)),
                  pl.BlockSpec((tk, d), lambda i, j: (j, 0)),
                  pl.BlockSpec((tm, d), lambda i, j: (i, 0)),
                  pl.BlockSpec((1, N_MOD, d), lambda i, j: midx(i)),
                  pl.BlockSpec((1, N_MOD, d), lambda i, j: midx(i)),
                  pl.BlockSpec((1, d), lambda i, j: (0, 0)),
                  pl.BlockSpec((1, d), lambda i, j: (0, 0))],
        out_specs=[pl.BlockSpec((tm, d), lambda i, j: (i, 0)),
                   pl.BlockSpec((tm, d), lambda i, j: (i, 0))],
        out_shape=[jax.ShapeDtypeStruct((m, d), F32), jax.ShapeDtypeStruct((m, d), h_dtype)],
        scratch_shapes=[pltpu.VMEM((tm, d), F32)],
        compiler_params=_cparams("parallel", "arbitrary"),
        name="out_ln",
    )(o, w, x, mod, modn, lng, lnb)


def _comb_ln_kernel(f0_ref, f1_ref, rg_ref, x_ref, mod_ref, modn_ref, lng_ref, lnb_ref, xn_ref, h_ref,
                    *, alpha, gate, shift, scale):
    y = f0_ref[...] * rg_ref[:, 0:1] + f1_ref[...] * rg_ref[:, 1:2]
    xn, h = _ln_epilogue(y, x_ref[...], mod_ref[0], modn_ref[0], lng_ref[...], lnb_ref[...],
                         alpha, gate, shift, scale)
    xn_ref[...] = xn
    h_ref[...] = h.astype(h_ref.dtype)


def _comb_ln(f, rg, x, mod, modn, lng, lnb, tm, tiles_per_seq, ctx_tiles, alpha, gate, shift, scale, h_dtype):
    m, d = x.shape
    midx = _mod_index(tiles_per_seq, ctx_tiles)
    nt = m // tm
    return pl.pallas_call(
        functools.partial(_comb_ln_kernel, alpha=alpha, gate=gate, shift=shift, scale=scale),
        grid=(nt,),
        in_specs=[pl.BlockSpec((tm, d), lambda i: (i, 0)),
                  pl.BlockSpec((tm, d), lambda i: (nt + i, 0)),
                  pl.BlockSpec((tm, HEAD_DIM), lambda i: (i, 0)),
                  pl.BlockSpec((tm, d), lambda i: (i, 0)),
                  pl.BlockSpec((1, N_MOD, d), midx),
                  pl.BlockSpec((1, N_MOD, d), midx),
                  pl.BlockSpec((1, d), lambda i: (0, 0)),
                  pl.BlockSpec((1, d), lambda i: (0, 0))],
        out_specs=[pl.BlockSpec((tm, d), lambda i: (i, 0)),
                   pl.BlockSpec((tm, d), lambda i: (i, 0))],
        out_shape=[jax.ShapeDtypeStruct((m, d), F32), jax.ShapeDtypeStruct((m, d), h_dtype)],
        compiler_params=_cparams("parallel"),
        name="comb_ln",
    )(f, f, rg, x, mod, modn, lng, lnb)


def _rope_tables(lc, l):
    pos = jnp.arange(l)
    row = (pos // GRID_W).astype(F32)
    col = (pos % GRID_W).astype(F32)
    inv_freq = ROPE_BASE ** (-jnp.arange(ROPE_FREQS, dtype=F32) / ROPE_FREQS)
    ar = row[:, None] * inv_freq
    ac = col[:, None] * inv_freq
    cos = jnp.concatenate([jnp.cos(ar), jnp.cos(ar), jnp.cos(ac), jnp.cos(ac)], axis=1)
    sin = jnp.concatenate([-jnp.sin(ar), jnp.sin(ar), -jnp.sin(ac), jnp.sin(ac)], axis=1)
    cos = jnp.concatenate([jnp.ones((lc, HEAD_DIM), F32), cos], axis=0)
    sin = jnp.concatenate([jnp.zeros((lc, HEAD_DIM), F32), sin], axis=0)
    return cos, sin


def _norm_rope(x, g, cos, sin):
    y = x * lax.rsqrt(jnp.mean(x * x, axis=-1, keepdims=True) + NORM_EPS) * g
    lane = lax.broadcasted_iota(jnp.int32, y.shape, 1)
    partner = jnp.where(lane % (2 * ROPE_FREQS) < ROPE_FREQS,
                        pltpu.roll(y, HEAD_DIM - ROPE_FREQS, 1), pltpu.roll(y, ROPE_FREQS, 1))
    return y * cos + partner * sin


def _attn_kernel(q_ref, k_ref, v_ref, qg_ref, kg_ref, cos_ref, sin_ref, o_ref, kb_ref, vb_ref,
                 *, lc, tq, t):
    j = pl.program_id(2)

    @pl.when(j == 0)
    def _():
        kb_ref[...] = _norm_rope(k_ref[0], kg_ref[...], cos_ref[...], sin_ref[...]).astype(BF16)
        vb_ref[...] = v_ref[0].astype(BF16)

    r0 = pl.multiple_of(j * tq, tq)
    cos = cos_ref[pl.ds(r0, tq), :]
    sin = sin_ref[pl.ds(r0, tq), :]
    scale = HEAD_DIM ** -0.5

    def attend(n_keys):
        for g in range(ATTN_GROUP):
            qh = _norm_rope(q_ref[0, :, g * HEAD_DIM:(g + 1) * HEAD_DIM], qg_ref[...], cos, sin)
            s = _bdot_nt(qh, kb_ref[0:n_keys, :])
            e = jnp.exp((s - jnp.max(s, axis=-1, keepdims=True)) * scale)
            den = jnp.sum(e, axis=-1, keepdims=True)
            o = jnp.dot(e.astype(BF16), vb_ref[0:n_keys, :], preferred_element_type=F32) / den
            o_ref[0, :, g * HEAD_DIM:(g + 1) * HEAD_DIM] = o.astype(o_ref.dtype)

    @pl.when(j < lc // tq)
    def _():
        attend(lc)

    @pl.when(j >= lc // tq)
    def _():
        attend(t)


def _attention(p, qg, kg, cos, sin, lc, tq):
    b, t, n = p.shape
    hkv = n // HEAD_DIM // (ATTN_GROUP + 2)
    hq = hkv * ATTN_GROUP
    gw = ATTN_GROUP * HEAD_DIM
    return pl.pallas_call(
        functools.partial(_attn_kernel, lc=lc, tq=tq, t=t),
        grid=(b, hkv, t // tq),
        in_specs=[pl.BlockSpec((1, tq, gw), lambda bi, h, j: (bi, j, h)),
                  pl.BlockSpec((1, t, HEAD_DIM), lambda bi, h, j: (bi, 0, hq + h)),
                  pl.BlockSpec((1, t, HEAD_DIM), lambda bi, h, j: (bi, 0, hq + hkv + h)),
                  pl.BlockSpec((1, HEAD_DIM), lambda bi, h, j: (0, 0)),
                  pl.BlockSpec((1, HEAD_DIM), lambda bi, h, j: (0, 0)),
                  pl.BlockSpec((t, HEAD_DIM), lambda bi, h, j: (0, 0)),
                  pl.BlockSpec((t, HEAD_DIM), lambda bi, h, j: (0, 0))],
        out_specs=pl.BlockSpec((1, tq, gw), lambda bi, h, j: (bi, j, h)),
        out_shape=jax.ShapeDtypeStruct((b, t, hq * HEAD_DIM), BF16),
        scratch_shapes=[pltpu.VMEM((t, HEAD_DIM), BF16), pltpu.VMEM((t, HEAD_DIM), BF16)],
        compiler_params=_cparams("parallel", "parallel", "arbitrary"),
        name="attention",
    )(p, p, p, qg, kg, cos, sin)


def _dn_prep_kernel(p_ref, cw_ref, o_ref, *, lc, t, n_qk_tiles):
    c = pl.program_id(1)
    x = p_ref[0]
    row = lax.broadcasted_iota(jnp.int32, x.shape, 0)
    seg = row >= lc
    acc = jnp.zeros_like(x)
    for jj in range(DN_CONV_W):
        off = jj - DN_CONV_W // 2
        if off == 0:
            sh = x
        else:
            src = row + off
            ok = (src >= 0) & (src < t) & ((src >= lc) == seg)
            sh = jnp.where(ok, pltpu.roll(x, (-off) % t, 0), 0.0)
        acc = acc + sh * cw_ref[jj:jj + 1, :]
    y = _silu(acc)
    nrm = lax.rsqrt(jnp.sum(y * y, axis=-1, keepdims=True) + NORM_EPS)
    mult = jnp.where(c < n_qk_tiles, nrm * (HEAD_DIM ** -0.5), jnp.where(c < 2 * n_qk_tiles, nrm, 1.0))
    o_ref[0] = y * mult


def _dn_prep(p, conv_w, lc, n_qk_tiles):
    b, t, _ = p.shape
    nch = conv_w.shape[1]
    return pl.pallas_call(
        functools.partial(_dn_prep_kernel, lc=lc, t=t, n_qk_tiles=n_qk_tiles),
        grid=(b, nch // HEAD_DIM),
        in_specs=[pl.BlockSpec((1, t, HEAD_DIM), lambda bi, c: (bi, 0, c)),
                  pl.BlockSpec((DN_CONV_W, HEAD_DIM), lambda bi, c: (0, c))],
        out_specs=pl.BlockSpec((1, t, HEAD_DIM), lambda bi, c: (bi, 0, c)),
        out_shape=jax.ShapeDtypeStruct((b, t, nch), F32),
        compiler_params=_cparams("parallel", "parallel"),
        name="dn_prep",
    )(p, conv_w)


def _dn_gates_kernel(ba_ref, alog_ref, dtb_ref, o_ref, *, t, hv):
    x = ba_ref[0]
    lane = lax.broadcasted_iota(jnp.int32, (DN_CHUNK, 4 * hv), 1)
    g = -jnp.exp(alog_ref[...]) * _softplus(x + dtb_ref[...])
    ii = lax.broadcasted_iota(jnp.int32, (DN_CHUNK, DN_CHUNK), 0)
    jj = lax.broadcasted_iota(jnp.int32, (DN_CHUNK, DN_CHUNK), 1)
    lower = (ii >= jj).astype(F32)
    upper = (ii <= jj).astype(F32)
    for c in range(t // DN_CHUNK):
        rows = slice(c * DN_CHUNK, (c + 1) * DN_CHUNK)
        gch = g[rows]
        pre = jnp.dot(lower, gch, precision=lax.Precision.HIGHEST, preferred_element_type=F32)
        suf = jnp.dot(upper, gch, precision=lax.Precision.HIGHEST, preferred_element_type=F32)
        o_ref[0, rows, :] = jnp.where(lane < 2 * hv, jax.nn.sigmoid(x[rows]),
                                      jnp.where(lane < 3 * hv, pre, suf))


def _dn_gates(ba, a_log, dt_bias):
    b, t, n = ba.shape
    hv = n // 4
    zeros = jnp.zeros((1, 2 * hv), F32)
    alog = jnp.concatenate([zeros, a_log.reshape(1, 2 * hv)], axis=1)
    dtb = jnp.concatenate([zeros, dt_bias.reshape(1, 2 * hv)], axis=1)
    return pl.pallas_call(
        functools.partial(_dn_gates_kernel, t=t, hv=hv),
        grid=(b,),
        in_specs=[pl.BlockSpec((1, t, n), lambda bi: (bi, 0, 0)),
                  pl.BlockSpec((1, n), lambda bi: (0, 0)),
                  pl.BlockSpec((1, n), lambda bi: (0, 0))],
        out_specs=pl.BlockSpec((1, t, n), lambda bi: (bi, 0, 0)),
        out_shape=jax.ShapeDtypeStruct((b, t, n), F32),
        compiler_params=_cparams("parallel"),
        name="dn_gates",
    )(ba, alog, dtb)


def _dn_chunk_prep(kk, qk_raw, qc, kc, vc, beta_r, gc_r, reverse):
    c = DN_CHUNK
    ii = lax.broadcasted_iota(jnp.int32, (c, c), 0)
    jj = lax.broadcasted_iota(jnp.int32, (c, c), 1)
    incl = (ii <= jj) if reverse else (ii >= jj)
    strict = (ii < jj) if reverse else (ii > jj)
    beta_c = _row_to_col(beta_r, c)
    gc_c = _row_to_col(gc_r, c)
    g_end = gc_r[:, 0:1] if reverse else gc_r[:, c - 1:c]
    decay = jnp.exp(jnp.where(incl, gc_c - gc_r, -jnp.inf))
    m = jnp.where(strict, beta_c * kk * decay, 0.0)
    eg = jnp.exp(gc_c)
    x = jnp.concatenate([vc * beta_c, kc * (beta_c * eg)], axis=1)
    qk = jnp.where(incl, qk_raw * decay, 0.0)
    return (m, x, qk.astype(BF16), (qc * eg).astype(BF16),
            (kc * jnp.exp(g_end - gc_c)).astype(BF16), jnp.exp(g_end))


def _unit_triangular_solve(ms, xs):
    xs = [x - _bdot(m, x) for m, x in zip(ms, xs)]
    pws = ms
    for _ in range(int(math.log2(DN_CHUNK)) - 1):
        pws = [_bdot(p, p) for p in pws]
        xs = [x + _bdot(p, x) for p, x in zip(pws, xs)]
    return xs


def _dn_core_kernel(q_ref, k_ref, v_ref, z_ref, gt_ref, ng_ref, o_ref,
                    u_ref, wq_ref, qk_ref, kd_ref, ge_ref, s_ref, acc_ref, *, lc, t, hv):
    qh = pl.program_id(1)
    c64 = DN_CHUNK
    nc = t // c64
    ncc = lc // c64
    combos = [(sub, d) for sub in range(2) for d in range(2)]

    cpi = _pick(nc, (4, 3, 2))

    def phase_a(it, carry):
        prepped, where = [], []
        for j in range(cpi):
            c = it * cpi + j
            r0 = pl.multiple_of(c * c64, c64)
            rows = pl.ds(r0, c64)
            qc, kc = q_ref[0, rows, :], k_ref[0, rows, :]
            gram = _bdot_nt(jnp.concatenate([kc, qc], axis=0), kc)
            kk, qk_raw = gram[:c64], gram[c64:]
            for ci, (sub, d) in enumerate(combos):
                head = qh * 2 + sub
                beta_r = gt_ref[0, d * hv + head, pl.ds(c, 1), :]
                gc_r = gt_ref[0, (2 + d) * hv + head, pl.ds(c, 1), :]
                vc = v_ref[0, rows, sub * HEAD_DIM:(sub + 1) * HEAD_DIM]
                prepped.append(_dn_chunk_prep(kk, qk_raw, qc, kc, vc, beta_r, gc_r, d == 1))
                where.append((ci, c, r0, rows))
        sols = _unit_triangular_solve([p[0] for p in prepped], [p[1] for p in prepped])
        for (ci, c, r0, rows), x, (_, _, qk, qd, kd, ge) in zip(where, sols, prepped):
            u_ref[ci, rows, :] = x[:, :HEAD_DIM]
            wq_ref[ci, pl.ds(pl.multiple_of(2 * r0, 2 * c64), c64), :] = x[:, HEAD_DIM:].astype(BF16)
            wq_ref[ci, pl.ds(pl.multiple_of(2 * r0 + c64, c64), c64), :] = qd
            qk_ref[ci, rows, :] = qk
            kd_ref[ci, rows, :] = kd
            ge_ref[ci, pl.ds(c, 1), :] = jnp.broadcast_to(ge, (1, HEAD_DIM))
        return carry

    lax.fori_loop(0, nc // cpi, phase_a, 0)

    s_ref[...] = jnp.zeros_like(s_ref)

    def phase_b(n, carry):
        cs = [n if d == 0 else jnp.where(n < ncc, ncc - 1 - n, nc - 1 - (n - ncc)) for _, d in combos]
        r0s = [pl.multiple_of(c * c64, c64) for c in cs]
        ss = [s_ref[ci] for ci in range(4)]
        wss = [jnp.dot(wq_ref[ci, pl.ds(pl.multiple_of(2 * r0s[ci], 2 * c64), 2 * c64), :],
                       ss[ci].astype(BF16), preferred_element_type=F32) for ci in range(4)]
        vns = [(u_ref[ci, pl.ds(r0s[ci], c64), :] - wss[ci][:c64]).astype(BF16) for ci in range(4)]
        upd = [lax.dot_general(kd_ref[ci, pl.ds(r0s[ci], c64), :], vns[ci], (((0,), (0,)), ((), ())),
                               preferred_element_type=F32) for ci in range(4)]
        outs = [wss[ci][c64:] + jnp.dot(qk_ref[ci, pl.ds(r0s[ci], c64), :], vns[ci],
                                        preferred_element_type=F32) for ci in range(4)]
        for ci, (sub, d) in enumerate(combos):
            s_ref[ci] = ss[ci] * ge_ref[ci, pl.ds(cs[ci], 1), :][:, 0:1] + upd[ci]
            acc_ref[d, pl.ds(r0s[ci], c64), sub * HEAD_DIM:(sub + 1) * HEAD_DIM] = outs[ci]
        return carry

    lax.fori_loop(0, nc, phase_b, 0)

    for sub in range(2):
        cols = slice(sub * HEAD_DIM, (sub + 1) * HEAD_DIM)
        o = acc_ref[0, :, cols] + acc_ref[1, :, cols]
        o = o * lax.rsqrt(jnp.mean(o * o, axis=-1, keepdims=True) + NORM_EPS) * ng_ref[...]
        o_ref[0, :, cols] = (o * _silu(z_ref[0, :, cols])).astype(o_ref.dtype)


def _dn_core(qkv, pz, gt, norm_g, lc, n_qk_heads, z_col0):
    b, t, _ = qkv.shape
    hv = 2 * n_qk_heads
    vw = 2 * HEAD_DIM
    nc = t // DN_CHUNK
    return pl.pallas_call(
        functools.partial(_dn_core_kernel, lc=lc, t=t, hv=hv),
        grid=(b, n_qk_heads),
        in_specs=[pl.BlockSpec((1, t, HEAD_DIM), lambda bi, h: (bi, 0, h)),
                  pl.BlockSpec((1, t, HEAD_DIM), lambda bi, h: (bi, 0, n_qk_heads + h)),
                  pl.BlockSpec((1, t, vw), lambda bi, h: (bi, 0, n_qk_heads + h)),
                  pl.BlockSpec((1, t, vw), lambda bi, h: (bi, 0, z_col0 // vw + h)),
                  pl.BlockSpec((1, 4 * hv, nc, DN_CHUNK), lambda bi, h: (bi, 0, 0, 0)),
                  pl.BlockSpec((1, HEAD_DIM), lambda bi, h: (0, 0))],
        out_specs=pl.BlockSpec((1, t, vw), lambda bi, h: (bi, 0, h)),
        out_shape=jax.ShapeDtypeStruct((b, t, hv * HEAD_DIM), BF16),
        scratch_shapes=[pltpu.VMEM((4, t, HEAD_DIM), F32),
                        pltpu.VMEM((4, 2 * t, HEAD_DIM), BF16),
                        pltpu.VMEM((4, t, DN_CHUNK), BF16),
                        pltpu.VMEM((4, t, HEAD_DIM), BF16),
                        pltpu.VMEM((4, -(-nc // 8) * 8, HEAD_DIM), F32),
                        pltpu.VMEM((4, HEAD_DIM, HEAD_DIM), F32),
                        pltpu.VMEM((2, t, vw), F32)],
        compiler_params=_cparams("parallel", "parallel"),
        name="dn_core",
    )(qkv, qkv, qkv, pz, gt, norm_g)


def _chunk_scan(x, reverse, chunk):
    n = x.shape[0]
    pos = lax.broadcasted_iota(jnp.int32, x.shape, 0) % chunk
    sh = 1
    while sh < chunk:
        if reverse:
            x = x + jnp.where(pos < chunk - sh, pltpu.roll(x, n - sh, 0), 0.0)
        else:
            x = x + jnp.where(pos >= sh, pltpu.roll(x, sh, 0), 0.0)
        sh *= 2
    return x


def _hg_kernel(q_ref, i_ref, gate_ref, ff_ref, fb_ref, lb_ref, ng_ref, o_ref,
               g_ref, k_ref, fl_ref, qa_ref, qd_ref, kd_ref, st_ref, acc_ref, *, lc, t):
    cs, c16 = HG_SUPER, HG_CHUNK
    nsub = cs // c16
    nc = t // cs
    ncc = lc // cs
    ii = lax.broadcasted_iota(jnp.int32, (cs, cs), 0)
    jj = lax.broadcasted_iota(jnp.int32, (cs, cs), 1)
    krow = lax.broadcasted_iota(jnp.int32, (cs, HEAD_DIM), 0)
    for d in range(2):
        f_ref = ff_ref if d == 0 else fb_ref
        lb = lb_ref[d:d + 1, :]
        fg = lb + (1.0 - lb) * jax.nn.sigmoid(f_ref[0])
        keys = 1.0 - fg
        logf = jnp.log(fg)
        pre = _chunk_scan(logf, False, cs)
        suf = _chunk_scan(logf, True, cs)
        gcum = suf if d == 1 else pre
        total = pre + suf - logf
        g_ref[d] = gcum
        k_ref[d] = keys
        fl_ref[d] = jnp.exp(total)
        qa_ref[d] = (q_ref[0] * jnp.exp(_chunk_scan(logf, d == 1, c16))).astype(BF16)
        qd_ref[d] = (q_ref[0] * jnp.exp(gcum)).astype(BF16)
        kd_ref[d] = (keys * jnp.exp(total - gcum)).astype(BF16)
    st_ref[...] = jnp.zeros_like(st_ref)

    def body(n, carry):
        r0s = [pl.multiple_of(n * cs, cs),
               pl.multiple_of(jnp.where(n < ncc, ncc - 1 - n, nc - 1 - (n - ncc)) * cs, cs)]
        vvs = [i_ref[0, pl.ds(r0, cs), :].astype(BF16) for r0 in r0s]
        sts = [st_ref[d] for d in range(2)]
        inter = [_bdot_nt(qd_ref[d, pl.ds(r0s[d], cs), :], sts[d]) for d in range(2)]
        upd = [_bdot_tn(vvs[d], kd_ref[d, pl.ds(r0s[d], cs), :]) for d in range(2)]
        scaled = []
        for d in range(2):
            gc = g_ref[d, pl.ds(r0s[d], cs), :]
            kc = k_ref[d, pl.ds(r0s[d], cs), :]
            for a in range(nsub):
                if d == 0:
                    ref = gc[a * c16 - 1:a * c16, :] if a > 0 else None
                    allowed = krow < (a + 1) * c16
                else:
                    ref = gc[(a + 1) * c16:(a + 1) * c16 + 1, :] if a < nsub - 1 else None
                    allowed = krow >= a * c16
                expo = -gc if ref is None else ref - gc
                scaled.append(kc * jnp.exp(jnp.where(allowed, expo, -jnp.inf)))
        parts = [_bdot_nt(qa_ref[d, pl.ds(pl.multiple_of(r0s[d] + a * c16, c16), c16), :], scaled[d * nsub + a])
                 for d in range(2) for a in range(nsub)]
        ps = [jnp.where((ii <= jj) if d == 1 else (ii >= jj),
                        jnp.concatenate(parts[d * nsub:(d + 1) * nsub], axis=0), 0.0) for d in range(2)]
        intra = [_bdot(ps[d], vvs[d]) for d in range(2)]
        for d in range(2):
            acc_ref[d, pl.ds(r0s[d], cs), :] = intra[d] + inter[d]
            st_ref[d] = sts[d] * fl_ref[d, pl.ds(r0s[d], 1), :] + upd[d]
        return carry

    lax.fori_loop(0, nc, body, 0)
    o = acc_ref[0] + acc_ref[1]
    o = o * lax.rsqrt(jnp.mean(o * o, axis=-1, keepdims=True) + NORM_EPS) * ng_ref[...]
    o_ref[0] = (o * jax.nn.sigmoid(gate_ref[0])).astype(o_ref.dtype)


def _hgrn2(p, lb, norm_g, lc, n_heads):
    b, t, _ = p.shape
    assert lc % HG_SUPER == 0 and t % HG_SUPER == 0
    blk = lambda k: pl.BlockSpec((1, t, HEAD_DIM), lambda bi, h, k=k: (bi, 0, k * n_heads + h))
    seq = lambda dt: pltpu.VMEM((2, t, HEAD_DIM), dt)
    return pl.pallas_call(
        functools.partial(_hg_kernel, lc=lc, t=t),
        grid=(b, n_heads),
        in_specs=[blk(0), blk(1), blk(2), blk(3), blk(4),
                  pl.BlockSpec((2, HEAD_DIM), lambda bi, h: (0, h)),
                  pl.BlockSpec((1, HEAD_DIM), lambda bi, h: (0, 0))],
        out_specs=pl.BlockSpec((1, t, HEAD_DIM), lambda bi, h: (bi, 0, h)),
        out_shape=jax.ShapeDtypeStruct((b, t, n_heads * HEAD_DIM), BF16),
        scratch_shapes=[seq(F32), seq(F32), seq(F32), seq(BF16), seq(BF16), seq(BF16),
                        pltpu.VMEM((2, HEAD_DIM, HEAD_DIM), F32), seq(F32)],
        compiler_params=_cparams("parallel", "parallel"),
        name="hgrn2",
    )(p, p, p, p, p, lb, norm_g)


def _split_bf16(x):
    hi = x.astype(BF16)
    return hi, (x - hi.astype(F32)).astype(BF16)


def _router_kernel(h_ref, w_ref, b_ref, e_ref, g_ref):
    h_hi, h_lo = _split_bf16(h_ref[...])
    w_hi, w_lo = _split_bf16(w_ref[...])
    nt = (((1,), (1,)), ((), ()))
    logits = (lax.dot_general(w_hi, h_hi, nt, preferred_element_type=F32)
              + lax.dot_general(w_hi, h_lo, nt, preferred_element_type=F32)
              + lax.dot_general(w_lo, h_hi, nt, preferred_element_type=F32))
    scores = jax.nn.sigmoid(logits)
    sel = scores + b_ref[...]
    epg = EXPERTS_PER_GROUP
    rows = lambda a, e: a[e:e + 1, :]
    best_gs = None
    for g in range(N_GROUPS):
        gs = None
        for a in range(epg):
            for c in range(a + 1, epg):
                pair = rows(sel, g * epg + a) + rows(sel, g * epg + c)
                gs = pair if gs is None else jnp.maximum(gs, pair)
        if best_gs is None:
            best_gs, group = gs, jnp.zeros(gs.shape, jnp.int32)
        else:
            upd = gs > best_gs
            best_gs = jnp.where(upd, gs, best_gs)
            group = jnp.where(upd, g, group)

    def in_group(a, l):
        v = rows(a, l)
        for g in range(1, N_GROUPS):
            v = jnp.where(group == g, rows(a, g * epg + l), v)
        return v

    sel_l = [in_group(sel, l) for l in range(epg)]
    sc_l = [in_group(scores, l) for l in range(epg)]

    def top1(exclude):
        best, idx, sc = None, None, None
        for l in range(epg):
            v = sel_l[l] if exclude is None else jnp.where(exclude == l, -jnp.inf, sel_l[l])
            if best is None:
                best, idx, sc = v, jnp.zeros(v.shape, jnp.int32), sc_l[l]
            else:
                upd = v > best
                best = jnp.where(upd, v, best)
                idx = jnp.where(upd, l, idx)
                sc = jnp.where(upd, sc_l[l], sc)
        return idx, sc

    l0, s0 = top1(None)
    l1, s1 = top1(l0)
    e_ref[0:1, :] = group * epg + l0
    e_ref[1:2, :] = group * epg + l1
    den = s0 + s1
    g_ref[0:1, :] = s0 / den
    g_ref[1:2, :] = s1 / den


def _router(h, router_w, router_b):
    m, d = h.shape
    tm = _pick(m, (512, 256, 128))
    ne = router_w.shape[1]
    return pl.pallas_call(
        _router_kernel,
        grid=(m // tm,),
        in_specs=[pl.BlockSpec((tm, d), lambda i: (i, 0)),
                  pl.BlockSpec((ne, d), lambda i: (0, 0)),
                  pl.BlockSpec((ne, 1), lambda i: (0, 0))],
        out_specs=[pl.BlockSpec((TOP_K, tm), lambda i: (0, i)),
                   pl.BlockSpec((TOP_K, tm), lambda i: (0, i))],
        out_shape=[jax.ShapeDtypeStruct((TOP_K, m), jnp.int32), jax.ShapeDtypeStruct((TOP_K, m), F32)],
        compiler_params=_cparams("parallel"),
        name="router",
    )(h, router_w.T, router_b.reshape(ne, 1))


def _moe_ffn_kernel(bexp_ref, code_ref, h_hbm, wg_ref, wu_ref, wd_ref, f_hbm,
                    x0, x1, y0, y1, sem_in, sem_out, *, n_tok):
    i = pl.program_id(0)
    last = pl.num_programs(0) - 1
    eb = EXPERT_BLOCK
    xbufs, ybufs = (x0, x1), (y0, y1)

    def gather(blk, sl, r):
        tok = code_ref[(blk + 1) * eb + r] & (CODE_K_UNIT - 1)
        return pltpu.make_async_copy(h_hbm.at[pl.ds(tok, 1)], xbufs[sl].at[pl.ds(r, 1)], sem_in.at[sl])

    def scatter(blk, sl, r):
        code = code_ref[(blk + 1) * eb + r]
        dst = lax.shift_right_logical(code, CODE_K_SHIFT) * n_tok + (code & (CODE_K_UNIT - 1))
        return pltpu.make_async_copy(ybufs[sl].at[pl.ds(r, 1)], f_hbm.at[pl.ds(dst, 1)], sem_out.at[sl])

    def for_rows(fn):
        def body(r, carry):
            fn(r)
            return carry
        lax.fori_loop(0, eb, body, 0, unroll=8)

    @pl.when(i == 0)
    def _():
        for sl in range(2):
            ybufs[sl][...] = jnp.zeros_like(ybufs[sl])
            spare = pltpu.make_async_copy(ybufs[sl], f_hbm.at[pl.ds(TOP_K * n_tok + sl * eb, eb)],
                                          sem_out.at[sl])
            spare.start()
            spare.wait()
        for_rows(lambda r: gather(0, 0, r).start())

    def step(sl):
        @pl.when(i >= 1)
        def _():
            for_rows(lambda r: scatter(i - 2, sl, r).wait())
        for_rows(lambda r: gather(i, sl, r).wait())
        x = xbufs[sl][...].astype(BF16)
        for r in range(eb):
            gather(i + 1, 1 - sl, r).start()
        a = _silu(jnp.dot(x, wg_ref[0], preferred_element_type=F32)) \
            * jnp.dot(x, wu_ref[0], preferred_element_type=F32)
        for r in range(eb):
            scatter(i - 1, 1 - sl, r).start()
        ybufs[sl][...] = jnp.dot(a.astype(BF16), wd_ref[0], preferred_element_type=F32)

    @pl.when(i % 2 == 0)
    def _():
        step(0)

    @pl.when(i % 2 == 1)
    def _():
        step(1)

    @pl.when(i == last)
    def _():
        for sl in range(2):
            @pl.when(i % 2 == sl)
            def _():
                for_rows(lambda r: scatter(i - 1, 1 - sl, r).wait())
                for_rows(lambda r: gather(i + 1, 1 - sl, r).wait())


def _moe_ffn(h, block_expert, code_buf, wg, wu, wd):
    m, d = h.shape
    de = wg.shape[2]
    nb = block_expert.shape[0]
    wmap = lambda i, be, cb: (be[jnp.minimum(i, nb - 1)], 0, 0)
    buf = pltpu.VMEM((EXPERT_BLOCK, d), F32)
    grid_spec = pltpu.PrefetchScalarGridSpec(
        num_scalar_prefetch=2,
        grid=(nb + 1,),
        in_specs=[pl.BlockSpec(memory_space=pl.ANY),
                  pl.BlockSpec((1, d, de), wmap),
                  pl.BlockSpec((1, d, de), wmap),
                  pl.BlockSpec((1, de, d), wmap)],
        out_specs=pl.BlockSpec(memory_space=pl.ANY),
        scratch_shapes=[buf, buf, buf, buf, pltpu.SemaphoreType.DMA((2,)), pltpu.SemaphoreType.DMA((2,))],
    )
    return pl.pallas_call(
        functools.partial(_moe_ffn_kernel, n_tok=m),
        grid_spec=grid_spec,
        out_shape=jax.ShapeDtypeStruct((TOP_K * m + 2 * EXPERT_BLOCK, d), F32),
        compiler_params=_cparams("arbitrary"),
        name="moe_ffn",
    )(block_expert, code_buf, h, wg, wu, wd)


def _moe_dispatch(expert):
    m = expert.shape[1]
    n_pairs = m * TOP_K
    e_flat = expert.T.reshape(-1)
    onehot = (e_flat[:, None] == jnp.arange(N_EXPERTS)[None, :]).astype(jnp.int32)
    csum = jnp.cumsum(onehot, axis=0)
    rank = jnp.take_along_axis(csum, e_flat[:, None], axis=1)[:, 0] - 1
    counts = csum[-1]
    padded = (counts + EXPERT_BLOCK - 1) // EXPERT_BLOCK * EXPERT_BLOCK
    pend = jnp.cumsum(padded)
    pstart = pend - padded
    dest = pstart[e_flat] + rank
    n_blocks = -(-(n_pairs + N_EXPERTS * (EXPERT_BLOCK - 1)) // EXPERT_BLOCK)
    n_rows = n_blocks * EXPERT_BLOCK
    assert 2 * EXPERT_BLOCK <= m <= CODE_K_UNIT
    pair = jnp.arange(n_pairs, dtype=jnp.int32)
    row = jnp.arange(n_rows + 3 * EXPERT_BLOCK, dtype=jnp.int32)
    pad_code = TOP_K * CODE_K_UNIT + row % (2 * EXPERT_BLOCK)
    code_buf = pad_code.at[dest + EXPERT_BLOCK].set((pair % TOP_K) * CODE_K_UNIT + pair // TOP_K)
    blk_start = jnp.arange(n_blocks) * EXPERT_BLOCK
    block_expert = jnp.minimum(jnp.searchsorted(pend, blk_start, side='right'), N_EXPERTS - 1).astype(jnp.int32)
    return block_expert, code_buf


def _moe(h2, router_w, router_b, wg, wu, wd):
    expert, gate = _router(h2, router_w, router_b)
    block_expert, code_buf = _moe_dispatch(expert)
    f = _moe_ffn(h2, block_expert, code_buf, wg, wu, wd)
    return f, jnp.pad(gate.T, ((0, 0), (0, HEAD_DIM - TOP_K)))


def kernel(x, c, ctx, c_ctx, ada_w, ada_b, ln_g, ln_b, attn_w_in, attn_q_g, attn_k_g, attn_w_out, dn_w_in, dn_conv, dn_a_log, dn_dt_bias, dn_norm_g, dn_w_out, hg_w_in, hg_lb, hg_norm_g, hg_w_out, router_w, router_b, moe_w_gate, moe_w_up, moe_w_down):
    b, l, d = x.shape
    lc = ctx.shape[1]
    t = lc + l
    m = b * t
    depth = ada_w.shape[0]
    alpha = (2.0 * depth) ** 0.25
    tm = _pick(math.gcd(lc, l), (256, 128, 64))
    tiles_per_seq, ctx_tiles = t // tm, lc // tm
    n_heads = d // HEAD_DIM

    mp = -(-(b + 1) // 8) * 8
    cc = jnp.concatenate([c, c_ctx[None, :], jnp.zeros((mp - b - 1, d), F32)], axis=0)
    mod = _ada_mod(cc, ada_w, ada_b)
    mod_l = mod[:, :b].reshape(depth, b, 1, N_MOD, d)
    mod_c = jnp.broadcast_to(mod[:, b].reshape(depth, 1, 1, N_MOD, d), (depth, b, 1, N_MOD, d))
    modtab = jnp.concatenate([mod_c, mod_l], axis=2).reshape(depth, 2 * b, N_MOD, d)

    lb_cum = jnp.cumsum(jax.nn.softmax(hg_lb.astype(F32), axis=1), axis=1)
    lb_cum = lb_cum - lb_cum[:, :1]

    xs = jnp.concatenate([ctx, x], axis=1).reshape(m, d)
    h = _modulate(xs, modtab[0], tm, tiles_per_seq, ctx_tiles, 0, 1, BF16)
    cos, sin = _rope_tables(lc, l)
    rows = (tm, tiles_per_seq, ctx_tiles)

    for i in range(depth):
        kind, j = i % 3, i // 3
        if kind == 0:
            p = _proj(h, attn_w_in[j], 0, attn_w_in.shape[2])
            o = _attention(p.reshape(b, t, -1), attn_q_g[j][None, :], attn_k_g[j][None, :], cos, sin, lc, tm)
            w_out = attn_w_out[j]
        elif kind == 1:
            n_conv = dn_conv.shape[2]
            hv = dn_a_log.shape[2]
            n_main = n_conv + hv * HEAD_DIM
            p = _proj(h, dn_w_in[j], 0, n_main).reshape(b, t, n_main)
            ba = _proj(h, dn_w_in[j], n_main, 4 * hv).reshape(b, t, 4 * hv)
            qkv = _dn_prep(p, dn_conv[j], lc, hv // 2)
            gates = _dn_gates(ba, dn_a_log[j], dn_dt_bias[j])
            gt = jnp.swapaxes(gates, 1, 2).reshape(b, 4 * hv, t // DN_CHUNK, DN_CHUNK)
            o = _dn_core(qkv, p, gt, dn_norm_g[j][None, :], lc, hv // 2, n_conv)
            w_out = dn_w_out[j]
        else:
            p = _proj(h, hg_w_in[j], 0, hg_w_in.shape[2]).reshape(b, t, -1)
            o = _hgrn2(p, lb_cum[:, i], hg_norm_g[j][None, :], lc, n_heads)
            w_out = hg_w_out[j]
        xs, h2 = _out_ln(o.reshape(m, -1), w_out.astype(BF16), xs, modtab[i], modtab[i],
                         ln_g[i, 0][None, :], ln_b[i, 0][None, :], *rows, alpha, 2, 3, 4, F32)
        f, rg = _moe(h2, router_w, router_b, moe_w_gate[i].astype(BF16), moe_w_up[i].astype(BF16),
                     moe_w_down[i].astype(BF16))
        nxt = modtab[min(i + 1, depth - 1)]
        xs, h = _comb_ln(f, rg, xs, modtab[i], nxt, ln_g[i, 1][None, :], ln_b[i, 1][None, :],
                         *rows, alpha, 5, 0, 1, BF16)
    return xs.reshape(b, t, d)[:, lc:]
```

```python
import functools
import math

import jax
import jax.numpy as jnp
from jax import lax
from jax.experimental import pallas as pl
from jax.experimental.pallas import tpu as pltpu

F32 = jnp.float32
BF16 = jnp.bfloat16

NORM_EPS = 1e-6
HEAD_DIM = 128
GRID_W = 64
ROPE_BASE = 10000.0
ROPE_FREQS = HEAD_DIM // 4
ATTN_GROUP = 4
DN_CONV_W = 5
DN_CHUNK = 64
HG_CHUNK = 16
HG_SUPER = 64
N_EXPERTS = 16
N_GROUPS = 4
EXPERTS_PER_GROUP = N_EXPERTS // N_GROUPS
TOP_K = 2
EXPERT_BLOCK = 256
CODE_K_SHIFT = 16
CODE_K_UNIT = 1 << CODE_K_SHIFT
N_MOD = 6
VMEM_LIMIT = 56 * 1024 * 1024


def _cparams(*sem):
    return pltpu.CompilerParams(dimension_semantics=sem, vmem_limit_bytes=VMEM_LIMIT)


def _pick(n, prefs):
    for p in prefs:
        if n % p == 0:
            return p
    return n


def _bdot(a, b):
    return jnp.dot(a.astype(BF16), b.astype(BF16), preferred_element_type=F32)


def _bdot_nt(a, b):
    return lax.dot_general(a.astype(BF16), b.astype(BF16), (((1,), (1,)), ((), ())),
                           preferred_element_type=F32)


def _bdot_tn(a, b):
    return lax.dot_general(a.astype(BF16), b.astype(BF16), (((0,), (0,)), ((), ())),
                           preferred_element_type=F32)


def _silu(x):
    return x * jax.nn.sigmoid(x)


def _softplus(x):
    return jnp.maximum(x, 0.0) + jnp.log1p(jnp.exp(-jnp.abs(x)))


def _row_to_col(row, n):
    eye = lax.broadcasted_iota(jnp.int32, (n, n), 0) == lax.broadcasted_iota(jnp.int32, (n, n), 1)
    return jnp.sum(jnp.where(eye, row, 0.0), axis=1, keepdims=True)


def _ada_kernel(cc_ref, w_ref, b_ref, o_ref):
    a = _silu(cc_ref[...])
    o_ref[0] = _bdot(a, w_ref[0]) + b_ref[0]


def _ada_mod(cc, ada_w, ada_b):
    depth, d, n = ada_w.shape
    mp = cc.shape[0]
    tn = _pick(n, (1024, 512, 256, 128))
    return pl.pallas_call(
        _ada_kernel,
        grid=(depth, n // tn),
        in_specs=[pl.BlockSpec((mp, d), lambda i, j: (0, 0)),
                  pl.BlockSpec((1, d, tn), lambda i, j: (i, 0, j)),
                  pl.BlockSpec((1, 1, tn), lambda i, j: (i, 0, j))],
        out_specs=pl.BlockSpec((1, mp, tn), lambda i, j: (i, 0, j)),
        out_shape=jax.ShapeDtypeStruct((depth, mp, n), F32),
        compiler_params=_cparams("parallel", "parallel"),
        name="ada_mod",
    )(cc, ada_w, ada_b.reshape(depth, 1, n))


def _modulate_kernel(x_ref, mod_ref, h_ref, *, shift, scale):
    m = mod_ref[0]
    h_ref[...] = (x_ref[...] * (1.0 + m[scale:scale + 1]) + m[shift:shift + 1]).astype(h_ref.dtype)


def _mod_index(tiles_per_seq, ctx_tiles):
    def index(i):
        return ((i // tiles_per_seq) * 2 + jnp.where(i % tiles_per_seq < ctx_tiles, 0, 1), 0, 0)
    return index


def _modulate(x, modtab, tm, tiles_per_seq, ctx_tiles, shift, scale, dtype):
    m, d = x.shape
    midx = _mod_index(tiles_per_seq, ctx_tiles)
    return pl.pallas_call(
        functools.partial(_modulate_kernel, shift=shift, scale=scale),
        grid=(m // tm,),
        in_specs=[pl.BlockSpec((tm, d), lambda i: (i, 0)),
                  pl.BlockSpec((1, N_MOD, d), midx)],
        out_specs=pl.BlockSpec((tm, d), lambda i: (i, 0)),
        out_shape=jax.ShapeDtypeStruct((m, d), dtype),
        compiler_params=_cparams("parallel"),
        name="modulate",
    )(x, modtab)


def _proj_kernel(a_ref, w_ref, o_ref, wb_ref):
    @pl.when(pl.program_id(1) == 0)
    def _():
        wb_ref[...] = w_ref[...].astype(BF16)

    o_ref[...] = jnp.dot(a_ref[...], wb_ref[...], preferred_element_type=F32).astype(o_ref.dtype)


def _proj(a, w, col0, ncols, out_dtype=F32):
    m, k = a.shape
    tn = _pick(ncols, (1024, 512, 256, 128))
    assert col0 % tn == 0
    tm = _pick(m, (512, 256, 128, 64))
    c0 = col0 // tn
    return pl.pallas_call(
        _proj_kernel,
        grid=(ncols // tn, m // tm),
        in_specs=[pl.BlockSpec((tm, k), lambda j, i: (i, 0)),
                  pl.BlockSpec((k, tn), lambda j, i: (0, c0 + j))],
        out_specs=pl.BlockSpec((tm, tn), lambda j, i: (i, j)),
        out_shape=jax.ShapeDtypeStruct((m, ncols), out_dtype),
        scratch_shapes=[pltpu.VMEM((k, tn), BF16)],
        compiler_params=_cparams("parallel", "arbitrary"),
        name="proj",
    )(a, w)


def _ln_epilogue(y, x, mod, modn, lng, lnb, alpha, gate, shift, scale):
    r = alpha * x + mod[gate:gate + 1] * y
    rc = r - jnp.mean(r, axis=-1, keepdims=True)
    var = jnp.mean(rc * rc, axis=-1, keepdims=True)
    xn = rc * lax.rsqrt(var + NORM_EPS) * lng + lnb
    h = xn * (1.0 + modn[scale:scale + 1]) + modn[shift:shift + 1]
    return xn, h


def _out_ln_kernel(o_ref, w_ref, x_ref, mod_ref, modn_ref, lng_ref, lnb_ref, xn_ref, h_ref, acc_ref,
                   *, alpha, gate, shift, scale, nk):
    kk = pl.program_id(1)
    part = jnp.dot(o_ref[...], w_ref[...], preferred_element_type=F32)

    def finish(y):
        xn, h = _ln_epilogue(y, x_ref[...], mod_ref[0], modn_ref[0], lng_ref[...], lnb_ref[...],
                             alpha, gate, shift, scale)
        xn_ref[...] = xn
        h_ref[...] = h.astype(h_ref.dtype)

    if nk == 1:
        finish(part)
    else:
        @pl.when(kk == 0)
        def _():
            acc_ref[...] = part

        @pl.when((kk > 0) & (kk < nk - 1))
        def _():
            acc_ref[...] += part

        @pl.when(kk == nk - 1)
        def _():
            finish(acc_ref[...] + part)


def _out_ln(o, w, x, mod, modn, lng, lnb, tm, tiles_per_seq, ctx_tiles, alpha, gate, shift, scale, h_dtype):
    m, k = o.shape
    d = w.shape[1]
    tk = _pick(k, (2048, 1024, 512, 256, 128))
    midx = _mod_index(tiles_per_seq, ctx_tiles)
    return pl.pallas_call(
        functools.partial(_out_ln_kernel, alpha=alpha, gate=gate, shift=shift, scale=scale, nk=k // tk),
        grid=(m // tm, k // tk),
        in_specs=[pl.BlockSpec((tm, tk), lambda i, j: (i, j)),
                  pl.BlockSpec((tk, d), lambda i, j: (j, 0)),
                  pl.BlockSpec((tm, d), lambda i, j: (i, 0)),
                  pl.BlockSpec((1, N_MOD, d), lambda i, j: midx(i)),
                  pl.BlockSpec((1, N_MOD, d), lambda i, j: midx(i)),
                  pl.BlockSpec((1, d), lambda i, j: (0, 0)),
                  pl.BlockSpec((1, d), lambda i, j: (0, 0))],
        out_specs=[pl.BlockSpec((tm, d), lambda i, j: (i, 0)),
                   pl.BlockSpec((tm, d), lambda i, j: (i, 0))],
        out_shape=[jax.ShapeDtypeStruct((m, d), F32), jax.ShapeDtypeStruct((m, d), h_dtype)],
        scratch_shapes=[pltpu.VMEM((tm, d), F32)],
        compiler_params=_cparams("parallel", "arbitrary"),
        name="out_ln",
    )(o, w, x, mod, modn, lng, lnb)


def _comb_ln_kernel(f0_ref, f1_ref, rg_ref, x_ref, mod_ref, modn_ref, lng_ref, lnb_ref, xn_ref, h_ref,
                    *, alpha, gate, shift, scale):
    y = f0_ref[...] * rg_ref[:, 0:1] + f1_ref[...] * rg_ref[:, 1:2]
    xn, h = _ln_epilogue(y, x_ref[...], mod_ref[0], modn_ref[0], lng_ref[...], lnb_ref[...],
                         alpha, gate, shift, scale)
    xn_ref[...] = xn
    h_ref[...] = h.astype(h_ref.dtype)


def _comb_ln(f, rg, x, mod, modn, lng, lnb, tm, tiles_per_seq, ctx_tiles, alpha, gate, shift, scale, h_dtype,
             latent_only=False):
    m, d = x.shape
    midx = _mod_index(tiles_per_seq, ctx_tiles)
    nt = m // tm
    if latent_only:
        lat_tiles = tiles_per_seq - ctx_tiles
        n_out = nt // tiles_per_seq * lat_tiles
        src = lambda i: (i // lat_tiles) * tiles_per_seq + ctx_tiles + i % lat_tiles
    else:
        n_out = nt
        src = lambda i: i
    return pl.pallas_call(
        functools.partial(_comb_ln_kernel, alpha=alpha, gate=gate, shift=shift, scale=scale),
        grid=(n_out,),
        in_specs=[pl.BlockSpec((tm, d), lambda i: (src(i), 0)),
                  pl.BlockSpec((tm, d), lambda i: (nt + src(i), 0)),
                  pl.BlockSpec((tm, HEAD_DIM), lambda i: (src(i), 0)),
                  pl.BlockSpec((tm, d), lambda i: (src(i), 0)),
                  pl.BlockSpec((1, N_MOD, d), lambda i: midx(src(i))),
                  pl.BlockSpec((1, N_MOD, d), lambda i: midx(src(i))),
                  pl.BlockSpec((1, d), lambda i: (0, 0)),
                  pl.BlockSpec((1, d), lambda i: (0, 0))],
        out_specs=[pl.BlockSpec((tm, d), lambda i: (i, 0)),
                   pl.BlockSpec((tm, d), lambda i: (i, 0))],
        out_shape=[jax.ShapeDtypeStruct((n_out * tm, d), F32), jax.ShapeDtypeStruct((n_out * tm, d), h_dtype)],
        compiler_params=_cparams("parallel"),
        name="comb_ln",
    )(f, f, rg, x, mod, modn, lng, lnb)


def _rope_tables(lc, l):
    pos = jnp.arange(l)
    row = (pos // GRID_W).astype(F32)
    col = (pos % GRID_W).astype(F32)
    inv_freq = ROPE_BASE ** (-jnp.arange(ROPE_FREQS, dtype=F32) / ROPE_FREQS)
    ar = row[:, None] * inv_freq
    ac = col[:, None] * inv_freq
    cos = jnp.concatenate([jnp.cos(ar), jnp.cos(ar), jnp.cos(ac), jnp.cos(ac)], axis=1)
    sin = jnp.concatenate([-jnp.sin(ar), jnp.sin(ar), -jnp.sin(ac), jnp.sin(ac)], axis=1)
    cos = jnp.concatenate([jnp.ones((lc, HEAD_DIM), F32), cos], axis=0)
    sin = jnp.concatenate([jnp.zeros((lc, HEAD_DIM), F32), sin], axis=0)
    return cos, sin


def _norm_rope(x, g, cos, sin):
    y = x * lax.rsqrt(jnp.mean(x * x, axis=-1, keepdims=True) + NORM_EPS) * g
    lane = lax.broadcasted_iota(jnp.int32, y.shape, 1)
    partner = jnp.where(lane % (2 * ROPE_FREQS) < ROPE_FREQS,
                        pltpu.roll(y, HEAD_DIM - ROPE_FREQS, 1), pltpu.roll(y, ROPE_FREQS, 1))
    return y * cos + partner * sin


def _attn_kernel(q_ref, k_ref, v_ref, qg_ref, kg_ref, cos_ref, sin_ref, o_ref, kb_ref, vb_ref,
                 *, lc, tq, t):
    j = pl.program_id(2)

    @pl.when(j == 0)
    def _():
        kb_ref[...] = _norm_rope(k_ref[0], kg_ref[...], cos_ref[...], sin_ref[...]).astype(BF16)
        vb_ref[...] = v_ref[0].astype(BF16)

    r0 = pl.multiple_of(j * tq, tq)
    cos = cos_ref[pl.ds(r0, tq), :]
    sin = sin_ref[pl.ds(r0, tq), :]
    scale = HEAD_DIM ** -0.5

    def attend(n_keys):
        for g in range(ATTN_GROUP):
            qh = _norm_rope(q_ref[0, :, g * HEAD_DIM:(g + 1) * HEAD_DIM], qg_ref[...], cos, sin)
            s = _bdot_nt(qh, kb_ref[0:n_keys, :])
            e = jnp.exp((s - jnp.max(s, axis=-1, keepdims=True)) * scale)
            den = jnp.sum(e, axis=-1, keepdims=True)
            o = jnp.dot(e.astype(BF16), vb_ref[0:n_keys, :], preferred_element_type=F32) / den
            o_ref[0, :, g * HEAD_DIM:(g + 1) * HEAD_DIM] = o.astype(o_ref.dtype)

    @pl.when(j < lc // tq)
    def _():
        attend(lc)

    @pl.when(j >= lc // tq)
    def _():
        attend(t)


def _attention(p, qg, kg, cos, sin, lc, tq):
    b, t, n = p.shape
    hkv = n // HEAD_DIM // (ATTN_GROUP + 2)
    hq = hkv * ATTN_GROUP
    gw = ATTN_GROUP * HEAD_DIM
    return pl.pallas_call(
        functools.partial(_attn_kernel, lc=lc, tq=tq, t=t),
        grid=(b, hkv, t // tq),
        in_specs=[pl.BlockSpec((1, tq, gw), lambda bi, h, j: (bi, j, h)),
                  pl.BlockSpec((1, t, HEAD_DIM), lambda bi, h, j: (bi, 0, hq + h)),
                  pl.BlockSpec((1, t, HEAD_DIM), lambda bi, h, j: (bi, 0, hq + hkv + h)),
                  pl.BlockSpec((1, HEAD_DIM), lambda bi, h, j: (0, 0)),
                  pl.BlockSpec((1, HEAD_DIM), lambda bi, h, j: (0, 0)),
                  pl.BlockSpec((t, HEAD_DIM), lambda bi, h, j: (0, 0)),
                  pl.BlockSpec((t, HEAD_DIM), lambda bi, h, j: (0, 0))],
        out_specs=pl.BlockSpec((1, tq, gw), lambda bi, h, j: (bi, j, h)),
        out_shape=jax.ShapeDtypeStruct((b, t, hq * HEAD_DIM), BF16),
        scratch_shapes=[pltpu.VMEM((t, HEAD_DIM), BF16), pltpu.VMEM((t, HEAD_DIM), BF16)],
        compiler_params=_cparams("parallel", "parallel", "arbitrary"),
        name="attention",
    )(p, p, p, qg, kg, cos, sin)


def _dn_prep_kernel(p_ref, cw_ref, o_ref, *, lc, t, n_qk_tiles):
    c = pl.program_id(1)
    x = p_ref[0]
    row = lax.broadcasted_iota(jnp.int32, x.shape, 0)
    seg = row >= lc
    acc = jnp.zeros_like(x)
    for jj in range(DN_CONV_W):
        off = jj - DN_CONV_W // 2
        if off == 0:
            sh = x
        else:
            src = row + off
            ok = (src >= 0) & (src < t) & ((src >= lc) == seg)
            sh = jnp.where(ok, pltpu.roll(x, (-off) % t, 0), 0.0)
        acc = acc + sh * cw_ref[jj:jj + 1, :]
    y = _silu(acc)
    nrm = lax.rsqrt(jnp.sum(y * y, axis=-1, keepdims=True) + NORM_EPS)
    mult = jnp.where(c < n_qk_tiles, nrm * (HEAD_DIM ** -0.5), jnp.where(c < 2 * n_qk_tiles, nrm, 1.0))
    o_ref[0] = y * mult


def _dn_prep(p, conv_w, lc, n_qk_tiles):
    b, t, _ = p.shape
    nch = conv_w.shape[1]
    return pl.pallas_call(
        functools.partial(_dn_prep_kernel, lc=lc, t=t, n_qk_tiles=n_qk_tiles),
        grid=(b, nch // HEAD_DIM),
        in_specs=[pl.BlockSpec((1, t, HEAD_DIM), lambda bi, c: (bi, 0, c)),
                  pl.BlockSpec((DN_CONV_W, HEAD_DIM), lambda bi, c: (0, c))],
        out_specs=pl.BlockSpec((1, t, HEAD_DIM), lambda bi, c: (bi, 0, c)),
        out_shape=jax.ShapeDtypeStruct((b, t, nch), F32),
        compiler_params=_cparams("parallel", "parallel"),
        name="dn_prep",
    )(p, conv_w)


def _dn_gates_kernel(ba_ref, alog_ref, dtb_ref, o_ref, *, t, hv):
    x = ba_ref[0]
    lane = lax.broadcasted_iota(jnp.int32, (DN_CHUNK, 4 * hv), 1)
    g = -jnp.exp(alog_ref[...]) * _softplus(x + dtb_ref[...])
    ii = lax.broadcasted_iota(jnp.int32, (DN_CHUNK, DN_CHUNK), 0)
    jj = lax.broadcasted_iota(jnp.int32, (DN_CHUNK, DN_CHUNK), 1)
    lower = (ii >= jj).astype(F32)
    upper = (ii <= jj).astype(F32)
    for c in range(t // DN_CHUNK):
        rows = slice(c * DN_CHUNK, (c + 1) * DN_CHUNK)
        gch = g[rows]
        pre = jnp.dot(lower, gch, precision=lax.Precision.HIGHEST, preferred_element_type=F32)
        suf = jnp.dot(upper, gch, precision=lax.Precision.HIGHEST, preferred_element_type=F32)
        o_ref[0, rows, :] = jnp.where(lane < 2 * hv, jax.nn.sigmoid(x[rows]),
                                      jnp.where(lane < 3 * hv, pre, suf))


def _dn_gates(ba, a_log, dt_bias):
    b, t, n = ba.shape
    hv = n // 4
    zeros = jnp.zeros((1, 2 * hv), F32)
    alog = jnp.concatenate([zeros, a_log.reshape(1, 2 * hv)], axis=1)
    dtb = jnp.concatenate([zeros, dt_bias.reshape(1, 2 * hv)], axis=1)
    return pl.pallas_call(
        functools.partial(_dn_gates_kernel, t=t, hv=hv),
        grid=(b,),
        in_specs=[pl.BlockSpec((1, t, n), lambda bi: (bi, 0, 0)),
                  pl.BlockSpec((1, n), lambda bi: (0, 0)),
                  pl.BlockSpec((1, n), lambda bi: (0, 0))],
        out_specs=pl.BlockSpec((1, t, n), lambda bi: (bi, 0, 0)),
        out_shape=jax.ShapeDtypeStruct((b, t, n), F32),
        compiler_params=_cparams("parallel"),
        name="dn_gates",
    )(ba, alog, dtb)


def _dn_chunk_prep(kk, qk_raw, qc, kc, vc, beta_r, gc_r, reverse):
    c = DN_CHUNK
    ii = lax.broadcasted_iota(jnp.int32, (c, c), 0)
    jj = lax.broadcasted_iota(jnp.int32, (c, c), 1)
    incl = (ii <= jj) if reverse else (ii >= jj)
    strict = (ii < jj) if reverse else (ii > jj)
    beta_c = _row_to_col(beta_r, c)
    gc_c = _row_to_col(gc_r, c)
    g_end = gc_r[:, 0:1] if reverse else gc_r[:, c - 1:c]
    decay = jnp.exp(jnp.where(incl, gc_c - gc_r, -jnp.inf))
    m = jnp.where(strict, beta_c * kk * decay, 0.0)
    eg = jnp.exp(gc_c)
    x = jnp.concatenate([vc * beta_c, kc * (beta_c * eg)], axis=1)
    qk = jnp.where(incl, qk_raw * decay, 0.0)
    return (m, x, qk.astype(BF16), qc * eg, (kc * jnp.exp(g_end - gc_c)).astype(BF16), jnp.exp(g_end))


def _unit_triangular_solve(ms, xs):
    xs = [x - _bdot(m, x) for m, x in zip(ms, xs)]
    pws = ms
    for _ in range(int(math.log2(DN_CHUNK)) - 1):
        pws = [_bdot(p, p) for p in pws]
        xs = [x + _bdot(p, x) for p, x in zip(pws, xs)]
    return xs


def _dn_core_kernel(q_ref, k_ref, v_ref, z_ref, gt_ref, ng_ref, o_ref,
                    a_ref, b_ref, c_ref, ge_ref, s_ref, acc_ref, *, lc, t, hv):
    qh = pl.program_id(1)
    c64 = DN_CHUNK
    nc = t // c64
    ncc = lc // c64
    combos = [(sub, d) for sub in range(2) for d in range(2)]

    cpi = _pick(nc, (4, 3, 2))

    def phase_a(it, carry):
        prepped, where = [], []
        for j in range(cpi):
            c = it * cpi + j
            r0 = pl.multiple_of(c * c64, c64)
            rows = pl.ds(r0, c64)
            qc, kc = q_ref[0, rows, :], k_ref[0, rows, :]
            gram = _bdot_nt(jnp.concatenate([kc, qc], axis=0), kc)
            kk, qk_raw = gram[:c64], gram[c64:]
            for ci, (sub, d) in enumerate(combos):
                head = qh * 2 + sub
                beta_r = gt_ref[0, d * hv + head, pl.ds(c, 1), :]
                gc_r = gt_ref[0, (2 + d) * hv + head, pl.ds(c, 1), :]
                vc = v_ref[0, rows, sub * HEAD_DIM:(sub + 1) * HEAD_DIM]
                prepped.append(_dn_chunk_prep(kk, qk_raw, qc, kc, vc, beta_r, gc_r, d == 1))
                where.append((ci, c, rows, sub, d))
        sols = _unit_triangular_solve([p[0] for p in prepped], [p[1] for p in prepped])
        kdx = [_bdot_tn(p[4], x) for p, x in zip(prepped, sols)]
        qkx = [_bdot(p[2], x) for p, x in zip(prepped, sols)]
        for (ci, c, rows, sub, d), kx, qx, (_, _, _, qd, _, ge) in zip(where, kdx, qkx, prepped):
            b_ref[ci, c] = kx[:, :HEAD_DIM]
            a_ref[ci, c] = kx[:, HEAD_DIM:].astype(BF16)
            c_ref[ci, rows, :] = (qd - qx[:, HEAD_DIM:]).astype(BF16)
            acc_ref[d, rows, sub * HEAD_DIM:(sub + 1) * HEAD_DIM] = qx[:, :HEAD_DIM]
            ge_ref[ci, pl.ds(c, 1), :] = jnp.broadcast_to(ge, (1, HEAD_DIM))
        return carry

    lax.fori_loop(0, nc // cpi, phase_a, 0)

    s_ref[...] = jnp.zeros_like(s_ref)

    def phase_b(n, carry):
        cs = [n if d == 0 else jnp.where(n < ncc, ncc - 1 - n, nc - 1 - (n - ncc)) for _, d in combos]
        ss = [s_ref[ci] for ci in range(4)]
        sb = [s.astype(BF16) for s in ss]
        m1 = [jnp.dot(a_ref[ci, cs[ci]], sb[ci], preferred_element_type=F32) for ci in range(4)]
        m2 = [jnp.dot(c_ref[ci, pl.ds(pl.multiple_of(cs[ci] * c64, c64), c64), :], sb[ci],
                      preferred_element_type=F32) for ci in range(4)]
        for ci, (sub, d) in enumerate(combos):
            s_ref[ci] = ss[ci] * ge_ref[ci, pl.ds(cs[ci], 1), :][:, 0:1] - m1[ci] + b_ref[ci, cs[ci]]
            acc_ref[d, pl.ds(pl.multiple_of(cs[ci] * c64, c64), c64), sub * HEAD_DIM:(sub + 1) * HEAD_DIM] += m2[ci]
        return carry

    lax.fori_loop(0, nc, phase_b, 0)

    for sub in range(2):
        cols = slice(sub * HEAD_DIM, (sub + 1) * HEAD_DIM)
        o = acc_ref[0, :, cols] + acc_ref[1, :, cols]
        o = o * lax.rsqrt(jnp.mean(o * o, axis=-1, keepdims=True) + NORM_EPS) * ng_ref[...]
        o_ref[0, :, cols] = (o * _silu(z_ref[0, :, cols])).astype(o_ref.dtype)


def _dn_core(qkv, pz, gt, norm_g, lc, n_qk_heads, z_col0):
    b, t, _ = qkv.shape
    hv = 2 * n_qk_heads
    vw = 2 * HEAD_DIM
    nc = t // DN_CHUNK
    return pl.pallas_call(
        functools.partial(_dn_core_kernel, lc=lc, t=t, hv=hv),
        grid=(b, n_qk_heads),
        in_specs=[pl.BlockSpec((1, t, HEAD_DIM), lambda bi, h: (bi, 0, h)),
                  pl.BlockSpec((1, t, HEAD_DIM), lambda bi, h: (bi, 0, n_qk_heads + h)),
                  pl.BlockSpec((1, t, vw), lambda bi, h: (bi, 0, n_qk_heads + h)),
                  pl.BlockSpec((1, t, vw), lambda bi, h: (bi, 0, z_col0 // vw + h)),
                  pl.BlockSpec((1, 4 * hv, nc, DN_CHUNK), lambda bi, h: (bi, 0, 0, 0)),
                  pl.BlockSpec((1, HEAD_DIM), lambda bi, h: (0, 0))],
        out_specs=pl.BlockSpec((1, t, vw), lambda bi, h: (bi, 0, h)),
        out_shape=jax.ShapeDtypeStruct((b, t, hv * HEAD_DIM), BF16),
        scratch_shapes=[pltpu.VMEM((4, nc, HEAD_DIM, HEAD_DIM), BF16),
                        pltpu.VMEM((4, nc, HEAD_DIM, HEAD_DIM), F32),
                        pltpu.VMEM((4, t, HEAD_DIM), BF16),
                        pltpu.VMEM((4, -(-nc // 8) * 8, HEAD_DIM), F32),
                        pltpu.VMEM((4, HEAD_DIM, HEAD_DIM), F32),
                        pltpu.VMEM((2, t, vw), F32)],
        compiler_params=_cparams("parallel", "parallel"),
        name="dn_core",
    )(qkv, qkv, qkv, pz, gt, norm_g)


def _hg_kernel(q_ref, i_ref, gate_ref, ff_ref, fb_ref, lb_ref, ng_ref, o_ref,
               lf_ref, k_ref, st_ref, acc_ref, *, lc, t):
    cs, c16 = HG_SUPER, HG_CHUNK
    nsub = cs // c16
    nc = t // cs
    ncc = lc // cs
    ii = lax.broadcasted_iota(jnp.int32, (cs, cs), 0)
    jj = lax.broadcasted_iota(jnp.int32, (cs, cs), 1)
    krow = lax.broadcasted_iota(jnp.int32, (cs, HEAD_DIM), 0)
    incl = [ii >= jj, ii <= jj]
    sums = [jnp.concatenate([m.astype(BF16), (m & (ii // c16 == jj // c16)).astype(BF16),
                             jnp.ones((8, cs), BF16)], axis=0) for m in incl]
    for d in range(2):
        f_ref = ff_ref if d == 0 else fb_ref
        lb = lb_ref[d:d + 1, :]
        fg = lb + (1.0 - lb) * jax.nn.sigmoid(f_ref[0])
        k_ref[d] = 1.0 - fg
        lf_ref[d] = jnp.log(fg)
    st_ref[...] = jnp.zeros_like(st_ref)
    per_it = 2 if nc % 2 == 0 else 1

    def body(it, carry):
        jobs = []
        for u in range(per_it):
            n = it * per_it + u
            jobs.append((0, pl.multiple_of(n * cs, cs)))
            jobs.append((1, pl.multiple_of(jnp.where(n < ncc, ncc - 1 - n, nc - 1 - (n - ncc)) * cs, cs)))
        lfs = [lf_ref[d, pl.ds(r0, cs), :] for d, r0 in jobs]
        kcs = [k_ref[d, pl.ds(r0, cs), :] for d, r0 in jobs]
        qcs = [q_ref[0, pl.ds(r0, cs), :] for d, r0 in jobs]
        vvs = [i_ref[0, pl.ds(r0, cs), :].astype(BF16) for d, r0 in jobs]
        scans = []
        for (d, _), lf in zip(jobs, lfs):
            hi = lf.astype(BF16)
            r1 = lf - hi.astype(F32)
            mid = r1.astype(BF16)
            lo = (r1 - mid.astype(F32)).astype(BF16)
            sc = jnp.dot(sums[d], jnp.concatenate([hi, mid, lo], axis=1), preferred_element_type=F32)
            scans.append(sc[:, :HEAD_DIM] + sc[:, HEAD_DIM:2 * HEAD_DIM] + sc[:, 2 * HEAD_DIM:])
        gcs = [sc[:cs] for sc in scans]
        tots = [sc[2 * cs:2 * cs + 1] for sc in scans]
        qas = [q * jnp.exp(sc[cs:2 * cs]) for q, sc in zip(qcs, scans)]
        qds = [q * jnp.exp(g) for q, g in zip(qcs, gcs)]
        kds = [k * jnp.exp(tt - g) for k, tt, g in zip(kcs, tots, gcs)]
        upd = [_bdot_tn(v, kd) for v, kd in zip(vvs, kds)]
        scaled = []
        for (d, _), gc, kc in zip(jobs, gcs, kcs):
            for a in range(nsub):
                if d == 0:
                    ref = gc[a * c16 - 1:a * c16, :] if a > 0 else None
                    allowed = krow < (a + 1) * c16
                else:
                    ref = gc[(a + 1) * c16:(a + 1) * c16 + 1, :] if a < nsub - 1 else None
                    allowed = krow >= a * c16
                expo = -gc if ref is None else ref - gc
                scaled.append(kc * jnp.exp(jnp.where(allowed, expo, -jnp.inf)))
        parts = [_bdot_nt(qas[j][a * c16:(a + 1) * c16], scaled[j * nsub + a])
                 for j in range(len(jobs)) for a in range(nsub)]
        ps = [jnp.where(incl[d], jnp.concatenate(parts[j * nsub:(j + 1) * nsub], axis=0), 0.0)
              for j, (d, _) in enumerate(jobs)]
        intra = [_bdot(p, v) for p, v in zip(ps, vvs)]
        sts = [st_ref[d] for d in range(2)]
        for j, (d, r0) in enumerate(jobs):
            acc_ref[d, pl.ds(r0, cs), :] = intra[j] + _bdot_nt(qds[j], sts[d])
            sts[d] = sts[d] * jnp.exp(tots[j]) + upd[j]
        for d in range(2):
            st_ref[d] = sts[d]
        return carry

    lax.fori_loop(0, nc // per_it, body, 0)
    o = acc_ref[0] + acc_ref[1]
    o = o * lax.rsqrt(jnp.mean(o * o, axis=-1, keepdims=True) + NORM_EPS) * ng_ref[...]
    o_ref[0] = (o * jax.nn.sigmoid(gate_ref[0])).astype(o_ref.dtype)


def _hgrn2(p, lb, norm_g, lc, n_heads):
    b, t, _ = p.shape
    assert lc % HG_SUPER == 0 and t % HG_SUPER == 0
    blk = lambda k: pl.BlockSpec((1, t, HEAD_DIM), lambda bi, h, k=k: (bi, 0, k * n_heads + h))
    seq = lambda dt: pltpu.VMEM((2, t, HEAD_DIM), dt)
    return pl.pallas_call(
        functools.partial(_hg_kernel, lc=lc, t=t),
        grid=(b, n_heads),
        in_specs=[blk(0), blk(1), blk(2), blk(3), blk(4),
                  pl.BlockSpec((2, HEAD_DIM), lambda bi, h: (0, h)),
                  pl.BlockSpec((1, HEAD_DIM), lambda bi, h: (0, 0))],
        out_specs=pl.BlockSpec((1, t, HEAD_DIM), lambda bi, h: (bi, 0, h)),
        out_shape=jax.ShapeDtypeStruct((b, t, n_heads * HEAD_DIM), BF16),
        scratch_shapes=[seq(F32), seq(F32), pltpu.VMEM((2, HEAD_DIM, HEAD_DIM), F32), seq(F32)],
        compiler_params=_cparams("parallel", "parallel"),
        name="hgrn2",
    )(p, p, p, p, p, lb, norm_g)


def _split_bf16(x):
    hi = x.astype(BF16)
    return hi, (x - hi.astype(F32)).astype(BF16)


def _router_kernel(h_ref, w_ref, b_ref, e_ref, g_ref):
    h_hi, h_lo = _split_bf16(h_ref[...])
    w_hi, w_lo = _split_bf16(w_ref[...])
    nt = (((1,), (1,)), ((), ()))
    logits = (lax.dot_general(w_hi, h_hi, nt, preferred_element_type=F32)
              + lax.dot_general(w_hi, h_lo, nt, preferred_element_type=F32)
              + lax.dot_general(w_lo, h_hi, nt, preferred_element_type=F32))
    scores = jax.nn.sigmoid(logits)
    sel = scores + b_ref[...]
    epg = EXPERTS_PER_GROUP
    rows = lambda a, e: a[e:e + 1, :]
    best_gs = None
    for g in range(N_GROUPS):
        gs = None
        for a in range(epg):
            for c in range(a + 1, epg):
                pair = rows(sel, g * epg + a) + rows(sel, g * epg + c)
                gs = pair if gs is None else jnp.maximum(gs, pair)
        if best_gs is None:
            best_gs, group = gs, jnp.zeros(gs.shape, jnp.int32)
        else:
            upd = gs > best_gs
            best_gs = jnp.where(upd, gs, best_gs)
            group = jnp.where(upd, g, group)

    def in_group(a, l):
        v = rows(a, l)
        for g in range(1, N_GROUPS):
            v = jnp.where(group == g, rows(a, g * epg + l), v)
        return v

    sel_l = [in_group(sel, l) for l in range(epg)]
    sc_l = [in_group(scores, l) for l in range(epg)]

    def top1(exclude):
        best, idx, sc = None, None, None
        for l in range(epg):
            v = sel_l[l] if exclude is None else jnp.where(exclude == l, -jnp.inf, sel_l[l])
            if best is None:
                best, idx, sc = v, jnp.zeros(v.shape, jnp.int32), sc_l[l]
            else:
                upd = v > best
                best = jnp.where(upd, v, best)
                idx = jnp.where(upd, l, idx)
                sc = jnp.where(upd, sc_l[l], sc)
        return idx, sc

    l0, s0 = top1(None)
    l1, s1 = top1(l0)
    e_ref[0:1, :] = group * epg + l0
    e_ref[1:2, :] = group * epg + l1
    den = s0 + s1
    g_ref[0:1, :] = s0 / den
    g_ref[1:2, :] = s1 / den


def _router(h, router_w, router_b):
    m, d = h.shape
    tm = _pick(m, (512, 256, 128))
    ne = router_w.shape[1]
    return pl.pallas_call(
        _router_kernel,
        grid=(m // tm,),
        in_specs=[pl.BlockSpec((tm, d), lambda i: (i, 0)),
                  pl.BlockSpec((ne, d), lambda i: (0, 0)),
                  pl.BlockSpec((ne, 1), lambda i: (0, 0))],
        out_specs=[pl.BlockSpec((TOP_K, tm), lambda i: (0, i)),
                   pl.BlockSpec((TOP_K, tm), lambda i: (0, i))],
        out_shape=[jax.ShapeDtypeStruct((TOP_K, m), jnp.int32), jax.ShapeDtypeStruct((TOP_K, m), F32)],
        compiler_params=_cparams("parallel"),
        name="router",
    )(h, router_w.T, router_b.reshape(ne, 1))


def _moe_ffn_kernel(bexp_ref, code_ref, h_hbm, wg_ref, wu_ref, wd_ref, f_hbm,
                    x0, x1, y0, y1, sem_in, sem_out, *, n_tok):
    i = pl.program_id(0)
    last = pl.num_programs(0) - 1
    eb = EXPERT_BLOCK
    xbufs, ybufs = (x0, x1), (y0, y1)

    def gather(blk, sl, r):
        tok = code_ref[(blk + 1) * eb + r] & (CODE_K_UNIT - 1)
        return pltpu.make_async_copy(h_hbm.at[pl.ds(tok, 1)], xbufs[sl].at[pl.ds(r, 1)], sem_in.at[sl])

    def scatter(blk, sl, r):
        code = code_ref[(blk + 1) * eb + r]
        dst = lax.shift_right_logical(code, CODE_K_SHIFT) * n_tok + (code & (CODE_K_UNIT - 1))
        return pltpu.make_async_copy(ybufs[sl].at[pl.ds(r, 1)], f_hbm.at[pl.ds(dst, 1)], sem_out.at[sl])

    def for_rows(fn):
        def body(r, carry):
            fn(r)
            return carry
        lax.fori_loop(0, eb, body, 0, unroll=8)

    @pl.when(i == 0)
    def _():
        for sl in range(2):
            ybufs[sl][...] = jnp.zeros_like(ybufs[sl])
            spare = pltpu.make_async_copy(ybufs[sl], f_hbm.at[pl.ds(TOP_K * n_tok + sl * eb, eb)],
                                          sem_out.at[sl])
            spare.start()
            spare.wait()
        for_rows(lambda r: gather(0, 0, r).start())

    def step(sl):
        @pl.when(i >= 1)
        def _():
            for_rows(lambda r: scatter(i - 2, sl, r).wait())
        for_rows(lambda r: gather(i, sl, r).wait())
        x = xbufs[sl][...].astype(BF16)
        for r in range(eb):
            gather(i + 1, 1 - sl, r).start()
        a = _silu(jnp.dot(x, wg_ref[0], preferred_element_type=F32)) \
            * jnp.dot(x, wu_ref[0], preferred_element_type=F32)
        for r in range(eb):
            scatter(i - 1, 1 - sl, r).start()
        ybufs[sl][...] = jnp.dot(a.astype(BF16), wd_ref[0], preferred_element_type=F32)

    @pl.when(i % 2 == 0)
    def _():
        step(0)

    @pl.when(i % 2 == 1)
    def _():
        step(1)

    @pl.when(i == last)
    def _():
        for sl in range(2):
            @pl.when(i % 2 == sl)
            def _():
                for_rows(lambda r: scatter(i - 1, 1 - sl, r).wait())
                for_rows(lambda r: gather(i + 1, 1 - sl, r).wait())


def _moe_ffn(h, block_expert, code_buf, wg, wu, wd):
    m, d = h.shape
    de = wg.shape[2]
    nb = block_expert.shape[0]
    wmap = lambda i, be, cb: (be[jnp.minimum(i, nb - 1)], 0, 0)
    buf = pltpu.VMEM((EXPERT_BLOCK, d), F32)
    grid_spec = pltpu.PrefetchScalarGridSpec(
        num_scalar_prefetch=2,
        grid=(nb + 1,),
        in_specs=[pl.BlockSpec(memory_space=pl.ANY),
                  pl.BlockSpec((1, d, de), wmap),
                  pl.BlockSpec((1, d, de), wmap),
                  pl.BlockSpec((1, de, d), wmap)],
        out_specs=pl.BlockSpec(memory_space=pl.ANY),
        scratch_shapes=[buf, buf, buf, buf, pltpu.SemaphoreType.DMA((2,)), pltpu.SemaphoreType.DMA((2,))],
    )
    return pl.pallas_call(
        functools.partial(_moe_ffn_kernel, n_tok=m),
        grid_spec=grid_spec,
        out_shape=jax.ShapeDtypeStruct((TOP_K * m + 2 * EXPERT_BLOCK, d), F32),
        compiler_params=_cparams("arbitrary"),
        name="moe_ffn",
    )(block_expert, code_buf, h, wg, wu, wd)


def _moe_dispatch(expert):
    m = expert.shape[1]
    n_pairs = m * TOP_K
    e_flat = expert.T.reshape(-1)
    onehot = (e_flat[:, None] == jnp.arange(N_EXPERTS)[None, :]).astype(jnp.int32)
    csum = jnp.cumsum(onehot, axis=0)
    rank = jnp.take_along_axis(csum, e_flat[:, None], axis=1)[:, 0] - 1
    counts = csum[-1]
    padded = (counts + EXPERT_BLOCK - 1) // EXPERT_BLOCK * EXPERT_BLOCK
    pend = jnp.cumsum(padded)
    pstart = pend - padded
    dest = pstart[e_flat] + rank
    n_blocks = -(-(n_pairs + N_EXPERTS * (EXPERT_BLOCK - 1)) // EXPERT_BLOCK)
    n_rows = n_blocks * EXPERT_BLOCK
    assert 2 * EXPERT_BLOCK <= m <= CODE_K_UNIT
    pair = jnp.arange(n_pairs, dtype=jnp.int32)
    row = jnp.arange(n_rows + 3 * EXPERT_BLOCK, dtype=jnp.int32)
    pad_code = TOP_K * CODE_K_UNIT + row % (2 * EXPERT_BLOCK)
    code_buf = pad_code.at[dest + EXPERT_BLOCK].set((pair % TOP_K) * CODE_K_UNIT + pair // TOP_K)
    blk_start = jnp.arange(n_blocks) * EXPERT_BLOCK
    block_expert = jnp.minimum(jnp.searchsorted(pend, blk_start, side='right'), N_EXPERTS - 1).astype(jnp.int32)
    return block_expert, code_buf


def _moe(h2, router_w, router_b, wg, wu, wd):
    expert, gate = _router(h2, router_w, router_b)
    block_expert, code_buf = _moe_dispatch(expert)
    f = _moe_ffn(h2, block_expert, code_buf, wg, wu, wd)
    return f, jnp.pad(gate.T, ((0, 0), (0, HEAD_DIM - TOP_K)))


def kernel(x, c, ctx, c_ctx, ada_w, ada_b, ln_g, ln_b, attn_w_in, attn_q_g, attn_k_g, attn_w_out, dn_w_in, dn_conv, dn_a_log, dn_dt_bias, dn_norm_g, dn_w_out, hg_w_in, hg_lb, hg_norm_g, hg_w_out, router_w, router_b, moe_w_gate, moe_w_up, moe_w_down):
    b, l, d = x.shape
    lc = ctx.shape[1]
    t = lc + l
    m = b * t
    depth = ada_w.shape[0]
    alpha = (2.0 * depth) ** 0.25
    tm = _pick(math.gcd(lc, l), (256, 128, 64))
    tiles_per_seq, ctx_tiles = t // tm, lc // tm
    n_heads = d // HEAD_DIM

    mp = -(-(b + 1) // 8) * 8
    cc = jnp.concatenate([c, c_ctx[None, :], jnp.zeros((mp - b - 1, d), F32)], axis=0)
    mod = _ada_mod(cc, ada_w, ada_b)
    mod_l = mod[:, :b].reshape(depth, b, 1, N_MOD, d)
    mod_c = jnp.broadcast_to(mod[:, b].reshape(depth, 1, 1, N_MOD, d), (depth, b, 1, N_MOD, d))
    modtab = jnp.concatenate([mod_c, mod_l], axis=2).reshape(depth, 2 * b, N_MOD, d)

    lb_cum = jnp.cumsum(jax.nn.softmax(hg_lb.astype(F32), axis=1), axis=1)
    lb_cum = lb_cum - lb_cum[:, :1]

    xs = jnp.concatenate([ctx, x], axis=1).reshape(m, d)
    h = _modulate(xs, modtab[0], tm, tiles_per_seq, ctx_tiles, 0, 1, BF16)
    cos, sin = _rope_tables(lc, l)
    rows = (tm, tiles_per_seq, ctx_tiles)

    for i in range(depth):
        kind, j = i % 3, i // 3
        if kind == 0:
            p = _proj(h, attn_w_in[j], 0, attn_w_in.shape[2])
            o = _attention(p.reshape(b, t, -1), attn_q_g[j][None, :], attn_k_g[j][None, :], cos, sin, lc, tm)
            w_out = attn_w_out[j]
        elif kind == 1:
            n_conv = dn_conv.shape[2]
            hv = dn_a_log.shape[2]
            n_main = n_conv + hv * HEAD_DIM
            p = _proj(h, dn_w_in[j], 0, n_main).reshape(b, t, n_main)
            ba = _proj(h, dn_w_in[j], n_main, 4 * hv).reshape(b, t, 4 * hv)
            qkv = _dn_prep(p, dn_conv[j], lc, hv // 2)
            gates = _dn_gates(ba, dn_a_log[j], dn_dt_bias[j])
            gt = jnp.swapaxes(gates, 1, 2).reshape(b, 4 * hv, t // DN_CHUNK, DN_CHUNK)
            o = _dn_core(qkv, p, gt, dn_norm_g[j][None, :], lc, hv // 2, n_conv)
            w_out = dn_w_out[j]
        else:
            p = _proj(h, hg_w_in[j], 0, hg_w_in.shape[2]).reshape(b, t, -1)
            o = _hgrn2(p, lb_cum[:, i], hg_norm_g[j][None, :], lc, n_heads)
            w_out = hg_w_out[j]
        xs, h2 = _out_ln(o.reshape(m, -1), w_out.astype(BF16), xs, modtab[i], modtab[i],
                         ln_g[i, 0][None, :], ln_b[i, 0][None, :], *rows, alpha, 2, 3, 4, F32)
        f, rg = _moe(h2, router_w, router_b, moe_w_gate[i].astype(BF16), moe_w_up[i].astype(BF16),
                     moe_w_down[i].astype(BF16))
        nxt = modtab[min(i + 1, depth - 1)]
        xs, h = _comb_ln(f, rg, xs, modtab[i], nxt, ln_g[i, 1][None, :], ln_b[i, 1][None, :],
                         *rows, alpha, 5, 0, 1, BF16, latent_only=(i == depth - 1))
    return xs.reshape(b, l, d)
```

```python
import functools
import math

import jax
import jax.numpy as jnp
from jax import lax
from jax.experimental import pallas as pl
from jax.experimental.pallas import tpu as pltpu

F32 = jnp.float32
BF16 = jnp.bfloat16

NORM_EPS = 1e-6
HEAD_DIM = 128
GRID_W = 64
ROPE_BASE = 10000.0
ROPE_FREQS = HEAD_DIM // 4
ATTN_GROUP = 4
DN_CONV_W = 5
DN_CHUNK = 64
HG_CHUNK = 16
HG_SUPER = 64
N_EXPERTS = 16
N_GROUPS = 4
EXPERTS_PER_GROUP = N_EXPERTS // N_GROUPS
TOP_K = 2
EXPERT_BLOCK = 256
CODE_K_SHIFT = 16
CODE_K_UNIT = 1 << CODE_K_SHIFT
N_MOD = 6
VMEM_LIMIT = 56 * 1024 * 1024


def _cparams(*sem):
    return pltpu.CompilerParams(dimension_semantics=sem, vmem_limit_bytes=VMEM_LIMIT)


def _pick(n, prefs):
    for p in prefs:
        if n % p == 0:
            return p
    return n


def _bdot(a, b):
    return jnp.dot(a.astype(BF16), b.astype(BF16), preferred_element_type=F32)


def _bdot_nt(a, b):
    return lax.dot_general(a.astype(BF16), b.astype(BF16), (((1,), (1,)), ((), ())),
                           preferred_element_type=F32)


def _bdot_tn(a, b):
    return lax.dot_general(a.astype(BF16), b.astype(BF16), (((0,), (0,)), ((), ())),
                           preferred_element_type=F32)


def _silu(x):
    return x * jax.nn.sigmoid(x)


def _softplus(x):
    return jnp.maximum(x, 0.0) + jnp.log1p(jnp.exp(-jnp.abs(x)))


def _row_to_col(row, n):
    eye = lax.broadcasted_iota(jnp.int32, (n, n), 0) == lax.broadcasted_iota(jnp.int32, (n, n), 1)
    return jnp.sum(jnp.where(eye, row, 0.0), axis=1, keepdims=True)


def _ada_kernel(cc_ref, w_ref, b_ref, o_ref):
    a = _silu(cc_ref[...])
    o_ref[0] = _bdot(a, w_ref[0]) + b_ref[0]


def _ada_mod(cc, ada_w, ada_b):
    depth, d, n = ada_w.shape
    mp = cc.shape[0]
    tn = _pick(n, (1024, 512, 256, 128))
    return pl.pallas_call(
        _ada_kernel,
        grid=(depth, n // tn),
        in_specs=[pl.BlockSpec((mp, d), lambda i, j: (0, 0)),
                  pl.BlockSpec((1, d, tn), lambda i, j: (i, 0, j)),
                  pl.BlockSpec((1, 1, tn), lambda i, j: (i, 0, j))],
        out_specs=pl.BlockSpec((1, mp, tn), lambda i, j: (i, 0, j)),
        out_shape=jax.ShapeDtypeStruct((depth, mp, n), F32),
        compiler_params=_cparams("parallel", "parallel"),
        name="ada_mod",
    )(cc, ada_w, ada_b.reshape(depth, 1, n))


def _modulate_kernel(ctx_ref, lat_ref, mod_ref, xs_ref, h_ref, *, shift, scale, tiles_per_seq, ctx_tiles):
    m = mod_ref[0]
    is_ctx = pl.program_id(0) % tiles_per_seq < ctx_tiles
    x = jnp.where(is_ctx, ctx_ref[0], lat_ref[0])
    xs_ref[...] = x
    h_ref[...] = (x * (1.0 + m[scale:scale + 1]) + m[shift:shift + 1]).astype(h_ref.dtype)


def _mod_index(tiles_per_seq, ctx_tiles):
    def index(i):
        return ((i // tiles_per_seq) * 2 + jnp.where(i % tiles_per_seq < ctx_tiles, 0, 1), 0, 0)
    return index


def _modulate(ctx, x, modtab, tm, tiles_per_seq, ctx_tiles, shift, scale, dtype):
    b, _, d = x.shape
    midx = _mod_index(tiles_per_seq, ctx_tiles)
    m = b * tiles_per_seq * tm
    return pl.pallas_call(
        functools.partial(_modulate_kernel, shift=shift, scale=scale, tiles_per_seq=tiles_per_seq,
                          ctx_tiles=ctx_tiles),
        grid=(m // tm,),
        in_specs=[pl.BlockSpec((1, tm, d), lambda i: (i // tiles_per_seq,
                                                      jnp.minimum(i % tiles_per_seq, ctx_tiles - 1), 0)),
                  pl.BlockSpec((1, tm, d), lambda i: (i // tiles_per_seq,
                                                      jnp.maximum(i % tiles_per_seq - ctx_tiles, 0), 0)),
                  pl.BlockSpec((1, N_MOD, d), midx)],
        out_specs=[pl.BlockSpec((tm, d), lambda i: (i, 0)), pl.BlockSpec((tm, d), lambda i: (i, 0))],
        out_shape=[jax.ShapeDtypeStruct((m, d), F32), jax.ShapeDtypeStruct((m, d), dtype)],
        compiler_params=_cparams("parallel"),
        name="modulate",
    )(ctx, x, modtab)


def _proj_kernel(a_ref, w_ref, o_ref, wb_ref):
    @pl.when(pl.program_id(1) == 0)
    def _():
        wb_ref[...] = w_ref[...].astype(BF16)

    o_ref[...] = jnp.dot(a_ref[...], wb_ref[...], preferred_element_type=F32).astype(o_ref.dtype)


def _proj(a, w, col0, ncols, out_dtype=F32):
    m, k = a.shape
    tn = _pick(ncols, (1024, 512, 256, 128))
    assert col0 % tn == 0
    tm = _pick(m, (512, 256, 128, 64))
    c0 = col0 // tn
    return pl.pallas_call(
        _proj_kernel,
        grid=(ncols // tn, m // tm),
        in_specs=[pl.BlockSpec((tm, k), lambda j, i: (i, 0)),
                  pl.BlockSpec((k, tn), lambda j, i: (0, c0 + j))],
        out_specs=pl.BlockSpec((tm, tn), lambda j, i: (i, j)),
        out_shape=jax.ShapeDtypeStruct((m, ncols), out_dtype),
        scratch_shapes=[pltpu.VMEM((k, tn), BF16)],
        compiler_params=_cparams("parallel", "arbitrary"),
        name="proj",
    )(a, w)


def _ln_epilogue(y, x, mod, modn, lng, lnb, alpha, gate, shift, scale):
    r = alpha * x + mod[gate:gate + 1] * y
    rc = r - jnp.mean(r, axis=-1, keepdims=True)
    var = jnp.mean(rc * rc, axis=-1, keepdims=True)
    xn = rc * lax.rsqrt(var + NORM_EPS) * lng + lnb
    h = xn * (1.0 + modn[scale:scale + 1]) + modn[shift:shift + 1]
    return xn, h


def _out_ln_kernel(o_ref, w_ref, x_ref, mod_ref, modn_ref, lng_ref, lnb_ref, xn_ref, h_ref, acc_ref,
                   *, alpha, gate, shift, scale, nk):
    kk = pl.program_id(1)
    part = jnp.dot(o_ref[...], w_ref[...], preferred_element_type=F32)

    def finish(y):
        xn, h = _ln_epilogue(y, x_ref[...], mod_ref[0], modn_ref[0], lng_ref[...], lnb_ref[...],
                             alpha, gate, shift, scale)
        xn_ref[...] = xn
        h_ref[...] = h.astype(h_ref.dtype)

    if nk == 1:
        finish(part)
    else:
        @pl.when(kk == 0)
        def _():
            acc_ref[...] = part

        @pl.when((kk > 0) & (kk < nk - 1))
        def _():
            acc_ref[...] += part

        @pl.when(kk == nk - 1)
        def _():
            finish(acc_ref[...] + part)


def _out_ln(o, w, x, mod, modn, lng, lnb, tm, tiles_per_seq, ctx_tiles, alpha, gate, shift, scale, h_dtype):
    m, k = o.shape
    d = w.shape[1]
    tk = _pick(k, (2048, 1024, 512, 256, 128))
    midx = _mod_index(tiles_per_seq, ctx_tiles)
    return pl.pallas_call(
        functools.partial(_out_ln_kernel, alpha=alpha, gate=gate, shift=shift, scale=scale, nk=k // tk),
        grid=(m // tm, k // tk),
        in_specs=[pl.BlockSpec((tm, tk), lambda i, j: (i, j)),
                  pl.BlockSpec((tk, d), lambda i, j: (j, 0)),
                  pl.BlockSpec((tm, d), lambda i, j: (i, 0)),
                  pl.BlockSpec((1, N_MOD, d), lambda i, j: midx(i)),
                  pl.BlockSpec((1, N_MOD, d), lambda i, j: midx(i)),
                  pl.BlockSpec((1, d), lambda i, j: (0, 0)),
                  pl.BlockSpec((1, d), lambda i, j: (0, 0))],
        out_specs=[pl.BlockSpec((tm, d), lambda i, j: (i, 0)),
                   pl.BlockSpec((tm, d), lambda i, j: (i, 0))],
        out_shape=[jax.ShapeDtypeStruct((m, d), F32), jax.ShapeDtypeStruct((m, d), h_dtype)],
        scratch_shapes=[pltpu.VMEM((tm, d), F32)],
        compiler_params=_cparams("parallel", "arbitrary"),
        name="out_ln",
    )(o, w, x, mod, modn, lng, lnb)


def _comb_ln_kernel(f0_ref, f1_ref, rg_ref, x_ref, mod_ref, modn_ref, lng_ref, lnb_ref, xn_ref, h_ref,
                    *, alpha, gate, shift, scale):
    y = f0_ref[...] * rg_ref[:, 0:1] + f1_ref[...] * rg_ref[:, 1:2]
    xn, h = _ln_epilogue(y, x_ref[...], mod_ref[0], modn_ref[0], lng_ref[...], lnb_ref[...],
                         alpha, gate, shift, scale)
    xn_ref[...] = xn
    h_ref[...] = h.astype(h_ref.dtype)


def _comb_ln(f, rg, x, mod, modn, lng, lnb, tm, tiles_per_seq, ctx_tiles, alpha, gate, shift, scale, h_dtype,
             latent_only=False):
    m, d = x.shape
    midx = _mod_index(tiles_per_seq, ctx_tiles)
    nt = m // tm
    if latent_only:
        lat_tiles = tiles_per_seq - ctx_tiles
        n_out = nt // tiles_per_seq * lat_tiles
        src = lambda i: (i // lat_tiles) * tiles_per_seq + ctx_tiles + i % lat_tiles
    else:
        n_out = nt
        src = lambda i: i
    return pl.pallas_call(
        functools.partial(_comb_ln_kernel, alpha=alpha, gate=gate, shift=shift, scale=scale),
        grid=(n_out,),
        in_specs=[pl.BlockSpec((tm, d), lambda i: (src(i), 0)),
                  pl.BlockSpec((tm, d), lambda i: (nt + src(i), 0)),
                  pl.BlockSpec((tm, HEAD_DIM), lambda i: (src(i), 0)),
                  pl.BlockSpec((tm, d), lambda i: (src(i), 0)),
                  pl.BlockSpec((1, N_MOD, d), lambda i: midx(src(i))),
                  pl.BlockSpec((1, N_MOD, d), lambda i: midx(src(i))),
                  pl.BlockSpec((1, d), lambda i: (0, 0)),
                  pl.BlockSpec((1, d), lambda i: (0, 0))],
        out_specs=[pl.BlockSpec((tm, d), lambda i: (i, 0)),
                   pl.BlockSpec((tm, d), lambda i: (i, 0))],
        out_shape=[jax.ShapeDtypeStruct((n_out * tm, d), F32), jax.ShapeDtypeStruct((n_out * tm, d), h_dtype)],
        compiler_params=_cparams("parallel"),
        name="comb_ln",
    )(f, f, rg, x, mod, modn, lng, lnb)


def _rope_tables(lc, l):
    pos = jnp.arange(l)
    row = (pos // GRID_W).astype(F32)
    col = (pos % GRID_W).astype(F32)
    inv_freq = ROPE_BASE ** (-jnp.arange(ROPE_FREQS, dtype=F32) / ROPE_FREQS)
    ar = row[:, None] * inv_freq
    ac = col[:, None] * inv_freq
    cos = jnp.concatenate([jnp.cos(ar), jnp.cos(ar), jnp.cos(ac), jnp.cos(ac)], axis=1)
    sin = jnp.concatenate([-jnp.sin(ar), jnp.sin(ar), -jnp.sin(ac), jnp.sin(ac)], axis=1)
    cos = jnp.concatenate([jnp.ones((lc, HEAD_DIM), F32), cos], axis=0)
    sin = jnp.concatenate([jnp.zeros((lc, HEAD_DIM), F32), sin], axis=0)
    return cos, sin


def _norm_rope(x, g, cos, sin):
    y = x * lax.rsqrt(jnp.mean(x * x, axis=-1, keepdims=True) + NORM_EPS) * g
    lane = lax.broadcasted_iota(jnp.int32, y.shape, 1)
    partner = jnp.where(lane % (2 * ROPE_FREQS) < ROPE_FREQS,
                        pltpu.roll(y, HEAD_DIM - ROPE_FREQS, 1), pltpu.roll(y, ROPE_FREQS, 1))
    return y * cos + partner * sin


def _attn_kernel(q_ref, k_ref, v_ref, qg_ref, kg_ref, cos_ref, sin_ref, o_ref, kb_ref, vb_ref,
                 *, lc, tq, t):
    j = pl.program_id(2)

    @pl.when(j == 0)
    def _():
        kb_ref[...] = _norm_rope(k_ref[0], kg_ref[...], cos_ref[...], sin_ref[...]).astype(BF16)
        vb_ref[...] = v_ref[0].astype(BF16)

    r0 = pl.multiple_of(j * tq, tq)
    cos = cos_ref[pl.ds(r0, tq), :]
    sin = sin_ref[pl.ds(r0, tq), :]
    scale = HEAD_DIM ** -0.5

    def attend(n_keys):
        def scores(g):
            qh = _norm_rope(q_ref[0, :, g * HEAD_DIM:(g + 1) * HEAD_DIM], qg_ref[...], cos, sin)
            return _bdot_nt(qh, kb_ref[0:n_keys, :])

        s_next = scores(0)
        for g in range(ATTN_GROUP):
            s = s_next
            if g + 1 < ATTN_GROUP:
                s_next = scores(g + 1)
            e = jnp.exp((s - jnp.max(s, axis=-1, keepdims=True)) * scale)
            den = jnp.sum(e, axis=-1, keepdims=True)
            o = jnp.dot(e.astype(BF16), vb_ref[0:n_keys, :], preferred_element_type=F32) / den
            o_ref[0, :, g * HEAD_DIM:(g + 1) * HEAD_DIM] = o.astype(o_ref.dtype)

    @pl.when(j < lc // tq)
    def _():
        attend(lc)

    @pl.when(j >= lc // tq)
    def _():
        attend(t)


def _attention(p, qg, kg, cos, sin, lc, tq):
    b, t, n = p.shape
    hkv = n // HEAD_DIM // (ATTN_GROUP + 2)
    hq = hkv * ATTN_GROUP
    gw = ATTN_GROUP * HEAD_DIM
    return pl.pallas_call(
        functools.partial(_attn_kernel, lc=lc, tq=tq, t=t),
        grid=(b, hkv, t // tq),
        in_specs=[pl.BlockSpec((1, tq, gw), lambda bi, h, j: (bi, j, h)),
                  pl.BlockSpec((1, t, HEAD_DIM), lambda bi, h, j: (bi, 0, hq + h)),
                  pl.BlockSpec((1, t, HEAD_DIM), lambda bi, h, j: (bi, 0, hq + hkv + h)),
                  pl.BlockSpec((1, HEAD_DIM), lambda bi, h, j: (0, 0)),
                  pl.BlockSpec((1, HEAD_DIM), lambda bi, h, j: (0, 0)),
                  pl.BlockSpec((t, HEAD_DIM), lambda bi, h, j: (0, 0)),
                  pl.BlockSpec((t, HEAD_DIM), lambda bi, h, j: (0, 0))],
        out_specs=pl.BlockSpec((1, tq, gw), lambda bi, h, j: (bi, j, h)),
        out_shape=jax.ShapeDtypeStruct((b, t, hq * HEAD_DIM), BF16),
        scratch_shapes=[pltpu.VMEM((t, HEAD_DIM), BF16), pltpu.VMEM((t, HEAD_DIM), BF16)],
        compiler_params=_cparams("parallel", "parallel", "arbitrary"),
        name="attention",
    )(p, p, p, qg, kg, cos, sin)


def _dn_prep_kernel(p_ref, cw_ref, o_ref, *, lc, t, n_qk_tiles):
    c = pl.program_id(1)
    x = p_ref[0]
    row = lax.broadcasted_iota(jnp.int32, x.shape, 0)
    seg = row >= lc
    acc = jnp.zeros_like(x)
    for jj in range(DN_CONV_W):
        off = jj - DN_CONV_W // 2
        if off == 0:
            sh = x
        else:
            src = row + off
            ok = (src >= 0) & (src < t) & ((src >= lc) == seg)
            sh = jnp.where(ok, pltpu.roll(x, (-off) % t, 0), 0.0)
        acc = acc + sh * cw_ref[jj:jj + 1, :]
    y = _silu(acc)
    nrm = lax.rsqrt(jnp.sum(y * y, axis=-1, keepdims=True) + NORM_EPS)
    mult = jnp.where(c < n_qk_tiles, nrm * (HEAD_DIM ** -0.5), jnp.where(c < 2 * n_qk_tiles, nrm, 1.0))
    o_ref[0] = y * mult


def _dn_prep(p, conv_w, lc, n_qk_tiles):
    b, t, _ = p.shape
    nch = conv_w.shape[1]
    return pl.pallas_call(
        functools.partial(_dn_prep_kernel, lc=lc, t=t, n_qk_tiles=n_qk_tiles),
        grid=(b, nch // HEAD_DIM),
        in_specs=[pl.BlockSpec((1, t, HEAD_DIM), lambda bi, c: (bi, 0, c)),
                  pl.BlockSpec((DN_CONV_W, HEAD_DIM), lambda bi, c: (0, c))],
        out_specs=pl.BlockSpec((1, t, HEAD_DIM), lambda bi, c: (bi, 0, c)),
        out_shape=jax.ShapeDtypeStruct((b, t, nch), F32),
        compiler_params=_cparams("parallel", "parallel"),
        name="dn_prep",
    )(p, conv_w)


def _dn_gates_kernel(ba_ref, alog_ref, dtb_ref, o_ref, *, t, hv):
    x = ba_ref[0]
    lane = lax.broadcasted_iota(jnp.int32, (DN_CHUNK, 4 * hv), 1)
    g = -jnp.exp(alog_ref[...]) * _softplus(x + dtb_ref[...])
    ii = lax.broadcasted_iota(jnp.int32, (DN_CHUNK, DN_CHUNK), 0)
    jj = lax.broadcasted_iota(jnp.int32, (DN_CHUNK, DN_CHUNK), 1)
    lower = (ii >= jj).astype(F32)
    upper = (ii <= jj).astype(F32)
    for c in range(t // DN_CHUNK):
        rows = slice(c * DN_CHUNK, (c + 1) * DN_CHUNK)
        gch = g[rows]
        pre = jnp.dot(lower, gch, precision=lax.Precision.HIGHEST, preferred_element_type=F32)
        suf = jnp.dot(upper, gch, precision=lax.Precision.HIGHEST, preferred_element_type=F32)
        o_ref[0, rows, :] = jnp.where(lane < 2 * hv, jax.nn.sigmoid(x[rows]),
                                      jnp.where(lane < 3 * hv, pre, suf))


def _dn_gates(ba, a_log, dt_bias):
    b, t, n = ba.shape
    hv = n // 4
    zeros = jnp.zeros((1, 2 * hv), F32)
    alog = jnp.concatenate([zeros, a_log.reshape(1, 2 * hv)], axis=1)
    dtb = jnp.concatenate([zeros, dt_bias.reshape(1, 2 * hv)], axis=1)
    return pl.pallas_call(
        functools.partial(_dn_gates_kernel, t=t, hv=hv),
        grid=(b,),
        in_specs=[pl.BlockSpec((1, t, n), lambda bi: (bi, 0, 0)),
                  pl.BlockSpec((1, n), lambda bi: (0, 0)),
                  pl.BlockSpec((1, n), lambda bi: (0, 0))],
        out_specs=pl.BlockSpec((1, t, n), lambda bi: (bi, 0, 0)),
        out_shape=jax.ShapeDtypeStruct((b, t, n), F32),
        compiler_params=_cparams("parallel"),
        name="dn_gates",
    )(ba, alog, dtb)


def _dn_chunk_prep(kk, qk_raw, qc, kc, vc, beta_r, gc_r, reverse):
    c = DN_CHUNK
    ii = lax.broadcasted_iota(jnp.int32, (c, c), 0)
    jj = lax.broadcasted_iota(jnp.int32, (c, c), 1)
    incl = (ii <= jj) if reverse else (ii >= jj)
    strict = (ii < jj) if reverse else (ii > jj)
    beta_c = _row_to_col(beta_r, c)
    gc_c = _row_to_col(gc_r, c)
    g_end = gc_r[:, 0:1] if reverse else gc_r[:, c - 1:c]
    decay = jnp.exp(jnp.where(incl, gc_c - gc_r, -jnp.inf))
    m = jnp.where(strict, beta_c * kk * decay, 0.0)
    eg = jnp.exp(gc_c)
    x = jnp.concatenate([vc * beta_c, kc * (beta_c * eg)], axis=1)
    qk = jnp.where(incl, qk_raw * decay, 0.0)
    return (m, x, qk.astype(BF16), qc * eg, (kc * jnp.exp(g_end - gc_c)).astype(BF16), jnp.exp(g_end))


def _unit_triangular_solve(ms, xs):
    pws = ms
    ns = [-m for m in ms]
    for _ in range(int(math.log2(DN_CHUNK)) - 1):
        pws = [_bdot(p, p) for p in pws]
        ns = [n + p + _bdot(p, n) for p, n in zip(pws, ns)]
    return [x + _bdot(n, x) for n, x in zip(ns, xs)]


def _dn_core_kernel(q_ref, k_ref, v_ref, z_ref, gt_ref, ng_ref, o_ref,
                    a_ref, b_ref, c_ref, ge_ref, s_ref, acc_ref, *, lc, t, hv):
    qh = pl.program_id(1)
    c64 = DN_CHUNK
    nc = t // c64
    ncc = lc // c64
    combos = [(sub, d) for sub in range(2) for d in range(2)]

    cpi = _pick(nc, (4, 3, 2))

    def phase_a(it, carry):
        prepped, where = [], []
        for j in range(cpi):
            c = it * cpi + j
            r0 = pl.multiple_of(c * c64, c64)
            rows = pl.ds(r0, c64)
            qc, kc = q_ref[0, rows, :], k_ref[0, rows, :]
            gram = _bdot_nt(jnp.concatenate([kc, qc], axis=0), kc)
            kk, qk_raw = gram[:c64], gram[c64:]
            for ci, (sub, d) in enumerate(combos):
                head = qh * 2 + sub
                beta_r = gt_ref[0, d * hv + head, pl.ds(c, 1), :]
                gc_r = gt_ref[0, (2 + d) * hv + head, pl.ds(c, 1), :]
                vc = v_ref[0, rows, sub * HEAD_DIM:(sub + 1) * HEAD_DIM]
                prepped.append(_dn_chunk_prep(kk, qk_raw, qc, kc, vc, beta_r, gc_r, d == 1))
                where.append((ci, c, rows, sub, d))
        sols = _unit_triangular_solve([p[0] for p in prepped], [p[1] for p in prepped])
        kdx = [_bdot_tn(p[4], x) for p, x in zip(prepped, sols)]
        qkx = [_bdot(p[2], x) for p, x in zip(prepped, sols)]
        for (ci, c, rows, sub, d), kx, qx, (_, _, _, qd, _, ge) in zip(where, kdx, qkx, prepped):
            b_ref[ci, c] = kx[:, :HEAD_DIM]
            a_ref[ci, c] = kx[:, HEAD_DIM:].astype(BF16)
            c_ref[ci, rows, :] = (qd - qx[:, HEAD_DIM:]).astype(BF16)
            acc_ref[d, rows, sub * HEAD_DIM:(sub + 1) * HEAD_DIM] = qx[:, :HEAD_DIM]
            ge_ref[ci, pl.ds(c, 1), :] = jnp.broadcast_to(ge, (1, HEAD_DIM))
        return carry

    lax.fori_loop(0, nc // cpi, phase_a, 0)

    s_ref[...] = jnp.zeros_like(s_ref)

    def phase_b(n, carry):
        cs = [n if d == 0 else jnp.where(n < ncc, ncc - 1 - n, nc - 1 - (n - ncc)) for _, d in combos]
        ss = [s_ref[ci] for ci in range(4)]
        sb = [s.astype(BF16) for s in ss]
        m1 = [jnp.dot(a_ref[ci, cs[ci]], sb[ci], preferred_element_type=F32) for ci in range(4)]
        m2 = [jnp.dot(c_ref[ci, pl.ds(pl.multiple_of(cs[ci] * c64, c64), c64), :], sb[ci],
                      preferred_element_type=F32) for ci in range(4)]
        for ci, (sub, d) in enumerate(combos):
            s_ref[ci] = ss[ci] * ge_ref[ci, pl.ds(cs[ci], 1), :][:, 0:1] - m1[ci] + b_ref[ci, cs[ci]]
            acc_ref[d, pl.ds(pl.multiple_of(cs[ci] * c64, c64), c64), sub * HEAD_DIM:(sub + 1) * HEAD_DIM] += m2[ci]
        return carry

    lax.fori_loop(0, nc, phase_b, 0)

    for sub in range(2):
        cols = slice(sub * HEAD_DIM, (sub + 1) * HEAD_DIM)
        o = acc_ref[0, :, cols] + acc_ref[1, :, cols]
        o = o * lax.rsqrt(jnp.mean(o * o, axis=-1, keepdims=True) + NORM_EPS) * ng_ref[...]
        o_ref[0, :, cols] = (o * _silu(z_ref[0, :, cols])).astype(o_ref.dtype)


def _dn_core(qkv, pz, gt, norm_g, lc, n_qk_heads, z_col0):
    b, t, _ = qkv.shape
    hv = 2 * n_qk_heads
    vw = 2 * HEAD_DIM
    nc = t // DN_CHUNK
    return pl.pallas_call(
        functools.partial(_dn_core_kernel, lc=lc, t=t, hv=hv),
        grid=(b, n_qk_heads),
        in_specs=[pl.BlockSpec((1, t, HEAD_DIM), lambda bi, h: (bi, 0, h)),
                  pl.BlockSpec((1, t, HEAD_DIM), lambda bi, h: (bi, 0, n_qk_heads + h)),
                  pl.BlockSpec((1, t, vw), lambda bi, h: (bi, 0, n_qk_heads + h)),
                  pl.BlockSpec((1, t, vw), lambda bi, h: (bi, 0, z_col0 // vw + h)),
                  pl.BlockSpec((1, 4 * hv, nc, DN_CHUNK), lambda bi, h: (bi, 0, 0, 0)),
                  pl.BlockSpec((1, HEAD_DIM), lambda bi, h: (0, 0))],
        out_specs=pl.BlockSpec((1, t, vw), lambda bi, h: (bi, 0, h)),
        out_shape=jax.ShapeDtypeStruct((b, t, hv * HEAD_DIM), BF16),
        scratch_shapes=[pltpu.VMEM((4, nc, HEAD_DIM, HEAD_DIM), BF16),
                        pltpu.VMEM((4, nc, HEAD_DIM, HEAD_DIM), F32),
                        pltpu.VMEM((4, t, HEAD_DIM), BF16),
                        pltpu.VMEM((4, -(-nc // 8) * 8, HEAD_DIM), F32),
                        pltpu.VMEM((4, HEAD_DIM, HEAD_DIM), F32),
                        pltpu.VMEM((2, t, vw), F32)],
        compiler_params=_cparams("parallel", "parallel"),
        name="dn_core",
    )(qkv, qkv, qkv, pz, gt, norm_g)


def _hg_kernel(q_ref, i_ref, gate_ref, ff_ref, fb_ref, lb_ref, ng_ref, o_ref,
               lf_ref, k_ref, st_ref, acc_ref, *, lc, t):
    cs, c16 = HG_SUPER, HG_CHUNK
    nsub = cs // c16
    nc = t // cs
    ncc = lc // cs
    ii = lax.broadcasted_iota(jnp.int32, (cs, cs), 0)
    jj = lax.broadcasted_iota(jnp.int32, (cs, cs), 1)
    krow = lax.broadcasted_iota(jnp.int32, (cs, HEAD_DIM), 0)
    incl = [ii >= jj, ii <= jj]
    sums = [jnp.concatenate([m.astype(BF16), (m & (ii // c16 == jj // c16)).astype(BF16),
                             jnp.ones((8, cs), BF16)], axis=0) for m in incl]
    for d in range(2):
        f_ref = ff_ref if d == 0 else fb_ref
        lb = lb_ref[d:d + 1, :]
        fg = lb + (1.0 - lb) * jax.nn.sigmoid(f_ref[0])
        k_ref[d] = 1.0 - fg
        lf_ref[d] = jnp.log(fg)
    st_ref[...] = jnp.zeros_like(st_ref)
    per_it = 2 if nc % 2 == 0 else 1

    def body(it, carry):
        jobs = []
        for u in range(per_it):
            n = it * per_it + u
            jobs.append((0, pl.multiple_of(n * cs, cs)))
            jobs.append((1, pl.multiple_of(jnp.where(n < ncc, ncc - 1 - n, nc - 1 - (n - ncc)) * cs, cs)))
        lfs = [lf_ref[d, pl.ds(r0, cs), :] for d, r0 in jobs]
        kcs = [k_ref[d, pl.ds(r0, cs), :] for d, r0 in jobs]
        qcs = [q_ref[0, pl.ds(r0, cs), :] for d, r0 in jobs]
        vvs = [i_ref[0, pl.ds(r0, cs), :].astype(BF16) for d, r0 in jobs]
        scans = []
        for (d, _), lf in zip(jobs, lfs):
            hi = lf.astype(BF16)
            r1 = lf - hi.astype(F32)
            mid = r1.astype(BF16)
            lo = (r1 - mid.astype(F32)).astype(BF16)
            sc = jnp.dot(sums[d], jnp.concatenate([hi, mid, lo], axis=1), preferred_element_type=F32)
            scans.append(sc[:, :HEAD_DIM] + sc[:, HEAD_DIM:2 * HEAD_DIM] + sc[:, 2 * HEAD_DIM:])
        gcs = [sc[:cs] for sc in scans]
        tots = [sc[2 * cs:2 * cs + 1] for sc in scans]
        qas = [q * jnp.exp(sc[cs:2 * cs]) for q, sc in zip(qcs, scans)]
        qds = [q * jnp.exp(g) for q, g in zip(qcs, gcs)]
        kds = [k * jnp.exp(tt - g) for k, tt, g in zip(kcs, tots, gcs)]
        upd = [_bdot_tn(v, kd) for v, kd in zip(vvs, kds)]
        scaled = []
        for (d, _), gc, kc in zip(jobs, gcs, kcs):
            for a in range(nsub):
                if d == 0:
                    ref = gc[a * c16 - 1:a * c16, :] if a > 0 else None
                    allowed = krow < (a + 1) * c16
                else:
                    ref = gc[(a + 1) * c16:(a + 1) * c16 + 1, :] if a < nsub - 1 else None
                    allowed = krow >= a * c16
                expo = -gc if ref is None else ref - gc
                scaled.append(kc * jnp.exp(jnp.where(allowed, expo, -jnp.inf)))
        parts = [_bdot_nt(qas[j][a * c16:(a + 1) * c16], scaled[j * nsub + a])
                 for j in range(len(jobs)) for a in range(nsub)]
        ps = [jnp.where(incl[d], jnp.concatenate(parts[j * nsub:(j + 1) * nsub], axis=0), 0.0)
              for j, (d, _) in enumerate(jobs)]
        intra = [_bdot(p, v) for p, v in zip(ps, vvs)]
        sts = [st_ref[d] for d in range(2)]
        for j, (d, r0) in enumerate(jobs):
            acc_ref[d, pl.ds(r0, cs), :] = intra[j] + _bdot_nt(qds[j], sts[d])
            sts[d] = sts[d] * jnp.exp(tots[j]) + upd[j]
        for d in range(2):
            st_ref[d] = sts[d]
        return carry

    lax.fori_loop(0, nc // per_it, body, 0)
    o = acc_ref[0] + acc_ref[1]
    o = o * lax.rsqrt(jnp.mean(o * o, axis=-1, keepdims=True) + NORM_EPS) * ng_ref[...]
    o_ref[0] = (o * jax.nn.sigmoid(gate_ref[0])).astype(o_ref.dtype)


def _hgrn2(p, lb, norm_g, lc, n_heads):
    b, t, _ = p.shape
    assert lc % HG_SUPER == 0 and t % HG_SUPER == 0
    blk = lambda k: pl.BlockSpec((1, t, HEAD_DIM), lambda bi, h, k=k: (bi, 0, k * n_heads + h))
    seq = lambda dt: pltpu.VMEM((2, t, HEAD_DIM), dt)
    return pl.pallas_call(
        functools.partial(_hg_kernel, lc=lc, t=t),
        grid=(b, n_heads),
        in_specs=[blk(0), blk(1), blk(2), blk(3), blk(4),
                  pl.BlockSpec((2, HEAD_DIM), lambda bi, h: (0, h)),
                  pl.BlockSpec((1, HEAD_DIM), lambda bi, h: (0, 0))],
        out_specs=pl.BlockSpec((1, t, HEAD_DIM), lambda bi, h: (bi, 0, h)),
        out_shape=jax.ShapeDtypeStruct((b, t, n_heads * HEAD_DIM), BF16),
        scratch_shapes=[seq(F32), seq(F32), pltpu.VMEM((2, HEAD_DIM, HEAD_DIM), F32), seq(F32)],
        compiler_params=_cparams("parallel", "parallel"),
        name="hgrn2",
    )(p, p, p, p, p, lb, norm_g)


def _split_bf16(x):
    hi = x.astype(BF16)
    return hi, (x - hi.astype(F32)).astype(BF16)


def _router_kernel(h_ref, w_ref, b_ref, e_ref, g_ref):
    h_hi, h_lo = _split_bf16(h_ref[...])
    w_hi, w_lo = _split_bf16(w_ref[...])
    nt = (((1,), (1,)), ((), ()))
    logits = (lax.dot_general(w_hi, h_hi, nt, preferred_element_type=F32)
              + lax.dot_general(w_hi, h_lo, nt, preferred_element_type=F32)
              + lax.dot_general(w_lo, h_hi, nt, preferred_element_type=F32))
    scores = jax.nn.sigmoid(logits)
    sel = scores + b_ref[...]
    epg = EXPERTS_PER_GROUP
    rows = lambda a, e: a[e:e + 1, :]
    best_gs = None
    for g in range(N_GROUPS):
        gs = None
        for a in range(epg):
            for c in range(a + 1, epg):
                pair = rows(sel, g * epg + a) + rows(sel, g * epg + c)
                gs = pair if gs is None else jnp.maximum(gs, pair)
        if best_gs is None:
            best_gs, group = gs, jnp.zeros(gs.shape, jnp.int32)
        else:
            upd = gs > best_gs
            best_gs = jnp.where(upd, gs, best_gs)
            group = jnp.where(upd, g, group)

    def in_group(a, l):
        v = rows(a, l)
        for g in range(1, N_GROUPS):
            v = jnp.where(group == g, rows(a, g * epg + l), v)
        return v

    sel_l = [in_group(sel, l) for l in range(epg)]
    sc_l = [in_group(scores, l) for l in range(epg)]

    def top1(exclude):
        best, idx, sc = None, None, None
        for l in range(epg):
            v = sel_l[l] if exclude is None else jnp.where(exclude == l, -jnp.inf, sel_l[l])
            if best is None:
                best, idx, sc = v, jnp.zeros(v.shape, jnp.int32), sc_l[l]
            else:
                upd = v > best
                best = jnp.where(upd, v, best)
                idx = jnp.where(upd, l, idx)
                sc = jnp.where(upd, sc_l[l], sc)
        return idx, sc

    l0, s0 = top1(None)
    l1, s1 = top1(l0)
    e_ref[0:1, :] = group * epg + l0
    e_ref[1:2, :] = group * epg + l1
    den = s0 + s1
    g_ref[0:1, :] = s0 / den
    g_ref[1:2, :] = s1 / den


def _router(h, router_w, router_b):
    m, d = h.shape
    tm = _pick(m, (512, 256, 128))
    ne = router_w.shape[1]
    return pl.pallas_call(
        _router_kernel,
        grid=(m // tm,),
        in_specs=[pl.BlockSpec((tm, d), lambda i: (i, 0)),
                  pl.BlockSpec((ne, d), lambda i: (0, 0)),
                  pl.BlockSpec((ne, 1), lambda i: (0, 0))],
        out_specs=[pl.BlockSpec((TOP_K, tm), lambda i: (0, i)),
                   pl.BlockSpec((TOP_K, tm), lambda i: (0, i))],
        out_shape=[jax.ShapeDtypeStruct((TOP_K, m), jnp.int32), jax.ShapeDtypeStruct((TOP_K, m), F32)],
        compiler_params=_cparams("parallel"),
        name="router",
    )(h, router_w.T, router_b.reshape(ne, 1))


def _moe_ffn_kernel(bexp_ref, code_ref, h_hbm, wg_ref, wu_ref, wd_ref, f_hbm,
                    x0, x1, y0, y1, sem_in, sem_out, *, n_tok):
    i = pl.program_id(0)
    last = pl.num_programs(0) - 1
    eb = EXPERT_BLOCK
    xbufs, ybufs = (x0, x1), (y0, y1)

    def gather(blk, sl, r):
        tok = code_ref[(blk + 1) * eb + r] & (CODE_K_UNIT - 1)
        return pltpu.make_async_copy(h_hbm.at[pl.ds(tok, 1)], xbufs[sl].at[pl.ds(r, 1)], sem_in.at[sl])

    def scatter(blk, sl, r):
        code = code_ref[(blk + 1) * eb + r]
        dst = lax.shift_right_logical(code, CODE_K_SHIFT) * n_tok + (code & (CODE_K_UNIT - 1))
        return pltpu.make_async_copy(ybufs[sl].at[pl.ds(r, 1)], f_hbm.at[pl.ds(dst, 1)], sem_out.at[sl])

    def for_rows(fn):
        def body(r, carry):
            fn(r)
            return carry
        lax.fori_loop(0, eb, body, 0, unroll=8)

    @pl.when(i == 0)
    def _():
        for sl in range(2):
            ybufs[sl][...] = jnp.zeros_like(ybufs[sl])
            spare = pltpu.make_async_copy(ybufs[sl], f_hbm.at[pl.ds(TOP_K * n_tok + sl * eb, eb)],
                                          sem_out.at[sl])
            spare.start()
            spare.wait()
        for_rows(lambda r: gather(0, 0, r).start())

    def step(sl):
        @pl.when(i >= 1)
        def _():
            for_rows(lambda r: scatter(i - 2, sl, r).wait())
        for_rows(lambda r: gather(i, sl, r).wait())
        x = xbufs[sl][...].astype(BF16)
        for r in range(eb):
            gather(i + 1, 1 - sl, r).start()
        a = _silu(jnp.dot(x, wg_ref[0, 0], preferred_element_type=F32)) \
            * jnp.dot(x, wu_ref[0, 0], preferred_element_type=F32)
        for r in range(eb):
            scatter(i - 1, 1 - sl, r).start()
        ybufs[sl][...] = jnp.dot(a.astype(BF16), wd_ref[0, 0], preferred_element_type=F32)

    @pl.when(i % 2 == 0)
    def _():
        step(0)

    @pl.when(i % 2 == 1)
    def _():
        step(1)

    @pl.when(i == last)
    def _():
        for sl in range(2):
            @pl.when(i % 2 == sl)
            def _():
                for_rows(lambda r: scatter(i - 1, 1 - sl, r).wait())
                for_rows(lambda r: gather(i + 1, 1 - sl, r).wait())


def _moe_ffn(h, block_expert, code_buf, wg, wu, wd, layer):
    m, d = h.shape
    de = wg.shape[3]
    nb = block_expert.shape[0]
    wmap = lambda i, be, cb: (layer, be[jnp.minimum(i, nb - 1)], 0, 0)
    buf = pltpu.VMEM((EXPERT_BLOCK, d), F32)
    grid_spec = pltpu.PrefetchScalarGridSpec(
        num_scalar_prefetch=2,
        grid=(nb + 1,),
        in_specs=[pl.BlockSpec(memory_space=pl.ANY),
                  pl.BlockSpec((1, 1, d, de), wmap),
                  pl.BlockSpec((1, 1, d, de), wmap),
                  pl.BlockSpec((1, 1, de, d), wmap)],
        out_specs=pl.BlockSpec(memory_space=pl.ANY),
        scratch_shapes=[buf, buf, buf, buf, pltpu.SemaphoreType.DMA((2,)), pltpu.SemaphoreType.DMA((2,))],
    )
    return pl.pallas_call(
        functools.partial(_moe_ffn_kernel, n_tok=m),
        grid_spec=grid_spec,
        out_shape=jax.ShapeDtypeStruct((TOP_K * m + 2 * EXPERT_BLOCK, d), F32),
        compiler_params=_cparams("arbitrary"),
        name="moe_ffn",
    )(block_expert, code_buf, h, wg, wu, wd)


def _moe_dispatch(expert):
    m = expert.shape[1]
    n_pairs = m * TOP_K
    e_flat = expert.T.reshape(-1)
    onehot = (e_flat[:, None] == jnp.arange(N_EXPERTS)[None, :]).astype(jnp.int32)
    csum = jnp.cumsum(onehot, axis=0)
    rank = jnp.take_along_axis(csum, e_flat[:, None], axis=1)[:, 0] - 1
    counts = csum[-1]
    padded = (counts + EXPERT_BLOCK - 1) // EXPERT_BLOCK * EXPERT_BLOCK
    pend = jnp.cumsum(padded)
    pstart = pend - padded
    dest = pstart[e_flat] + rank
    n_blocks = -(-(n_pairs + N_EXPERTS * (EXPERT_BLOCK - 1)) // EXPERT_BLOCK)
    n_rows = n_blocks * EXPERT_BLOCK
    assert 2 * EXPERT_BLOCK <= m <= CODE_K_UNIT
    pair = jnp.arange(n_pairs, dtype=jnp.int32)
    row = jnp.arange(n_rows + 3 * EXPERT_BLOCK, dtype=jnp.int32)
    pad_code = TOP_K * CODE_K_UNIT + row % (2 * EXPERT_BLOCK)
    code_buf = pad_code.at[dest + EXPERT_BLOCK].set((pair % TOP_K) * CODE_K_UNIT + pair // TOP_K)
    blk_start = jnp.arange(n_blocks) * EXPERT_BLOCK
    block_expert = jnp.minimum(jnp.searchsorted(pend, blk_start, side='right'), N_EXPERTS - 1).astype(jnp.int32)
    return block_expert, code_buf


def _moe(h2, router_w, router_b, wg, wu, wd, layer):
    expert, gate = _router(h2, router_w, router_b)
    block_expert, code_buf = _moe_dispatch(expert)
    f = _moe_ffn(h2, block_expert, code_buf, wg, wu, wd, layer)
    return f, jnp.pad(gate.T, ((0, 0), (0, HEAD_DIM - TOP_K)))


def kernel(x, c, ctx, c_ctx, ada_w, ada_b, ln_g, ln_b, attn_w_in, attn_q_g, attn_k_g, attn_w_out, dn_w_in, dn_conv, dn_a_log, dn_dt_bias, dn_norm_g, dn_w_out, hg_w_in, hg_lb, hg_norm_g, hg_w_out, router_w, router_b, moe_w_gate, moe_w_up, moe_w_down):
    b, l, d = x.shape
    lc = ctx.shape[1]
    t = lc + l
    m = b * t
    depth = ada_w.shape[0]
    alpha = (2.0 * depth) ** 0.25
    tm = _pick(math.gcd(lc, l), (256, 128, 64))
    tiles_per_seq, ctx_tiles = t // tm, lc // tm
    n_heads = d // HEAD_DIM

    mp = -(-(b + 1) // 8) * 8
    cc = jnp.concatenate([c, c_ctx[None, :], jnp.zeros((mp - b - 1, d), F32)], axis=0)
    mod = _ada_mod(cc, ada_w, ada_b)
    mod_l = mod[:, :b].reshape(depth, b, 1, N_MOD, d)
    mod_c = jnp.broadcast_to(mod[:, b].reshape(depth, 1, 1, N_MOD, d), (depth, b, 1, N_MOD, d))
    modtab = jnp.concatenate([mod_c, mod_l], axis=2).reshape(depth, 2 * b, N_MOD, d)

    lb_cum = jnp.cumsum(jax.nn.softmax(hg_lb.astype(F32), axis=1), axis=1)
    lb_cum = lb_cum - lb_cum[:, :1]

    xs, h = _modulate(ctx, x, modtab[0], tm, tiles_per_seq, ctx_tiles, 0, 1, BF16)
    cos, sin = _rope_tables(lc, l)
    rows = (tm, tiles_per_seq, ctx_tiles)

    moe_wg, moe_wu, moe_wd = moe_w_gate.astype(BF16), moe_w_up.astype(BF16), moe_w_down.astype(BF16)

    for i in range(depth):
        kind, j = i % 3, i // 3
        if kind == 0:
            p = _proj(h, attn_w_in[j], 0, attn_w_in.shape[2])
            o = _attention(p.reshape(b, t, -1), attn_q_g[j][None, :], attn_k_g[j][None, :], cos, sin, lc, tm)
            w_out = attn_w_out[j]
        elif kind == 1:
            n_conv = dn_conv.shape[2]
            hv = dn_a_log.shape[2]
            n_main = n_conv + hv * HEAD_DIM
            p = _proj(h, dn_w_in[j], 0, n_main).reshape(b, t, n_main)
            ba = _proj(h, dn_w_in[j], n_main, 4 * hv).reshape(b, t, 4 * hv)
            qkv = _dn_prep(p, dn_conv[j], lc, hv // 2)
            gates = _dn_gates(ba, dn_a_log[j], dn_dt_bias[j])
            gt = jnp.swapaxes(gates, 1, 2).reshape(b, 4 * hv, t // DN_CHUNK, DN_CHUNK)
            o = _dn_core(qkv, p, gt, dn_norm_g[j][None, :], lc, hv // 2, n_conv)
            w_out = dn_w_out[j]
        else:
            p = _proj(h, hg_w_in[j], 0, hg_w_in.shape[2]).reshape(b, t, -1)
            o = _hgrn2(p, lb_cum[:, i], hg_norm_g[j][None, :], lc, n_heads)
            w_out = hg_w_out[j]
        xs, h2 = _out_ln(o.reshape(m, -1), w_out.astype(BF16), xs, modtab[i], modtab[i],
                         ln_g[i, 0][None, :], ln_b[i, 0][None, :], *rows, alpha, 2, 3, 4, F32)
        f, rg = _moe(h2, router_w, router_b, moe_wg, moe_wu, moe_wd, i)
        nxt = modtab[min(i + 1, depth - 1)]
        xs, h = _comb_ln(f, rg, xs, modtab[i], nxt, ln_g[i, 1][None, :], ln_b[i, 1][None, :],
                         *rows, alpha, 5, 0, 1, BF16, latent_only=(i == depth - 1))
    return xs.reshape(b, l, d)
```

```python
import functools
import math

import jax
import jax.numpy as jnp
from jax import lax
from jax.experimental import pallas as pl
from jax.experimental.pallas import tpu as pltpu

F32 = jnp.float32
BF16 = jnp.bfloat16

NORM_EPS = 1e-6
HEAD_DIM = 128
GRID_W = 64
ROPE_BASE = 10000.0
ROPE_FREQS = HEAD_DIM // 4
ATTN_GROUP = 4
DN_CONV_W = 5
DN_CHUNK = 64
HG_CHUNK = 16
HG_SUPER = 64
N_EXPERTS = 16
N_GROUPS = 4
EXPERTS_PER_GROUP = N_EXPERTS // N_GROUPS
TOP_K = 2
EXPERT_BLOCK = 256
CODE_K_SHIFT = 16
CODE_K_UNIT = 1 << CODE_K_SHIFT
N_MOD = 6
VMEM_LIMIT = 56 * 1024 * 1024


def _cparams(*sem):
    return pltpu.CompilerParams(dimension_semantics=sem, vmem_limit_bytes=VMEM_LIMIT)


def _pick(n, prefs):
    for p in prefs:
        if n % p == 0:
            return p
    return n


def _bdot(a, b):
    return jnp.dot(a.astype(BF16), b.astype(BF16), preferred_element_type=F32)


def _bdot_nt(a, b):
    return lax.dot_general(a.astype(BF16), b.astype(BF16), (((1,), (1,)), ((), ())),
                           preferred_element_type=F32)


def _bdot_tn(a, b):
    return lax.dot_general(a.astype(BF16), b.astype(BF16), (((0,), (0,)), ((), ())),
                           preferred_element_type=F32)


def _silu(x):
    return x * jax.nn.sigmoid(x)


def _softplus(x):
    return jnp.maximum(x, 0.0) + jnp.log1p(jnp.exp(-jnp.abs(x)))


def _row_to_col(row, n):
    eye = lax.broadcasted_iota(jnp.int32, (n, n), 0) == lax.broadcasted_iota(jnp.int32, (n, n), 1)
    return jnp.sum(jnp.where(eye, row, 0.0), axis=1, keepdims=True)


def _ada_kernel(cc_ref, w_ref, b_ref, o_ref):
    a = _silu(cc_ref[...])
    o_ref[0] = _bdot(a, w_ref[0]) + b_ref[0]


def _ada_mod(cc, ada_w, ada_b):
    depth, d, n = ada_w.shape
    mp = cc.shape[0]
    tn = _pick(n, (1024, 512, 256, 128))
    return pl.pallas_call(
        _ada_kernel,
        grid=(depth, n // tn),
        in_specs=[pl.BlockSpec((mp, d), lambda i, j: (0, 0)),
                  pl.BlockSpec((1, d, tn), lambda i, j: (i, 0, j)),
                  pl.BlockSpec((1, 1, tn), lambda i, j: (i, 0, j))],
        out_specs=pl.BlockSpec((1, mp, tn), lambda i, j: (i, 0, j)),
        out_shape=jax.ShapeDtypeStruct((depth, mp, n), F32),
        compiler_params=_cparams("parallel", "parallel"),
        name="ada_mod",
    )(cc, ada_w, ada_b.reshape(depth, 1, n))


def _modulate_kernel(ctx_ref, lat_ref, mod_ref, xs_ref, h_ref, *, shift, scale, tiles_per_seq, ctx_tiles):
    m = mod_ref[0]
    is_ctx = pl.program_id(0) % tiles_per_seq < ctx_tiles
    x = jnp.where(is_ctx, ctx_ref[0], lat_ref[0])
    xs_ref[...] = x
    h_ref[...] = (x * (1.0 + m[scale:scale + 1]) + m[shift:shift + 1]).astype(h_ref.dtype)


def _mod_index(tiles_per_seq, ctx_tiles):
    def index(i):
        return ((i // tiles_per_seq) * 2 + jnp.where(i % tiles_per_seq < ctx_tiles, 0, 1), 0, 0)
    return index


def _modulate(ctx, x, modtab, tm, tiles_per_seq, ctx_tiles, shift, scale, dtype):
    b, _, d = x.shape
    midx = _mod_index(tiles_per_seq, ctx_tiles)
    m = b * tiles_per_seq * tm
    return pl.pallas_call(
        functools.partial(_modulate_kernel, shift=shift, scale=scale, tiles_per_seq=tiles_per_seq,
                          ctx_tiles=ctx_tiles),
        grid=(m // tm,),
        in_specs=[pl.BlockSpec((1, tm, d), lambda i: (i // tiles_per_seq,
                                                      jnp.minimum(i % tiles_per_seq, ctx_tiles - 1), 0)),
                  pl.BlockSpec((1, tm, d), lambda i: (i // tiles_per_seq,
                                                      jnp.maximum(i % tiles_per_seq - ctx_tiles, 0), 0)),
                  pl.BlockSpec((1, N_MOD, d), midx)],
        out_specs=[pl.BlockSpec((tm, d), lambda i: (i, 0)), pl.BlockSpec((tm, d), lambda i: (i, 0))],
        out_shape=[jax.ShapeDtypeStruct((m, d), F32), jax.ShapeDtypeStruct((m, d), dtype)],
        compiler_params=_cparams("parallel"),
        name="modulate",
    )(ctx, x, modtab)


def _proj_kernel(a_ref, w_ref, o_ref, wb_ref):
    @pl.when(pl.program_id(1) == 0)
    def _():
        wb_ref[...] = w_ref[...].astype(BF16)

    o_ref[...] = jnp.dot(a_ref[...], wb_ref[...], preferred_element_type=F32).astype(o_ref.dtype)


def _proj(a, w, col0, ncols, out_dtype=F32):
    m, k = a.shape
    tn = _pick(ncols, (1024, 512, 256, 128))
    assert col0 % tn == 0
    tm = _pick(m, (512, 256, 128, 64))
    c0 = col0 // tn
    return pl.pallas_call(
        _proj_kernel,
        grid=(ncols // tn, m // tm),
        in_specs=[pl.BlockSpec((tm, k), lambda j, i: (i, 0)),
                  pl.BlockSpec((k, tn), lambda j, i: (0, c0 + j))],
        out_specs=pl.BlockSpec((tm, tn), lambda j, i: (i, j)),
        out_shape=jax.ShapeDtypeStruct((m, ncols), out_dtype),
        scratch_shapes=[pltpu.VMEM((k, tn), BF16)],
        compiler_params=_cparams("parallel", "arbitrary"),
        name="proj",
    )(a, w)


def _ln_epilogue(y, x, mod, modn, lng, lnb, alpha, gate, shift, scale):
    r = alpha * x + mod[gate:gate + 1] * y
    rc = r - jnp.mean(r, axis=-1, keepdims=True)
    var = jnp.mean(rc * rc, axis=-1, keepdims=True)
    xn = rc * lax.rsqrt(var + NORM_EPS) * lng + lnb
    h = xn * (1.0 + modn[scale:scale + 1]) + modn[shift:shift + 1]
    return xn, h


def _out_ln_kernel(o_ref, w_ref, x_ref, mod_ref, modn_ref, lng_ref, lnb_ref, xn_ref, h_ref, acc_ref,
                   *, alpha, gate, shift, scale, nk):
    kk = pl.program_id(1)
    part = jnp.dot(o_ref[...], w_ref[...], preferred_element_type=F32)

    def finish(y):
        xn, h = _ln_epilogue(y, x_ref[...], mod_ref[0], modn_ref[0], lng_ref[...], lnb_ref[...],
                             alpha, gate, shift, scale)
        xn_ref[...] = xn
        h_ref[...] = h.astype(h_ref.dtype)

    if nk == 1:
        finish(part)
    else:
        @pl.when(kk == 0)
        def _():
            acc_ref[...] = part

        @pl.when((kk > 0) & (kk < nk - 1))
        def _():
            acc_ref[...] += part

        @pl.when(kk == nk - 1)
        def _():
            finish(acc_ref[...] + part)


def _out_ln(o, w, x, mod, modn, lng, lnb, tm, tiles_per_seq, ctx_tiles, alpha, gate, shift, scale, h_dtype):
    m, k = o.shape
    d = w.shape[1]
    tk = _pick(k, (4096, 2048, 1024, 512, 256, 128))
    midx = _mod_index(tiles_per_seq, ctx_tiles)
    return pl.pallas_call(
        functools.partial(_out_ln_kernel, alpha=alpha, gate=gate, shift=shift, scale=scale, nk=k // tk),
        grid=(m // tm, k // tk),
        in_specs=[pl.BlockSpec((tm, tk), lambda i, j: (i, j)),
                  pl.BlockSpec((tk, d), lambda i, j: (j, 0)),
                  pl.BlockSpec((tm, d), lambda i, j: (i, 0)),
                  pl.BlockSpec((1, N_MOD, d), lambda i, j: midx(i)),
                  pl.BlockSpec((1, N_MOD, d), lambda i, j: midx(i)),
                  pl.BlockSpec((1, d), lambda i, j: (0, 0)),
                  pl.BlockSpec((1, d), lambda i, j: (0, 0))],
        out_specs=[pl.BlockSpec((tm, d), lambda i, j: (i, 0)),
                   pl.BlockSpec((tm, d), lambda i, j: (i, 0))],
        out_shape=[jax.ShapeDtypeStruct((m, d), F32), jax.ShapeDtypeStruct((m, d), h_dtype)],
        scratch_shapes=[pltpu.VMEM((tm, d), F32)],
        compiler_params=_cparams("parallel", "arbitrary"),
        name="out_ln",
    )(o, w, x, mod, modn, lng, lnb)


def _comb_ln_kernel(f0_ref, f1_ref, rg_ref, x_ref, mod_ref, modn_ref, lng_ref, lnb_ref, xn_ref, h_ref,
                    *, alpha, gate, shift, scale):
    y = f0_ref[...] * rg_ref[:, 0:1] + f1_ref[...] * rg_ref[:, 1:2]
    xn, h = _ln_epilogue(y, x_ref[...], mod_ref[0], modn_ref[0], lng_ref[...], lnb_ref[...],
                         alpha, gate, shift, scale)
    xn_ref[...] = xn
    h_ref[...] = h.astype(h_ref.dtype)


def _comb_ln(f, rg, x, mod, modn, lng, lnb, tm, tiles_per_seq, ctx_tiles, alpha, gate, shift, scale, h_dtype,
             latent_only=False):
    m, d = x.shape
    midx = _mod_index(tiles_per_seq, ctx_tiles)
    nt = m // tm
    if latent_only:
        lat_tiles = tiles_per_seq - ctx_tiles
        n_out = nt // tiles_per_seq * lat_tiles
        src = lambda i: (i // lat_tiles) * tiles_per_seq + ctx_tiles + i % lat_tiles
    else:
        n_out = nt
        src = lambda i: i
    return pl.pallas_call(
        functools.partial(_comb_ln_kernel, alpha=alpha, gate=gate, shift=shift, scale=scale),
        grid=(n_out,),
        in_specs=[pl.BlockSpec((tm, d), lambda i: (src(i), 0)),
                  pl.BlockSpec((tm, d), lambda i: (nt + src(i), 0)),
                  pl.BlockSpec((tm, HEAD_DIM), lambda i: (src(i), 0)),
                  pl.BlockSpec((tm, d), lambda i: (src(i), 0)),
                  pl.BlockSpec((1, N_MOD, d), lambda i: midx(src(i))),
                  pl.BlockSpec((1, N_MOD, d), lambda i: midx(src(i))),
                  pl.BlockSpec((1, d), lambda i: (0, 0)),
                  pl.BlockSpec((1, d), lambda i: (0, 0))],
        out_specs=[pl.BlockSpec((tm, d), lambda i: (i, 0)),
                   pl.BlockSpec((tm, d), lambda i: (i, 0))],
        out_shape=[jax.ShapeDtypeStruct((n_out * tm, d), F32), jax.ShapeDtypeStruct((n_out * tm, d), h_dtype)],
        compiler_params=_cparams("parallel"),
        name="comb_ln",
    )(f, f, rg, x, mod, modn, lng, lnb)


def _rope_tables(lc, l):
    pos = jnp.arange(l)
    row = (pos // GRID_W).astype(F32)
    col = (pos % GRID_W).astype(F32)
    inv_freq = ROPE_BASE ** (-jnp.arange(ROPE_FREQS, dtype=F32) / ROPE_FREQS)
    ar = row[:, None] * inv_freq
    ac = col[:, None] * inv_freq
    cos = jnp.concatenate([jnp.cos(ar), jnp.cos(ar), jnp.cos(ac), jnp.cos(ac)], axis=1)
    sin = jnp.concatenate([-jnp.sin(ar), jnp.sin(ar), -jnp.sin(ac), jnp.sin(ac)], axis=1)
    cos = jnp.concatenate([jnp.ones((lc, HEAD_DIM), F32), cos], axis=0)
    sin = jnp.concatenate([jnp.zeros((lc, HEAD_DIM), F32), sin], axis=0)
    return cos, sin


def _norm_rope(x, g, cos, sin):
    y = x * lax.rsqrt(jnp.mean(x * x, axis=-1, keepdims=True) + NORM_EPS) * g
    lane = lax.broadcasted_iota(jnp.int32, y.shape, 1)
    partner = jnp.where(lane % (2 * ROPE_FREQS) < ROPE_FREQS,
                        pltpu.roll(y, HEAD_DIM - ROPE_FREQS, 1), pltpu.roll(y, ROPE_FREQS, 1))
    return y * cos + partner * sin


def _attn_kernel(q_ref, k_ref, v_ref, qg_ref, kg_ref, cos_ref, sin_ref, o_ref, kb_ref, vb_ref,
                 *, lc, tq, t):
    j = pl.program_id(2)

    @pl.when(j == 0)
    def _():
        kb_ref[...] = _norm_rope(k_ref[0], kg_ref[...], cos_ref[...], sin_ref[...]).astype(BF16)
        vb_ref[...] = v_ref[0].astype(BF16)

    r0 = pl.multiple_of(j * tq, tq)
    cos = cos_ref[pl.ds(r0, tq), :]
    sin = sin_ref[pl.ds(r0, tq), :]
    scale = HEAD_DIM ** -0.5

    def attend(n_keys):
        def scores(g):
            qh = _norm_rope(q_ref[0, :, g * HEAD_DIM:(g + 1) * HEAD_DIM], qg_ref[...], cos, sin)
            return _bdot_nt(qh, kb_ref[0:n_keys, :])

        s_next = scores(0)
        for g in range(ATTN_GROUP):
            s = s_next
            if g + 1 < ATTN_GROUP:
                s_next = scores(g + 1)
            e = jnp.exp((s - jnp.max(s, axis=-1, keepdims=True)) * scale)
            den = jnp.sum(e, axis=-1, keepdims=True)
            o = jnp.dot(e.astype(BF16), vb_ref[0:n_keys, :], preferred_element_type=F32) / den
            o_ref[0, :, g * HEAD_DIM:(g + 1) * HEAD_DIM] = o.astype(o_ref.dtype)

    @pl.when(j < lc // tq)
    def _():
        attend(lc)

    @pl.when(j >= lc // tq)
    def _():
        attend(t)


def _attention(p, qg, kg, cos, sin, lc, tq):
    b, t, n = p.shape
    hkv = n // HEAD_DIM // (ATTN_GROUP + 2)
    hq = hkv * ATTN_GROUP
    gw = ATTN_GROUP * HEAD_DIM
    return pl.pallas_call(
        functools.partial(_attn_kernel, lc=lc, tq=tq, t=t),
        grid=(b, hkv, t // tq),
        in_specs=[pl.BlockSpec((1, tq, gw), lambda bi, h, j: (bi, j, h)),
                  pl.BlockSpec((1, t, HEAD_DIM), lambda bi, h, j: (bi, 0, hq + h)),
                  pl.BlockSpec((1, t, HEAD_DIM), lambda bi, h, j: (bi, 0, hq + hkv + h)),
                  pl.BlockSpec((1, HEAD_DIM), lambda bi, h, j: (0, 0)),
                  pl.BlockSpec((1, HEAD_DIM), lambda bi, h, j: (0, 0)),
                  pl.BlockSpec((t, HEAD_DIM), lambda bi, h, j: (0, 0)),
                  pl.BlockSpec((t, HEAD_DIM), lambda bi, h, j: (0, 0))],
        out_specs=pl.BlockSpec((1, tq, gw), lambda bi, h, j: (bi, j, h)),
        out_shape=jax.ShapeDtypeStruct((b, t, hq * HEAD_DIM), BF16),
        scratch_shapes=[pltpu.VMEM((t, HEAD_DIM), BF16), pltpu.VMEM((t, HEAD_DIM), BF16)],
        compiler_params=_cparams("parallel", "parallel", "arbitrary"),
        name="attention",
    )(p, p, p, qg, kg, cos, sin)


def _dn_prep_kernel(p_ref, cw_ref, o_ref, *, lc, t, n_qk_tiles):
    c = pl.program_id(1)
    tt = math.gcd(lc, t - lc)
    tt = _pick(tt, (256, 128, 64, 32, 16, 8))
    halo = 8
    zeros = jnp.zeros((halo, HEAD_DIM), F32)
    post = jnp.where(c < n_qk_tiles, HEAD_DIM ** -0.5, 1.0)
    is_qk = c < 2 * n_qk_tiles
    for r0 in range(0, t, tt):
        top = zeros if r0 in (0, lc) else p_ref[0, r0 - halo:r0, :]
        bot = zeros if r0 + tt in (lc, t) else p_ref[0, r0 + tt:r0 + tt + halo, :]
        win = jnp.concatenate([top, p_ref[0, r0:r0 + tt, :], bot], axis=0)
        acc = win[halo:halo + tt] * cw_ref[DN_CONV_W // 2:DN_CONV_W // 2 + 1, :]
        for jj in range(DN_CONV_W):
            off = jj - DN_CONV_W // 2
            if off != 0:
                acc = acc + pltpu.roll(win, (-off) % (tt + 2 * halo), 0)[halo:halo + tt] * cw_ref[jj:jj + 1, :]
        y = _silu(acc)
        nrm = lax.rsqrt(jnp.sum(y * y, axis=-1, keepdims=True) + NORM_EPS)
        o_ref[0, r0:r0 + tt, :] = y * jnp.where(is_qk, nrm * post, 1.0)


def _dn_prep(p, conv_w, lc, n_qk_tiles):
    b, t, _ = p.shape
    nch = conv_w.shape[1]
    return pl.pallas_call(
        functools.partial(_dn_prep_kernel, lc=lc, t=t, n_qk_tiles=n_qk_tiles),
        grid=(b, nch // HEAD_DIM),
        in_specs=[pl.BlockSpec((1, t, HEAD_DIM), lambda bi, c: (bi, 0, c)),
                  pl.BlockSpec((DN_CONV_W, HEAD_DIM), lambda bi, c: (0, c))],
        out_specs=pl.BlockSpec((1, t, HEAD_DIM), lambda bi, c: (bi, 0, c)),
        out_shape=jax.ShapeDtypeStruct((b, t, nch), F32),
        compiler_params=_cparams("parallel", "parallel"),
        name="dn_prep",
    )(p, conv_w)


def _dn_gates_kernel(ba_ref, alog_ref, dtb_ref, o_ref, *, t, hv):
    x = ba_ref[0]
    lane = lax.broadcasted_iota(jnp.int32, (DN_CHUNK, 4 * hv), 1)
    g = -jnp.exp(alog_ref[...]) * _softplus(x + dtb_ref[...])
    ii = lax.broadcasted_iota(jnp.int32, (DN_CHUNK, DN_CHUNK), 0)
    jj = lax.broadcasted_iota(jnp.int32, (DN_CHUNK, DN_CHUNK), 1)
    lower = (ii >= jj).astype(F32)
    upper = (ii <= jj).astype(F32)
    for c in range(t // DN_CHUNK):
        rows = slice(c * DN_CHUNK, (c + 1) * DN_CHUNK)
        gch = g[rows]
        pre = jnp.dot(lower, gch, precision=lax.Precision.HIGHEST, preferred_element_type=F32)
        suf = jnp.dot(upper, gch, precision=lax.Precision.HIGHEST, preferred_element_type=F32)
        o_ref[0, rows, :] = jnp.where(lane < 2 * hv, jax.nn.sigmoid(x[rows]),
                                      jnp.where(lane < 3 * hv, pre, suf))


def _dn_gates(ba, a_log, dt_bias):
    b, t, n = ba.shape
    hv = n // 4
    zeros = jnp.zeros((1, 2 * hv), F32)
    alog = jnp.concatenate([zeros, a_log.reshape(1, 2 * hv)], axis=1)
    dtb = jnp.concatenate([zeros, dt_bias.reshape(1, 2 * hv)], axis=1)
    return pl.pallas_call(
        functools.partial(_dn_gates_kernel, t=t, hv=hv),
        grid=(b,),
        in_specs=[pl.BlockSpec((1, t, n), lambda bi: (bi, 0, 0)),
                  pl.BlockSpec((1, n), lambda bi: (0, 0)),
                  pl.BlockSpec((1, n), lambda bi: (0, 0))],
        out_specs=pl.BlockSpec((1, t, n), lambda bi: (bi, 0, 0)),
        out_shape=jax.ShapeDtypeStruct((b, t, n), F32),
        compiler_params=_cparams("parallel"),
        name="dn_gates",
    )(ba, alog, dtb)


def _dn_chunk_prep(kk, qk_raw, qc, kc, vc, beta_r, gc_r, reverse):
    c = DN_CHUNK
    ii = lax.broadcasted_iota(jnp.int32, (c, c), 0)
    jj = lax.broadcasted_iota(jnp.int32, (c, c), 1)
    incl = (ii <= jj) if reverse else (ii >= jj)
    strict = (ii < jj) if reverse else (ii > jj)
    beta_c = _row_to_col(beta_r, c)
    gc_c = _row_to_col(gc_r, c)
    g_end = gc_r[:, 0:1] if reverse else gc_r[:, c - 1:c]
    decay = jnp.exp(jnp.where(incl, gc_c - gc_r, -jnp.inf))
    m = jnp.where(strict, beta_c * kk * decay, 0.0)
    eg = jnp.exp(gc_c)
    x = jnp.concatenate([vc * beta_c, kc * (beta_c * eg)], axis=1)
    qk = jnp.where(incl, qk_raw * decay, 0.0)
    return (m, x, qk.astype(BF16), qc * eg, (kc * jnp.exp(g_end - gc_c)).astype(BF16), jnp.exp(g_end))


def _unit_triangular_solve(ms, xs):
    pws = ms
    ns = [-m for m in ms]
    for _ in range(int(math.log2(DN_CHUNK)) - 1):
        pws = [_bdot(p, p) for p in pws]
        ns = [n + p + _bdot(p, n) for p, n in zip(pws, ns)]
    return [x + _bdot(n, x) for n, x in zip(ns, xs)]


def _dn_core_kernel(q_ref, k_ref, v_ref, z_ref, gt_ref, ng_ref, o_ref,
                    a_ref, b_ref, c_ref, ge_ref, s_ref, acc_ref, *, lc, t, hv):
    qh = pl.program_id(1)
    c64 = DN_CHUNK
    nc = t // c64
    ncc = lc // c64
    combos = [(sub, d) for sub in range(2) for d in range(2)]

    cpi = _pick(nc, (6, 4, 3, 2))

    def phase_a(it, carry):
        prepped, where = [], []
        for j in range(cpi):
            c = it * cpi + j
            r0 = pl.multiple_of(c * c64, c64)
            rows = pl.ds(r0, c64)
            qc, kc = q_ref[0, rows, :], k_ref[0, rows, :]
            gram = _bdot_nt(jnp.concatenate([kc, qc], axis=0), kc)
            kk, qk_raw = gram[:c64], gram[c64:]
            for ci, (sub, d) in enumerate(combos):
                head = qh * 2 + sub
                beta_r = gt_ref[0, d * hv + head, pl.ds(c, 1), :]
                gc_r = gt_ref[0, (2 + d) * hv + head, pl.ds(c, 1), :]
                vc = v_ref[0, rows, sub * HEAD_DIM:(sub + 1) * HEAD_DIM]
                prepped.append(_dn_chunk_prep(kk, qk_raw, qc, kc, vc, beta_r, gc_r, d == 1))
                where.append((ci, c, rows, sub, d))
        sols = _unit_triangular_solve([p[0] for p in prepped], [p[1] for p in prepped])
        kdx = [_bdot_tn(p[4], x) for p, x in zip(prepped, sols)]
        qkx = [_bdot(p[2], x) for p, x in zip(prepped, sols)]
        for (ci, c, rows, sub, d), kx, qx, (_, _, _, qd, _, ge) in zip(where, kdx, qkx, prepped):
            b_ref[ci, c] = kx[:, :HEAD_DIM]
            a_ref[ci, c] = kx[:, HEAD_DIM:].astype(BF16)
            c_ref[ci, rows, :] = (qd - qx[:, HEAD_DIM:]).astype(BF16)
            acc_ref[d, rows, sub * HEAD_DIM:(sub + 1) * HEAD_DIM] = qx[:, :HEAD_DIM]
            ge_ref[ci, pl.ds(c, 1), :] = jnp.broadcast_to(ge, (1, HEAD_DIM))
        return carry

    lax.fori_loop(0, nc // cpi, phase_a, 0)

    s_ref[...] = jnp.zeros_like(s_ref)

    def phase_b(n, carry):
        cs = [n if d == 0 else jnp.where(n < ncc, ncc - 1 - n, nc - 1 - (n - ncc)) for _, d in combos]
        ss = [s_ref[ci] for ci in range(4)]
        sb = [s.astype(BF16) for s in ss]
        m1 = [jnp.dot(a_ref[ci, cs[ci]], sb[ci], preferred_element_type=F32) for ci in range(4)]
        m2 = [jnp.dot(c_ref[ci, pl.ds(pl.multiple_of(cs[ci] * c64, c64), c64), :], sb[ci],
                      preferred_element_type=F32) for ci in range(4)]
        for ci, (sub, d) in enumerate(combos):
            s_ref[ci] = ss[ci] * ge_ref[ci, pl.ds(cs[ci], 1), :][:, 0:1] - m1[ci] + b_ref[ci, cs[ci]]
            acc_ref[d, pl.ds(pl.multiple_of(cs[ci] * c64, c64), c64), sub * HEAD_DIM:(sub + 1) * HEAD_DIM] += m2[ci]
        return carry

    lax.fori_loop(0, nc, phase_b, 0)

    for sub in range(2):
        cols = slice(sub * HEAD_DIM, (sub + 1) * HEAD_DIM)
        o = acc_ref[0, :, cols] + acc_ref[1, :, cols]
        o = o * lax.rsqrt(jnp.mean(o * o, axis=-1, keepdims=True) + NORM_EPS) * ng_ref[...]
        o_ref[0, :, cols] = (o * _silu(z_ref[0, :, cols])).astype(o_ref.dtype)


def _dn_core(qkv, pz, gt, norm_g, lc, n_qk_heads, z_col0):
    b, t, _ = qkv.shape
    hv = 2 * n_qk_heads
    vw = 2 * HEAD_DIM
    nc = t // DN_CHUNK
    return pl.pallas_call(
        functools.partial(_dn_core_kernel, lc=lc, t=t, hv=hv),
        grid=(b, n_qk_heads),
        in_specs=[pl.BlockSpec((1, t, HEAD_DIM), lambda bi, h: (bi, 0, h)),
                  pl.BlockSpec((1, t, HEAD_DIM), lambda bi, h: (bi, 0, n_qk_heads + h)),
                  pl.BlockSpec((1, t, vw), lambda bi, h: (bi, 0, n_qk_heads + h)),
                  pl.BlockSpec((1, t, vw), lambda bi, h: (bi, 0, z_col0 // vw + h)),
                  pl.BlockSpec((1, 4 * hv, nc, DN_CHUNK), lambda bi, h: (bi, 0, 0, 0)),
                  pl.BlockSpec((1, HEAD_DIM), lambda bi, h: (0, 0))],
        out_specs=pl.BlockSpec((1, t, vw), lambda bi, h: (bi, 0, h)),
        out_shape=jax.ShapeDtypeStruct((b, t, hv * HEAD_DIM), BF16),
        scratch_shapes=[pltpu.VMEM((4, nc, HEAD_DIM, HEAD_DIM), BF16),
                        pltpu.VMEM((4, nc, HEAD_DIM, HEAD_DIM), F32),
                        pltpu.VMEM((4, t, HEAD_DIM), BF16),
                        pltpu.VMEM((4, -(-nc // 8) * 8, HEAD_DIM), F32),
                        pltpu.VMEM((4, HEAD_DIM, HEAD_DIM), F32),
                        pltpu.VMEM((2, t, vw), F32)],
        compiler_params=_cparams("parallel", "parallel"),
        name="dn_core",
    )(qkv, qkv, qkv, pz, gt, norm_g)


def _hg_kernel(q_ref, i_ref, gate_ref, ff_ref, fb_ref, lb_ref, ng_ref, o_ref,
               lf_ref, k_ref, st_ref, acc_ref, *, lc, t):
    cs, c16 = HG_SUPER, HG_CHUNK
    nsub = cs // c16
    nc = t // cs
    ncc = lc // cs
    ii = lax.broadcasted_iota(jnp.int32, (cs, cs), 0)
    jj = lax.broadcasted_iota(jnp.int32, (cs, cs), 1)
    krow = lax.broadcasted_iota(jnp.int32, (cs, HEAD_DIM), 0)
    incl = [ii >= jj, ii <= jj]
    sums = [jnp.concatenate([m.astype(BF16), (m & (ii // c16 == jj // c16)).astype(BF16),
                             jnp.ones((8, cs), BF16)], axis=0) for m in incl]
    for d in range(2):
        f_ref = ff_ref if d == 0 else fb_ref
        lb = lb_ref[d:d + 1, :]
        fg = lb + (1.0 - lb) * jax.nn.sigmoid(f_ref[0])
        k_ref[d] = 1.0 - fg
        lf_ref[d] = jnp.log(fg)
    st_ref[...] = jnp.zeros_like(st_ref)
    per_it = _pick(nc, (4, 3, 2))

    def body(it, carry):
        jobs = []
        for u in range(per_it):
            n = it * per_it + u
            jobs.append((0, pl.multiple_of(n * cs, cs)))
            jobs.append((1, pl.multiple_of(jnp.where(n < ncc, ncc - 1 - n, nc - 1 - (n - ncc)) * cs, cs)))
        lfs = [lf_ref[d, pl.ds(r0, cs), :] for d, r0 in jobs]
        kcs = [k_ref[d, pl.ds(r0, cs), :] for d, r0 in jobs]
        qcs = [q_ref[0, pl.ds(r0, cs), :] for d, r0 in jobs]
        vvs = [i_ref[0, pl.ds(r0, cs), :].astype(BF16) for d, r0 in jobs]
        scans = []
        for (d, _), lf in zip(jobs, lfs):
            hi = lf.astype(BF16)
            r1 = lf - hi.astype(F32)
            mid = r1.astype(BF16)
            lo = (r1 - mid.astype(F32)).astype(BF16)
            sc = jnp.dot(sums[d], jnp.concatenate([hi, mid, lo], axis=1), preferred_element_type=F32)
            scans.append(sc[:, :HEAD_DIM] + sc[:, HEAD_DIM:2 * HEAD_DIM] + sc[:, 2 * HEAD_DIM:])
        gcs = [sc[:cs] for sc in scans]
        tots = [sc[2 * cs:2 * cs + 1] for sc in scans]
        qas = [q * jnp.exp(sc[cs:2 * cs]) for q, sc in zip(qcs, scans)]
        qds = [q * jnp.exp(g) for q, g in zip(qcs, gcs)]
        kds = [k * jnp.exp(tt - g) for k, tt, g in zip(kcs, tots, gcs)]
        upd = [_bdot_tn(v, kd) for v, kd in zip(vvs, kds)]
        scaled = []
        for (d, _), gc, kc in zip(jobs, gcs, kcs):
            for a in range(nsub):
                if d == 0:
                    ref = gc[a * c16 - 1:a * c16, :] if a > 0 else None
                    allowed = krow < (a + 1) * c16
                else:
                    ref = gc[(a + 1) * c16:(a + 1) * c16 + 1, :] if a < nsub - 1 else None
                    allowed = krow >= a * c16
                expo = -gc if ref is None else ref - gc
                scaled.append(kc * jnp.exp(jnp.where(allowed, expo, -jnp.inf)))
        parts = [_bdot_nt(qas[j][a * c16:(a + 1) * c16], scaled[j * nsub + a])
                 for j in range(len(jobs)) for a in range(nsub)]
        ps = [jnp.where(incl[d], jnp.concatenate(parts[j * nsub:(j + 1) * nsub], axis=0), 0.0)
              for j, (d, _) in enumerate(jobs)]
        intra = [_bdot(p, v) for p, v in zip(ps, vvs)]
        sts = [st_ref[d] for d in range(2)]
        for j, (d, r0) in enumerate(jobs):
            acc_ref[d, pl.ds(r0, cs), :] = intra[j] + _bdot_nt(qds[j], sts[d])
            sts[d] = sts[d] * jnp.exp(tots[j]) + upd[j]
        for d in range(2):
            st_ref[d] = sts[d]
        return carry

    lax.fori_loop(0, nc // per_it, body, 0)
    o = acc_ref[0] + acc_ref[1]
    o = o * lax.rsqrt(jnp.mean(o * o, axis=-1, keepdims=True) + NORM_EPS) * ng_ref[...]
    o_ref[0] = (o * jax.nn.sigmoid(gate_ref[0])).astype(o_ref.dtype)


def _hgrn2(p, lb, norm_g, lc, n_heads):
    b, t, _ = p.shape
    assert lc % HG_SUPER == 0 and t % HG_SUPER == 0
    blk = lambda k: pl.BlockSpec((1, t, HEAD_DIM), lambda bi, h, k=k: (bi, 0, k * n_heads + h))
    seq = lambda dt: pltpu.VMEM((2, t, HEAD_DIM), dt)
    return pl.pallas_call(
        functools.partial(_hg_kernel, lc=lc, t=t),
        grid=(b, n_heads),
        in_specs=[blk(0), blk(1), blk(2), blk(3), blk(4),
                  pl.BlockSpec((2, HEAD_DIM), lambda bi, h: (0, h)),
                  pl.BlockSpec((1, HEAD_DIM), lambda bi, h: (0, 0))],
        out_specs=pl.BlockSpec((1, t, HEAD_DIM), lambda bi, h: (bi, 0, h)),
        out_shape=jax.ShapeDtypeStruct((b, t, n_heads * HEAD_DIM), BF16),
        scratch_shapes=[seq(F32), seq(F32), pltpu.VMEM((2, HEAD_DIM, HEAD_DIM), F32), seq(F32)],
        compiler_params=_cparams("parallel", "parallel"),
        name="hgrn2",
    )(p, p, p, p, p, lb, norm_g)


def _split_bf16(x):
    hi = x.astype(BF16)
    return hi, (x - hi.astype(F32)).astype(BF16)


def _router_kernel(h_ref, w_ref, b_ref, e_ref, g_ref):
    h_hi, h_lo = _split_bf16(h_ref[...])
    w_hi, w_lo = _split_bf16(w_ref[...])
    nt = (((1,), (1,)), ((), ()))
    logits = (lax.dot_general(w_hi, h_hi, nt, preferred_element_type=F32)
              + lax.dot_general(w_hi, h_lo, nt, preferred_element_type=F32)
              + lax.dot_general(w_lo, h_hi, nt, preferred_element_type=F32))
    scores = jax.nn.sigmoid(logits)
    sel = scores + b_ref[...]
    epg = EXPERTS_PER_GROUP
    rows = lambda a, e: a[e:e + 1, :]
    best_gs = None
    for g in range(N_GROUPS):
        gs = None
        for a in range(epg):
            for c in range(a + 1, epg):
                pair = rows(sel, g * epg + a) + rows(sel, g * epg + c)
                gs = pair if gs is None else jnp.maximum(gs, pair)
        if best_gs is None:
            best_gs, group = gs, jnp.zeros(gs.shape, jnp.int32)
        else:
            upd = gs > best_gs
            best_gs = jnp.where(upd, gs, best_gs)
            group = jnp.where(upd, g, group)

    def in_group(a, l):
        v = rows(a, l)
        for g in range(1, N_GROUPS):
            v = jnp.where(group == g, rows(a, g * epg + l), v)
        return v

    sel_l = [in_group(sel, l) for l in range(epg)]
    sc_l = [in_group(scores, l) for l in range(epg)]

    def top1(exclude):
        best, idx, sc = None, None, None
        for l in range(epg):
            v = sel_l[l] if exclude is None else jnp.where(exclude == l, -jnp.inf, sel_l[l])
            if best is None:
                best, idx, sc = v, jnp.zeros(v.shape, jnp.int32), sc_l[l]
            else:
                upd = v > best
                best = jnp.where(upd, v, best)
                idx = jnp.where(upd, l, idx)
                sc = jnp.where(upd, sc_l[l], sc)
        return idx, sc

    l0, s0 = top1(None)
    l1, s1 = top1(l0)
    e_ref[0:1, :] = group * epg + l0
    e_ref[1:2, :] = group * epg + l1
    den = s0 + s1
    g_ref[0:1, :] = s0 / den
    g_ref[1:2, :] = s1 / den


def _router(h, router_w, router_b):
    m, d = h.shape
    tm = _pick(m, (512, 256, 128))
    ne = router_w.shape[1]
    return pl.pallas_call(
        _router_kernel,
        grid=(m // tm,),
        in_specs=[pl.BlockSpec((tm, d), lambda i: (i, 0)),
                  pl.BlockSpec((ne, d), lambda i: (0, 0)),
                  pl.BlockSpec((ne, 1), lambda i: (0, 0))],
        out_specs=[pl.BlockSpec((TOP_K, tm), lambda i: (0, i)),
                   pl.BlockSpec((TOP_K, tm), lambda i: (0, i))],
        out_shape=[jax.ShapeDtypeStruct((TOP_K, m), jnp.int32), jax.ShapeDtypeStruct((TOP_K, m), F32)],
        compiler_params=_cparams("parallel"),
        name="router",
    )(h, router_w.T, router_b.reshape(ne, 1))


def _moe_ffn_kernel(bexp_ref, code_ref, h_hbm, wg_ref, wu_ref, wd_ref, f_hbm,
                    x0, x1, y0, y1, sem_in, sem_out, *, n_tok):
    i = pl.program_id(0)
    last = pl.num_programs(0) - 1
    eb = EXPERT_BLOCK
    xbufs, ybufs = (x0, x1), (y0, y1)

    def gather(blk, sl, r):
        tok = code_ref[(blk + 1) * eb + r] & (CODE_K_UNIT - 1)
        return pltpu.make_async_copy(h_hbm.at[pl.ds(tok, 1)], xbufs[sl].at[pl.ds(r, 1)], sem_in.at[sl])

    def scatter(blk, sl, r):
        code = code_ref[(blk + 1) * eb + r]
        dst = lax.shift_right_logical(code, CODE_K_SHIFT) * n_tok + (code & (CODE_K_UNIT - 1))
        return pltpu.make_async_copy(ybufs[sl].at[pl.ds(r, 1)], f_hbm.at[pl.ds(dst, 1)], sem_out.at[sl])

    def for_rows(fn):
        def body(r, carry):
            fn(r)
            return carry
        lax.fori_loop(0, eb, body, 0, unroll=8)

    @pl.when(i == 0)
    def _():
        for sl in range(2):
            ybufs[sl][...] = jnp.zeros_like(ybufs[sl])
            spare = pltpu.make_async_copy(ybufs[sl], f_hbm.at[pl.ds(TOP_K * n_tok + sl * eb, eb)],
                                          sem_out.at[sl])
            spare.start()
            spare.wait()
        for_rows(lambda r: gather(0, 0, r).start())

    def step(sl):
        @pl.when(i >= 1)
        def _():
            for_rows(lambda r: scatter(i - 2, sl, r).wait())
        for_rows(lambda r: gather(i, sl, r).wait())
        x = xbufs[sl][...].astype(BF16)
        for r in range(eb):
            gather(i + 1, 1 - sl, r).start()
        a = _silu(jnp.dot(x, wg_ref[0, 0], preferred_element_type=F32)) \
            * jnp.dot(x, wu_ref[0, 0], preferred_element_type=F32)
        for r in range(eb):
            scatter(i - 1, 1 - sl, r).start()
        ybufs[sl][...] = jnp.dot(a.astype(BF16), wd_ref[0, 0], preferred_element_type=F32)

    @pl.when(i % 2 == 0)
    def _():
        step(0)

    @pl.when(i % 2 == 1)
    def _():
        step(1)

    @pl.when(i == last)
    def _():
        for sl in range(2):
            @pl.when(i % 2 == sl)
            def _():
                for_rows(lambda r: scatter(i - 1, 1 - sl, r).wait())
                for_rows(lambda r: gather(i + 1, 1 - sl, r).wait())


def _moe_ffn(h, block_expert, code_buf, wg, wu, wd, layer):
    m, d = h.shape
    de = wg.shape[3]
    nb = block_expert.shape[0]
    wmap = lambda i, be, cb: (layer, be[jnp.minimum(i, nb - 1)], 0, 0)
    buf = pltpu.VMEM((EXPERT_BLOCK, d), F32)
    grid_spec = pltpu.PrefetchScalarGridSpec(
        num_scalar_prefetch=2,
        grid=(nb + 1,),
        in_specs=[pl.BlockSpec(memory_space=pl.ANY),
                  pl.BlockSpec((1, 1, d, de), wmap),
                  pl.BlockSpec((1, 1, d, de), wmap),
                  pl.BlockSpec((1, 1, de, d), wmap)],
        out_specs=pl.BlockSpec(memory_space=pl.ANY),
        scratch_shapes=[buf, buf, buf, buf, pltpu.SemaphoreType.DMA((2,)), pltpu.SemaphoreType.DMA((2,))],
    )
    return pl.pallas_call(
        functools.partial(_moe_ffn_kernel, n_tok=m),
        grid_spec=grid_spec,
        out_shape=jax.ShapeDtypeStruct((TOP_K * m + 2 * EXPERT_BLOCK, d), F32),
        compiler_params=_cparams("arbitrary"),
        name="moe_ffn",
    )(block_expert, code_buf, h, wg, wu, wd)


def _moe_dispatch(expert):
    m = expert.shape[1]
    n_pairs = m * TOP_K
    e_flat = expert.T.reshape(-1)
    onehot = (e_flat[:, None] == jnp.arange(N_EXPERTS)[None, :]).astype(jnp.int32)
    csum = jnp.cumsum(onehot, axis=0)
    rank = jnp.take_along_axis(csum, e_flat[:, None], axis=1)[:, 0] - 1
    counts = csum[-1]
    padded = (counts + EXPERT_BLOCK - 1) // EXPERT_BLOCK * EXPERT_BLOCK
    pend = jnp.cumsum(padded)
    pstart = pend - padded
    dest = pstart[e_flat] + rank
    n_blocks = -(-(n_pairs + N_EXPERTS * (EXPERT_BLOCK - 1)) // EXPERT_BLOCK)
    n_rows = n_blocks * EXPERT_BLOCK
    assert 2 * EXPERT_BLOCK <= m <= CODE_K_UNIT
    pair = jnp.arange(n_pairs, dtype=jnp.int32)
    row = jnp.arange(n_rows + 3 * EXPERT_BLOCK, dtype=jnp.int32)
    pad_code = TOP_K * CODE_K_UNIT + row % (2 * EXPERT_BLOCK)
    code_buf = pad_code.at[dest + EXPERT_BLOCK].set((pair % TOP_K) * CODE_K_UNIT + pair // TOP_K)
    blk_start = jnp.arange(n_blocks) * EXPERT_BLOCK
    block_expert = jnp.minimum(jnp.searchsorted(pend, blk_start, side='right'), N_EXPERTS - 1).astype(jnp.int32)
    return block_expert, code_buf


def _moe(h2, router_w, router_b, wg, wu, wd, layer):
    expert, gate = _router(h2, router_w, router_b)
    block_expert, code_buf = _moe_dispatch(expert)
    f = _moe_ffn(h2, block_expert, code_buf, wg, wu, wd, layer)
    return f, jnp.pad(gate.T, ((0, 0), (0, HEAD_DIM - TOP_K)))


def kernel(x, c, ctx, c_ctx, ada_w, ada_b, ln_g, ln_b, attn_w_in, attn_q_g, attn_k_g, attn_w_out, dn_w_in, dn_conv, dn_a_log, dn_dt_bias, dn_norm_g, dn_w_out, hg_w_in, hg_lb, hg_norm_g, hg_w_out, router_w, router_b, moe_w_gate, moe_w_up, moe_w_down):
    b, l, d = x.shape
    lc = ctx.shape[1]
    t = lc + l
    m = b * t
    depth = ada_w.shape[0]
    alpha = (2.0 * depth) ** 0.25
    tm = _pick(math.gcd(lc, l), (256, 128, 64))
    tiles_per_seq, ctx_tiles = t // tm, lc // tm
    n_heads = d // HEAD_DIM

    mp = -(-(b + 1) // 8) * 8
    cc = jnp.concatenate([c, c_ctx[None, :], jnp.zeros((mp - b - 1, d), F32)], axis=0)
    mod = _ada_mod(cc, ada_w, ada_b)
    mod_l = mod[:, :b].reshape(depth, b, 1, N_MOD, d)
    mod_c = jnp.broadcast_to(mod[:, b].reshape(depth, 1, 1, N_MOD, d), (depth, b, 1, N_MOD, d))
    modtab = jnp.concatenate([mod_c, mod_l], axis=2).reshape(depth, 2 * b, N_MOD, d)

    lb_cum = jnp.cumsum(jax.nn.softmax(hg_lb.astype(F32), axis=1), axis=1)
    lb_cum = lb_cum - lb_cum[:, :1]

    xs, h = _modulate(ctx, x, modtab[0], tm, tiles_per_seq, ctx_tiles, 0, 1, BF16)
    cos, sin = _rope_tables(lc, l)
    rows = (tm, tiles_per_seq, ctx_tiles)

    moe_wg, moe_wu, moe_wd = moe_w_gate.astype(BF16), moe_w_up.astype(BF16), moe_w_down.astype(BF16)

    for i in range(depth):
        kind, j = i % 3, i // 3
        if kind == 0:
            p = _proj(h, attn_w_in[j], 0, attn_w_in.shape[2])
            o = _attention(p.reshape(b, t, -1), attn_q_g[j][None, :], attn_k_g[j][None, :], cos, sin, lc, tm)
            w_out = attn_w_out[j]
        elif kind == 1:
            n_conv = dn_conv.shape[2]
            hv = dn_a_log.shape[2]
            n_main = n_conv + hv * HEAD_DIM
            p = _proj(h, dn_w_in[j], 0, n_main).reshape(b, t, n_main)
            ba = _proj(h, dn_w_in[j], n_main, 4 * hv).reshape(b, t, 4 * hv)
            qkv = _dn_prep(p, dn_conv[j], lc, hv // 2)
            gates = _dn_gates(ba, dn_a_log[j], dn_dt_bias[j])
            gt = jnp.swapaxes(gates, 1, 2).reshape(b, 4 * hv, t // DN_CHUNK, DN_CHUNK)
            o = _dn_core(qkv, p, gt, dn_norm_g[j][None, :], lc, hv // 2, n_conv)
            w_out = dn_w_out[j]
        else:
            p = _proj(h, hg_w_in[j], 0, hg_w_in.shape[2]).reshape(b, t, -1)
            o = _hgrn2(p, lb_cum[:, i], hg_norm_g[j][None, :], lc, n_heads)
            w_out = hg_w_out[j]
        xs, h2 = _out_ln(o.reshape(m, -1), w_out.astype(BF16), xs, modtab[i], modtab[i],
                         ln_g[i, 0][None, :], ln_b[i, 0][None, :], *rows, alpha, 2, 3, 4, F32)
        f, rg = _moe(h2, router_w, router_b, moe_wg, moe_wu, moe_wd, i)
        nxt = modtab[min(i + 1, depth - 1)]
        xs, h = _comb_ln(f, rg, xs, modtab[i], nxt, ln_g[i, 1][None, :], ln_b[i, 1][None, :],
                         *rows, alpha, 5, 0, 1, BF16, latent_only=(i == depth - 1))
    return xs.reshape(b, l, d)
```

```python
import functools
import math

import jax
import jax.numpy as jnp
from jax import lax
from jax.experimental import pallas as pl
from jax.experimental.pallas import tpu as pltpu

F32 = jnp.float32
BF16 = jnp.bfloat16

NORM_EPS = 1e-6
HEAD_DIM = 128
GRID_W = 64
ROPE_BASE = 10000.0
ROPE_FREQS = HEAD_DIM // 4
ATTN_GROUP = 4
DN_CONV_W = 5
DN_CHUNK = 64
HG_CHUNK = 16
HG_SUPER = 64
N_EXPERTS = 16
N_GROUPS = 4
EXPERTS_PER_GROUP = N_EXPERTS // N_GROUPS
TOP_K = 2
EXPERT_BLOCK = 256
CODE_K_SHIFT = 16
CODE_K_UNIT = 1 << CODE_K_SHIFT
N_MOD = 6
VMEM_LIMIT = 56 * 1024 * 1024


def _cparams(*sem):
    return pltpu.CompilerParams(dimension_semantics=sem, vmem_limit_bytes=VMEM_LIMIT)


def _pick(n, prefs):
    for p in prefs:
        if n % p == 0:
            return p
    return n


def _bdot(a, b):
    return jnp.dot(a.astype(BF16), b.astype(BF16), preferred_element_type=F32)


def _bdot_nt(a, b):
    return lax.dot_general(a.astype(BF16), b.astype(BF16), (((1,), (1,)), ((), ())),
                           preferred_element_type=F32)


def _bdot_tn(a, b):
    return lax.dot_general(a.astype(BF16), b.astype(BF16), (((0,), (0,)), ((), ())),
                           preferred_element_type=F32)


def _silu(x):
    return x * jax.nn.sigmoid(x)


def _softplus(x):
    return jnp.maximum(x, 0.0) + jnp.log1p(jnp.exp(-jnp.abs(x)))


def _row_to_col(row, n):
    eye = lax.broadcasted_iota(jnp.int32, (n, n), 0) == lax.broadcasted_iota(jnp.int32, (n, n), 1)
    return jnp.sum(jnp.where(eye, row, 0.0), axis=1, keepdims=True)


def _ada_kernel(cc_ref, w_ref, b_ref, o_ref):
    a = _silu(cc_ref[...])
    o_ref[0] = _bdot(a, w_ref[0]) + b_ref[0]


def _ada_mod(cc, ada_w, ada_b):
    depth, d, n = ada_w.shape
    mp = cc.shape[0]
    tn = _pick(n, (1024, 512, 256, 128))
    return pl.pallas_call(
        _ada_kernel,
        grid=(depth, n // tn),
        in_specs=[pl.BlockSpec((mp, d), lambda i, j: (0, 0)),
                  pl.BlockSpec((1, d, tn), lambda i, j: (i, 0, j)),
                  pl.BlockSpec((1, 1, tn), lambda i, j: (i, 0, j))],
        out_specs=pl.BlockSpec((1, mp, tn), lambda i, j: (i, 0, j)),
        out_shape=jax.ShapeDtypeStruct((depth, mp, n), F32),
        compiler_params=_cparams("parallel", "parallel"),
        name="ada_mod",
    )(cc, ada_w, ada_b.reshape(depth, 1, n))


def _modulate_kernel(ctx_ref, lat_ref, mod_ref, xs_ref, h_ref, *, shift, scale, tiles_per_seq, ctx_tiles):
    m = mod_ref[0]
    is_ctx = pl.program_id(0) % tiles_per_seq < ctx_tiles
    x = jnp.where(is_ctx, ctx_ref[0], lat_ref[0])
    xs_ref[...] = x
    h_ref[...] = (x * (1.0 + m[scale:scale + 1]) + m[shift:shift + 1]).astype(h_ref.dtype)


def _mod_index(tiles_per_seq, ctx_tiles):
    def index(i):
        return ((i // tiles_per_seq) * 2 + jnp.where(i % tiles_per_seq < ctx_tiles, 0, 1), 0, 0)
    return index


def _modulate(ctx, x, modtab, tm, tiles_per_seq, ctx_tiles, shift, scale, dtype):
    b, _, d = x.shape
    midx = _mod_index(tiles_per_seq, ctx_tiles)
    m = b * tiles_per_seq * tm
    return pl.pallas_call(
        functools.partial(_modulate_kernel, shift=shift, scale=scale, tiles_per_seq=tiles_per_seq,
                          ctx_tiles=ctx_tiles),
        grid=(m // tm,),
        in_specs=[pl.BlockSpec((1, tm, d), lambda i: (i // tiles_per_seq,
                                                      jnp.minimum(i % tiles_per_seq, ctx_tiles - 1), 0)),
                  pl.BlockSpec((1, tm, d), lambda i: (i // tiles_per_seq,
                                                      jnp.maximum(i % tiles_per_seq - ctx_tiles, 0), 0)),
                  pl.BlockSpec((1, N_MOD, d), midx)],
        out_specs=[pl.BlockSpec((tm, d), lambda i: (i, 0)), pl.BlockSpec((tm, d), lambda i: (i, 0))],
        out_shape=[jax.ShapeDtypeStruct((m, d), F32), jax.ShapeDtypeStruct((m, d), dtype)],
        compiler_params=_cparams("parallel"),
        name="modulate",
    )(ctx, x, modtab)


def _proj_kernel(a_ref, w_ref, o_ref, wb_ref):
    @pl.when(pl.program_id(1) == 0)
    def _():
        wb_ref[...] = w_ref[...].astype(BF16)

    o_ref[...] = jnp.dot(a_ref[...], wb_ref[...], preferred_element_type=F32).astype(o_ref.dtype)


def _proj(a, w, col0, ncols, out_dtype=F32):
    m, k = a.shape
    tn = _pick(ncols, (1024, 512, 256, 128))
    assert col0 % tn == 0
    tm = _pick(m, (1024, 512, 256, 128, 64))
    c0 = col0 // tn
    return pl.pallas_call(
        _proj_kernel,
        grid=(ncols // tn, m // tm),
        in_specs=[pl.BlockSpec((tm, k), lambda j, i: (i, 0)),
                  pl.BlockSpec((k, tn), lambda j, i: (0, c0 + j))],
        out_specs=pl.BlockSpec((tm, tn), lambda j, i: (i, j)),
        out_shape=jax.ShapeDtypeStruct((m, ncols), out_dtype),
        scratch_shapes=[pltpu.VMEM((k, tn), BF16)],
        compiler_params=_cparams("parallel", "arbitrary"),
        name="proj",
    )(a, w)


def _ln_epilogue(y, x, mod, modn, lng, lnb, alpha, gate, shift, scale):
    r = alpha * x + mod[gate:gate + 1] * y
    rc = r - jnp.mean(r, axis=-1, keepdims=True)
    var = jnp.mean(rc * rc, axis=-1, keepdims=True)
    xn = rc * lax.rsqrt(var + NORM_EPS) * lng + lnb
    h = xn * (1.0 + modn[scale:scale + 1]) + modn[shift:shift + 1]
    return xn, h


def _out_ln_kernel(o_ref, w_ref, x_ref, mod_ref, modn_ref, lng_ref, lnb_ref, xn_ref, h_ref, acc_ref,
                   *, alpha, gate, shift, scale, nk):
    kk = pl.program_id(1)
    part = jnp.dot(o_ref[...], w_ref[...], preferred_element_type=F32)

    def finish(y):
        xn, h = _ln_epilogue(y, x_ref[...], mod_ref[0], modn_ref[0], lng_ref[...], lnb_ref[...],
                             alpha, gate, shift, scale)
        xn_ref[...] = xn
        h_ref[...] = h.astype(h_ref.dtype)

    if nk == 1:
        finish(part)
    else:
        @pl.when(kk == 0)
        def _():
            acc_ref[...] = part

        @pl.when((kk > 0) & (kk < nk - 1))
        def _():
            acc_ref[...] += part

        @pl.when(kk == nk - 1)
        def _():
            finish(acc_ref[...] + part)


def _out_ln(o, w, x, mod, modn, lng, lnb, tm, tiles_per_seq, ctx_tiles, alpha, gate, shift, scale, h_dtype):
    m, k = o.shape
    d = w.shape[1]
    tk = _pick(k, (4096, 2048, 1024, 512, 256, 128))
    midx = _mod_index(tiles_per_seq, ctx_tiles)
    return pl.pallas_call(
        functools.partial(_out_ln_kernel, alpha=alpha, gate=gate, shift=shift, scale=scale, nk=k // tk),
        grid=(m // tm, k // tk),
        in_specs=[pl.BlockSpec((tm, tk), lambda i, j: (i, j)),
                  pl.BlockSpec((tk, d), lambda i, j: (j, 0)),
                  pl.BlockSpec((tm, d), lambda i, j: (i, 0)),
                  pl.BlockSpec((1, N_MOD, d), lambda i, j: midx(i)),
                  pl.BlockSpec((1, N_MOD, d), lambda i, j: midx(i)),
                  pl.BlockSpec((1, d), lambda i, j: (0, 0)),
                  pl.BlockSpec((1, d), lambda i, j: (0, 0))],
        out_specs=[pl.BlockSpec((tm, d), lambda i, j: (i, 0)),
                   pl.BlockSpec((tm, d), lambda i, j: (i, 0))],
        out_shape=[jax.ShapeDtypeStruct((m, d), F32), jax.ShapeDtypeStruct((m, d), h_dtype)],
        scratch_shapes=[pltpu.VMEM((tm, d), F32)],
        compiler_params=_cparams("parallel", "arbitrary"),
        name="out_ln",
    )(o, w, x, mod, modn, lng, lnb)


def _comb_ln_kernel(f0_ref, f1_ref, rg_ref, x_ref, mod_ref, modn_ref, lng_ref, lnb_ref, xn_ref, h_ref,
                    *, alpha, gate, shift, scale):
    y = f0_ref[...] * rg_ref[:, 0:1] + f1_ref[...] * rg_ref[:, 1:2]
    xn, h = _ln_epilogue(y, x_ref[...], mod_ref[0], modn_ref[0], lng_ref[...], lnb_ref[...],
                         alpha, gate, shift, scale)
    xn_ref[...] = xn
    h_ref[...] = h.astype(h_ref.dtype)


def _comb_ln(f, rg, x, mod, modn, lng, lnb, tm, tiles_per_seq, ctx_tiles, alpha, gate, shift, scale, h_dtype,
             latent_only=False):
    m, d = x.shape
    midx = _mod_index(tiles_per_seq, ctx_tiles)
    nt = m // tm
    if latent_only:
        lat_tiles = tiles_per_seq - ctx_tiles
        n_out = nt // tiles_per_seq * lat_tiles
        src = lambda i: (i // lat_tiles) * tiles_per_seq + ctx_tiles + i % lat_tiles
    else:
        n_out = nt
        src = lambda i: i
    return pl.pallas_call(
        functools.partial(_comb_ln_kernel, alpha=alpha, gate=gate, shift=shift, scale=scale),
        grid=(n_out,),
        in_specs=[pl.BlockSpec((tm, d), lambda i: (src(i), 0)),
                  pl.BlockSpec((tm, d), lambda i: (nt + src(i), 0)),
                  pl.BlockSpec((tm, HEAD_DIM), lambda i: (src(i), 0)),
                  pl.BlockSpec((tm, d), lambda i: (src(i), 0)),
                  pl.BlockSpec((1, N_MOD, d), lambda i: midx(src(i))),
                  pl.BlockSpec((1, N_MOD, d), lambda i: midx(src(i))),
                  pl.BlockSpec((1, d), lambda i: (0, 0)),
                  pl.BlockSpec((1, d), lambda i: (0, 0))],
        out_specs=[pl.BlockSpec((tm, d), lambda i: (i, 0)),
                   pl.BlockSpec((tm, d), lambda i: (i, 0))],
        out_shape=[jax.ShapeDtypeStruct((n_out * tm, d), F32), jax.ShapeDtypeStruct((n_out * tm, d), h_dtype)],
        compiler_params=_cparams("parallel"),
        name="comb_ln",
    )(f, f, rg, x, mod, modn, lng, lnb)


def _rope_tables(lc, l):
    pos = jnp.arange(l)
    row = (pos // GRID_W).astype(F32)
    col = (pos % GRID_W).astype(F32)
    inv_freq = ROPE_BASE ** (-jnp.arange(ROPE_FREQS, dtype=F32) / ROPE_FREQS)
    ar = row[:, None] * inv_freq
    ac = col[:, None] * inv_freq
    cos = jnp.concatenate([jnp.cos(ar), jnp.cos(ar), jnp.cos(ac), jnp.cos(ac)], axis=1)
    sin = jnp.concatenate([-jnp.sin(ar), jnp.sin(ar), -jnp.sin(ac), jnp.sin(ac)], axis=1)
    cos = jnp.concatenate([jnp.ones((lc, HEAD_DIM), F32), cos], axis=0)
    sin = jnp.concatenate([jnp.zeros((lc, HEAD_DIM), F32), sin], axis=0)
    return cos, sin


def _norm_rope(x, g, cos, sin):
    y = x * lax.rsqrt(jnp.mean(x * x, axis=-1, keepdims=True) + NORM_EPS) * g
    lane = lax.broadcasted_iota(jnp.int32, y.shape, 1)
    partner = jnp.where(lane % (2 * ROPE_FREQS) < ROPE_FREQS,
                        pltpu.roll(y, HEAD_DIM - ROPE_FREQS, 1), pltpu.roll(y, ROPE_FREQS, 1))
    return y * cos + partner * sin


def _attn_kernel(q_ref, k_ref, v_ref, qg_ref, kg_ref, cos_ref, sin_ref, o_ref, kb_ref, vb_ref,
                 *, lc, tq, t):
    j = pl.program_id(2)

    @pl.when(j == 0)
    def _():
        kb_ref[...] = _norm_rope(k_ref[0], kg_ref[...], cos_ref[...], sin_ref[...]).astype(BF16)
        vb_ref[...] = v_ref[0].astype(BF16)

    r0 = pl.multiple_of(j * tq, tq)
    cos = cos_ref[pl.ds(r0, tq), :]
    sin = sin_ref[pl.ds(r0, tq), :]
    scale = HEAD_DIM ** -0.5

    def attend(n_keys):
        def scores(g):
            qh = _norm_rope(q_ref[0, :, g * HEAD_DIM:(g + 1) * HEAD_DIM], qg_ref[...], cos, sin)
            return _bdot_nt(qh, kb_ref[0:n_keys, :])

        s_next = scores(0)
        for g in range(ATTN_GROUP):
            s = s_next
            if g + 1 < ATTN_GROUP:
                s_next = scores(g + 1)
            e = jnp.exp((s - jnp.max(s, axis=-1, keepdims=True)) * scale)
            den = jnp.sum(e, axis=-1, keepdims=True)
            o = jnp.dot(e.astype(BF16), vb_ref[0:n_keys, :], preferred_element_type=F32) / den
            o_ref[0, :, g * HEAD_DIM:(g + 1) * HEAD_DIM] = o.astype(o_ref.dtype)

    @pl.when(j < lc // tq)
    def _():
        attend(lc)

    @pl.when(j >= lc // tq)
    def _():
        attend(t)


def _attention(p, qg, kg, cos, sin, lc, tq):
    b, t, n = p.shape
    hkv = n // HEAD_DIM // (ATTN_GROUP + 2)
    hq = hkv * ATTN_GROUP
    gw = ATTN_GROUP * HEAD_DIM
    return pl.pallas_call(
        functools.partial(_attn_kernel, lc=lc, tq=tq, t=t),
        grid=(b, hkv, t // tq),
        in_specs=[pl.BlockSpec((1, tq, gw), lambda bi, h, j: (bi, j, h)),
                  pl.BlockSpec((1, t, HEAD_DIM), lambda bi, h, j: (bi, 0, hq + h)),
                  pl.BlockSpec((1, t, HEAD_DIM), lambda bi, h, j: (bi, 0, hq + hkv + h)),
                  pl.BlockSpec((1, HEAD_DIM), lambda bi, h, j: (0, 0)),
                  pl.BlockSpec((1, HEAD_DIM), lambda bi, h, j: (0, 0)),
                  pl.BlockSpec((t, HEAD_DIM), lambda bi, h, j: (0, 0)),
                  pl.BlockSpec((t, HEAD_DIM), lambda bi, h, j: (0, 0))],
        out_specs=pl.BlockSpec((1, tq, gw), lambda bi, h, j: (bi, j, h)),
        out_shape=jax.ShapeDtypeStruct((b, t, hq * HEAD_DIM), BF16),
        scratch_shapes=[pltpu.VMEM((t, HEAD_DIM), BF16), pltpu.VMEM((t, HEAD_DIM), BF16)],
        compiler_params=_cparams("parallel", "parallel", "arbitrary"),
        name="attention",
    )(p, p, p, qg, kg, cos, sin)


def _dn_prep_kernel(p_ref, cw_ref, o_ref, *, lc, t, n_qk_tiles):
    c = pl.program_id(1)
    heads = p_ref.shape[2] // HEAD_DIM
    tt = math.gcd(lc, t - lc)
    tt = _pick(tt, (128, 64, 32, 16, 8))
    halo = 8
    zeros = jnp.zeros((halo, HEAD_DIM), F32)
    post = jnp.where(c * heads < n_qk_tiles, HEAD_DIM ** -0.5, 1.0)
    is_qk = c * heads < 2 * n_qk_tiles
    for r0 in range(0, t, tt):
        for hh in range(heads):
            cols = slice(hh * HEAD_DIM, (hh + 1) * HEAD_DIM)
            top = zeros if r0 in (0, lc) else p_ref[0, r0 - halo:r0, cols]
            bot = zeros if r0 + tt in (lc, t) else p_ref[0, r0 + tt:r0 + tt + halo, cols]
            win = jnp.concatenate([top, p_ref[0, r0:r0 + tt, cols], bot], axis=0)
            acc = win[halo:halo + tt] * cw_ref[DN_CONV_W // 2:DN_CONV_W // 2 + 1, cols]
            for jj in range(DN_CONV_W):
                off = jj - DN_CONV_W // 2
                if off != 0:
                    acc = acc + (pltpu.roll(win, (-off) % (tt + 2 * halo), 0)[halo:halo + tt]
                                 * cw_ref[jj:jj + 1, cols])
            y = _silu(acc)
            nrm = lax.rsqrt(jnp.sum(y * y, axis=-1, keepdims=True) + NORM_EPS)
            o_ref[0, r0:r0 + tt, cols] = y * jnp.where(is_qk, nrm * post, 1.0)


def _dn_prep(p, conv_w, lc, n_qk_tiles):
    b, t, _ = p.shape
    nch = conv_w.shape[1]
    cw = _pick(n_qk_tiles, (4, 2, 1)) * HEAD_DIM
    return pl.pallas_call(
        functools.partial(_dn_prep_kernel, lc=lc, t=t, n_qk_tiles=n_qk_tiles),
        grid=(b, nch // cw),
        in_specs=[pl.BlockSpec((1, t, cw), lambda bi, c: (bi, 0, c)),
                  pl.BlockSpec((DN_CONV_W, cw), lambda bi, c: (0, c))],
        out_specs=pl.BlockSpec((1, t, cw), lambda bi, c: (bi, 0, c)),
        out_shape=jax.ShapeDtypeStruct((b, t, nch), F32),
        compiler_params=_cparams("parallel", "parallel"),
        name="dn_prep",
    )(p, conv_w)


def _dn_gates_kernel(ba_ref, alog_ref, dtb_ref, o_ref, *, t, hv):
    x = ba_ref[0]
    lane = lax.broadcasted_iota(jnp.int32, (DN_CHUNK, 4 * hv), 1)
    g = -jnp.exp(alog_ref[...]) * _softplus(x + dtb_ref[...])
    ii = lax.broadcasted_iota(jnp.int32, (DN_CHUNK, DN_CHUNK), 0)
    jj = lax.broadcasted_iota(jnp.int32, (DN_CHUNK, DN_CHUNK), 1)
    lower = (ii >= jj).astype(F32)
    upper = (ii <= jj).astype(F32)
    for c in range(t // DN_CHUNK):
        rows = slice(c * DN_CHUNK, (c + 1) * DN_CHUNK)
        gch = g[rows]
        pre = jnp.dot(lower, gch, precision=lax.Precision.HIGHEST, preferred_element_type=F32)
        suf = jnp.dot(upper, gch, precision=lax.Precision.HIGHEST, preferred_element_type=F32)
        o_ref[0, rows, :] = jnp.where(lane < 2 * hv, jax.nn.sigmoid(x[rows]),
                                      jnp.where(lane < 3 * hv, pre, suf))


def _dn_gates(ba, a_log, dt_bias):
    b, t, n = ba.shape
    hv = n // 4
    zeros = jnp.zeros((1, 2 * hv), F32)
    alog = jnp.concatenate([zeros, a_log.reshape(1, 2 * hv)], axis=1)
    dtb = jnp.concatenate([zeros, dt_bias.reshape(1, 2 * hv)], axis=1)
    return pl.pallas_call(
        functools.partial(_dn_gates_kernel, t=t, hv=hv),
        grid=(b,),
        in_specs=[pl.BlockSpec((1, t, n), lambda bi: (bi, 0, 0)),
                  pl.BlockSpec((1, n), lambda bi: (0, 0)),
                  pl.BlockSpec((1, n), lambda bi: (0, 0))],
        out_specs=pl.BlockSpec((1, t, n), lambda bi: (bi, 0, 0)),
        out_shape=jax.ShapeDtypeStruct((b, t, n), F32),
        compiler_params=_cparams("parallel"),
        name="dn_gates",
    )(ba, alog, dtb)


def _dn_chunk_prep(kk, qk_raw, qc, kc, vc, beta_r, gc_r, reverse):
    c = DN_CHUNK
    ii = lax.broadcasted_iota(jnp.int32, (c, c), 0)
    jj = lax.broadcasted_iota(jnp.int32, (c, c), 1)
    incl = (ii <= jj) if reverse else (ii >= jj)
    strict = (ii < jj) if reverse else (ii > jj)
    beta_c = _row_to_col(beta_r, c)
    gc_c = _row_to_col(gc_r, c)
    g_end = gc_r[:, 0:1] if reverse else gc_r[:, c - 1:c]
    decay = jnp.exp(jnp.where(incl, gc_c - gc_r, -jnp.inf))
    m = jnp.where(strict, beta_c * kk * decay, 0.0)
    eg = jnp.exp(gc_c)
    x = jnp.concatenate([vc * beta_c, kc * (beta_c * eg)], axis=1)
    qk = jnp.where(incl, qk_raw * decay, 0.0)
    return (m, x, qk.astype(BF16), qc * eg, (kc * jnp.exp(g_end - gc_c)).astype(BF16), jnp.exp(g_end))


def _unit_triangular_solve(ms, xs):
    pws = ms
    ns = [-m for m in ms]
    for _ in range(int(math.log2(DN_CHUNK)) - 1):
        pws = [_bdot(p, p) for p in pws]
        ns = [n + p + _bdot(p, n) for p, n in zip(pws, ns)]
    return [x + _bdot(n, x) for n, x in zip(ns, xs)]


def _dn_core_kernel(q_ref, k_ref, v_ref, z_ref, gt_ref, ng_ref, o_ref,
                    a_ref, b_ref, c_ref, ge_ref, s_ref, acc_ref, *, lc, t, hv):
    qh = pl.program_id(1)
    c64 = DN_CHUNK
    nc = t // c64
    ncc = lc // c64
    combos = [(sub, d) for sub in range(2) for d in range(2)]

    cpi = _pick(nc, (6, 4, 3, 2))

    def phase_a(it, carry):
        prepped, where = [], []
        for j in range(cpi):
            c = it * cpi + j
            r0 = pl.multiple_of(c * c64, c64)
            rows = pl.ds(r0, c64)
            qc, kc = q_ref[0, rows, :], k_ref[0, rows, :]
            gram = _bdot_nt(jnp.concatenate([kc, qc], axis=0), kc)
            kk, qk_raw = gram[:c64], gram[c64:]
            for ci, (sub, d) in enumerate(combos):
                head = qh * 2 + sub
                beta_r = gt_ref[0, d * hv + head, pl.ds(c, 1), :]
                gc_r = gt_ref[0, (2 + d) * hv + head, pl.ds(c, 1), :]
                vc = v_ref[0, rows, sub * HEAD_DIM:(sub + 1) * HEAD_DIM]
                prepped.append(_dn_chunk_prep(kk, qk_raw, qc, kc, vc, beta_r, gc_r, d == 1))
                where.append((ci, c, rows, sub, d))
        sols = _unit_triangular_solve([p[0] for p in prepped], [p[1] for p in prepped])
        kdx = [_bdot_tn(p[4], x) for p, x in zip(prepped, sols)]
        qkx = [_bdot(p[2], x) for p, x in zip(prepped, sols)]
        for (ci, c, rows, sub, d), kx, qx, (_, _, _, qd, _, ge) in zip(where, kdx, qkx, prepped):
            b_ref[ci, c] = kx[:, :HEAD_DIM]
            a_ref[ci, c] = kx[:, HEAD_DIM:].astype(BF16)
            c_ref[ci, rows, :] = (qd - qx[:, HEAD_DIM:]).astype(BF16)
            acc_ref[d, rows, sub * HEAD_DIM:(sub + 1) * HEAD_DIM] = qx[:, :HEAD_DIM]
            ge_ref[ci, pl.ds(c, 1), :] = jnp.broadcast_to(ge, (1, HEAD_DIM))
        return carry

    lax.fori_loop(0, nc // cpi, phase_a, 0)

    s_ref[...] = jnp.zeros_like(s_ref)

    def phase_b(n, carry):
        cs = [n if d == 0 else jnp.where(n < ncc, ncc - 1 - n, nc - 1 - (n - ncc)) for _, d in combos]
        ss = [s_ref[ci] for ci in range(4)]
        sb = [s.astype(BF16) for s in ss]
        m1 = [jnp.dot(a_ref[ci, cs[ci]], sb[ci], preferred_element_type=F32) for ci in range(4)]
        m2 = [jnp.dot(c_ref[ci, pl.ds(pl.multiple_of(cs[ci] * c64, c64), c64), :], sb[ci],
                      preferred_element_type=F32) for ci in range(4)]
        for ci, (sub, d) in enumerate(combos):
            s_ref[ci] = ss[ci] * ge_ref[ci, pl.ds(cs[ci], 1), :][:, 0:1] - m1[ci] + b_ref[ci, cs[ci]]
            acc_ref[d, pl.ds(pl.multiple_of(cs[ci] * c64, c64), c64), sub * HEAD_DIM:(sub + 1) * HEAD_DIM] += m2[ci]
        return carry

    lax.fori_loop(0, nc, phase_b, 0)

    for sub in range(2):
        cols = slice(sub * HEAD_DIM, (sub + 1) * HEAD_DIM)
        o = acc_ref[0, :, cols] + acc_ref[1, :, cols]
        o = o * lax.rsqrt(jnp.mean(o * o, axis=-1, keepdims=True) + NORM_EPS) * ng_ref[...]
        o_ref[0, :, cols] = (o * _silu(z_ref[0, :, cols])).astype(o_ref.dtype)


def _dn_core(qkv, pz, gt, norm_g, lc, n_qk_heads, z_col0):
    b, t, _ = qkv.shape
    hv = 2 * n_qk_heads
    vw = 2 * HEAD_DIM
    nc = t // DN_CHUNK
    return pl.pallas_call(
        functools.partial(_dn_core_kernel, lc=lc, t=t, hv=hv),
        grid=(b, n_qk_heads),
        in_specs=[pl.BlockSpec((1, t, HEAD_DIM), lambda bi, h: (bi, 0, h)),
                  pl.BlockSpec((1, t, HEAD_DIM), lambda bi, h: (bi, 0, n_qk_heads + h)),
                  pl.BlockSpec((1, t, vw), lambda bi, h: (bi, 0, n_qk_heads + h)),
                  pl.BlockSpec((1, t, vw), lambda bi, h: (bi, 0, z_col0 // vw + h)),
                  pl.BlockSpec((1, 4 * hv, nc, DN_CHUNK), lambda bi, h: (bi, 0, 0, 0)),
                  pl.BlockSpec((1, HEAD_DIM), lambda bi, h: (0, 0))],
        out_specs=pl.BlockSpec((1, t, vw), lambda bi, h: (bi, 0, h)),
        out_shape=jax.ShapeDtypeStruct((b, t, hv * HEAD_DIM), BF16),
        scratch_shapes=[pltpu.VMEM((4, nc, HEAD_DIM, HEAD_DIM), BF16),
                        pltpu.VMEM((4, nc, HEAD_DIM, HEAD_DIM), F32),
                        pltpu.VMEM((4, t, HEAD_DIM), BF16),
                        pltpu.VMEM((4, -(-nc // 8) * 8, HEAD_DIM), F32),
                        pltpu.VMEM((4, HEAD_DIM, HEAD_DIM), F32),
                        pltpu.VMEM((2, t, vw), F32)],
        compiler_params=_cparams("parallel", "parallel"),
        name="dn_core",
    )(qkv, qkv, qkv, pz, gt, norm_g)


def _hg_kernel(q_ref, i_ref, gate_ref, ff_ref, fb_ref, lb_ref, ng_ref, o_ref,
               lf_ref, k_ref, st_ref, acc_ref, *, lc, t):
    cs, c16 = HG_SUPER, HG_CHUNK
    nsub = cs // c16
    nc = t // cs
    ncc = lc // cs
    ii = lax.broadcasted_iota(jnp.int32, (cs, cs), 0)
    jj = lax.broadcasted_iota(jnp.int32, (cs, cs), 1)
    krow = lax.broadcasted_iota(jnp.int32, (cs, HEAD_DIM), 0)
    incl = [ii >= jj, ii <= jj]
    sums = [jnp.concatenate([m.astype(BF16), (m & (ii // c16 == jj // c16)).astype(BF16),
                             jnp.ones((8, cs), BF16)], axis=0) for m in incl]
    for d in range(2):
        f_ref = ff_ref if d == 0 else fb_ref
        lb = lb_ref[d:d + 1, :]
        fg = lb + (1.0 - lb) * jax.nn.sigmoid(f_ref[0])
        k_ref[d] = 1.0 - fg
        lf_ref[d] = jnp.log(fg)
    st_ref[...] = jnp.zeros_like(st_ref)
    per_it = _pick(nc, (4, 3, 2))

    def body(it, carry):
        jobs = []
        for u in range(per_it):
            n = it * per_it + u
            jobs.append((0, pl.multiple_of(n * cs, cs)))
            jobs.append((1, pl.multiple_of(jnp.where(n < ncc, ncc - 1 - n, nc - 1 - (n - ncc)) * cs, cs)))
        lfs = [lf_ref[d, pl.ds(r0, cs), :] for d, r0 in jobs]
        kcs = [k_ref[d, pl.ds(r0, cs), :] for d, r0 in jobs]
        qcs = [q_ref[0, pl.ds(r0, cs), :] for d, r0 in jobs]
        vvs = [i_ref[0, pl.ds(r0, cs), :].astype(BF16) for d, r0 in jobs]
        scans = []
        for (d, _), lf in zip(jobs, lfs):
            hi = lf.astype(BF16)
            r1 = lf - hi.astype(F32)
            mid = r1.astype(BF16)
            lo = (r1 - mid.astype(F32)).astype(BF16)
            sc = jnp.dot(sums[d], jnp.concatenate([hi, mid, lo], axis=1), preferred_element_type=F32)
            scans.append(sc[:, :HEAD_DIM] + sc[:, HEAD_DIM:2 * HEAD_DIM] + sc[:, 2 * HEAD_DIM:])
        gcs = [sc[:cs] for sc in scans]
        tots = [sc[2 * cs:2 * cs + 1] for sc in scans]
        qas = [q * jnp.exp(sc[cs:2 * cs]) for q, sc in zip(qcs, scans)]
        qds = [q * jnp.exp(g) for q, g in zip(qcs, gcs)]
        kds = [k * jnp.exp(tt - g) for k, tt, g in zip(kcs, tots, gcs)]
        upd = [_bdot_tn(v, kd) for v, kd in zip(vvs, kds)]
        scaled = []
        for (d, _), gc, kc in zip(jobs, gcs, kcs):
            for a in range(nsub):
                if d == 0:
                    ref = gc[a * c16 - 1:a * c16, :] if a > 0 else None
                    allowed = krow < (a + 1) * c16
                else:
                    ref = gc[(a + 1) * c16:(a + 1) * c16 + 1, :] if a < nsub - 1 else None
                    allowed = krow >= a * c16
                expo = -gc if ref is None else ref - gc
                scaled.append(kc * jnp.exp(jnp.where(allowed, expo, -jnp.inf)))
        parts = [_bdot_nt(qas[j][a * c16:(a + 1) * c16], scaled[j * nsub + a])
                 for j in range(len(jobs)) for a in range(nsub)]
        ps = [jnp.where(incl[d], jnp.concatenate(parts[j * nsub:(j + 1) * nsub], axis=0), 0.0)
              for j, (d, _) in enumerate(jobs)]
        intra = [_bdot(p, v) for p, v in zip(ps, vvs)]
        sts = [st_ref[d] for d in range(2)]
        for j, (d, r0) in enumerate(jobs):
            acc_ref[d, pl.ds(r0, cs), :] = intra[j] + _bdot_nt(qds[j], sts[d])
            sts[d] = sts[d] * jnp.exp(tots[j]) + upd[j]
        for d in range(2):
            st_ref[d] = sts[d]
        return carry

    lax.fori_loop(0, nc // per_it, body, 0)
    o = acc_ref[0] + acc_ref[1]
    o = o * lax.rsqrt(jnp.mean(o * o, axis=-1, keepdims=True) + NORM_EPS) * ng_ref[...]
    o_ref[0] = (o * jax.nn.sigmoid(gate_ref[0])).astype(o_ref.dtype)


def _hgrn2(p, lb, norm_g, lc, n_heads):
    b, t, _ = p.shape
    assert lc % HG_SUPER == 0 and t % HG_SUPER == 0
    blk = lambda k: pl.BlockSpec((1, t, HEAD_DIM), lambda bi, h, k=k: (bi, 0, k * n_heads + h))
    seq = lambda dt: pltpu.VMEM((2, t, HEAD_DIM), dt)
    return pl.pallas_call(
        functools.partial(_hg_kernel, lc=lc, t=t),
        grid=(b, n_heads),
        in_specs=[blk(0), blk(1), blk(2), blk(3), blk(4),
                  pl.BlockSpec((2, HEAD_DIM), lambda bi, h: (0, h)),
                  pl.BlockSpec((1, HEAD_DIM), lambda bi, h: (0, 0))],
        out_specs=pl.BlockSpec((1, t, HEAD_DIM), lambda bi, h: (bi, 0, h)),
        out_shape=jax.ShapeDtypeStruct((b, t, n_heads * HEAD_DIM), BF16),
        scratch_shapes=[seq(F32), seq(F32), pltpu.VMEM((2, HEAD_DIM, HEAD_DIM), F32), seq(F32)],
        compiler_params=_cparams("parallel", "parallel"),
        name="hgrn2",
    )(p, p, p, p, p, lb, norm_g)


def _split_bf16(x):
    hi = x.astype(BF16)
    return hi, (x - hi.astype(F32)).astype(BF16)


def _router_kernel(h_ref, w_ref, b_ref, e_ref, g_ref):
    h_hi, h_lo = _split_bf16(h_ref[...])
    w_hi, w_lo = _split_bf16(w_ref[...])
    nt = (((1,), (1,)), ((), ()))
    logits = (lax.dot_general(w_hi, h_hi, nt, preferred_element_type=F32)
              + lax.dot_general(w_hi, h_lo, nt, preferred_element_type=F32)
              + lax.dot_general(w_lo, h_hi, nt, preferred_element_type=F32))
    scores = jax.nn.sigmoid(logits)
    sel = scores + b_ref[...]
    epg = EXPERTS_PER_GROUP
    rows = lambda a, e: a[e:e + 1, :]
    best_gs = None
    for g in range(N_GROUPS):
        gs = None
        for a in range(epg):
            for c in range(a + 1, epg):
                pair = rows(sel, g * epg + a) + rows(sel, g * epg + c)
                gs = pair if gs is None else jnp.maximum(gs, pair)
        if best_gs is None:
            best_gs, group = gs, jnp.zeros(gs.shape, jnp.int32)
        else:
            upd = gs > best_gs
            best_gs = jnp.where(upd, gs, best_gs)
            group = jnp.where(upd, g, group)

    def in_group(a, l):
        v = rows(a, l)
        for g in range(1, N_GROUPS):
            v = jnp.where(group == g, rows(a, g * epg + l), v)
        return v

    sel_l = [in_group(sel, l) for l in range(epg)]
    sc_l = [in_group(scores, l) for l in range(epg)]

    def top1(exclude):
        best, idx, sc = None, None, None
        for l in range(epg):
            v = sel_l[l] if exclude is None else jnp.where(exclude == l, -jnp.inf, sel_l[l])
            if best is None:
                best, idx, sc = v, jnp.zeros(v.shape, jnp.int32), sc_l[l]
            else:
                upd = v > best
                best = jnp.where(upd, v, best)
                idx = jnp.where(upd, l, idx)
                sc = jnp.where(upd, sc_l[l], sc)
        return idx, sc

    l0, s0 = top1(None)
    l1, s1 = top1(l0)
    e_ref[0:1, :] = group * epg + l0
    e_ref[1:2, :] = group * epg + l1
    den = s0 + s1
    g_ref[0:1, :] = s0 / den
    g_ref[1:2, :] = s1 / den


def _router(h, router_w, router_b):
    m, d = h.shape
    tm = _pick(m, (512, 256, 128))
    ne = router_w.shape[1]
    return pl.pallas_call(
        _router_kernel,
        grid=(m // tm,),
        in_specs=[pl.BlockSpec((tm, d), lambda i: (i, 0)),
                  pl.BlockSpec((ne, d), lambda i: (0, 0)),
                  pl.BlockSpec((ne, 1), lambda i: (0, 0))],
        out_specs=[pl.BlockSpec((TOP_K, tm), lambda i: (0, i)),
                   pl.BlockSpec((TOP_K, tm), lambda i: (0, i))],
        out_shape=[jax.ShapeDtypeStruct((TOP_K, m), jnp.int32), jax.ShapeDtypeStruct((TOP_K, m), F32)],
        compiler_params=_cparams("parallel"),
        name="router",
    )(h, router_w.T, router_b.reshape(ne, 1))


def _moe_ffn_kernel(bexp_ref, code_ref, h_hbm, wg_ref, wu_ref, wd_ref, f_hbm,
                    x0, x1, y0, y1, sem_in, sem_out, *, n_tok):
    i = pl.program_id(0)
    last = pl.num_programs(0) - 1
    eb = EXPERT_BLOCK
    xbufs, ybufs = (x0, x1), (y0, y1)

    def gather(blk, sl, r):
        tok = code_ref[(blk + 1) * eb + r] & (CODE_K_UNIT - 1)
        return pltpu.make_async_copy(h_hbm.at[pl.ds(tok, 1)], xbufs[sl].at[pl.ds(r, 1)], sem_in.at[sl])

    def scatter(blk, sl, r):
        code = code_ref[(blk + 1) * eb + r]
        dst = lax.shift_right_logical(code, CODE_K_SHIFT) * n_tok + (code & (CODE_K_UNIT - 1))
        return pltpu.make_async_copy(ybufs[sl].at[pl.ds(r, 1)], f_hbm.at[pl.ds(dst, 1)], sem_out.at[sl])

    def for_rows(fn):
        def body(r, carry):
            fn(r)
            return carry
        lax.fori_loop(0, eb, body, 0, unroll=8)

    @pl.when(i == 0)
    def _():
        for sl in range(2):
            ybufs[sl][...] = jnp.zeros_like(ybufs[sl])
            spare = pltpu.make_async_copy(ybufs[sl], f_hbm.at[pl.ds(TOP_K * n_tok + sl * eb, eb)],
                                          sem_out.at[sl])
            spare.start()
            spare.wait()
        for_rows(lambda r: gather(0, 0, r).start())

    def step(sl):
        @pl.when(i >= 1)
        def _():
            for_rows(lambda r: scatter(i - 2, sl, r).wait())
        for_rows(lambda r: gather(i, sl, r).wait())
        x = xbufs[sl][...].astype(BF16)
        for r in range(eb):
            gather(i + 1, 1 - sl, r).start()
        a = _silu(jnp.dot(x, wg_ref[0, 0], preferred_element_type=F32)) \
            * jnp.dot(x, wu_ref[0, 0], preferred_element_type=F32)
        for r in range(eb):
            scatter(i - 1, 1 - sl, r).start()
        ybufs[sl][...] = jnp.dot(a.astype(BF16), wd_ref[0, 0], preferred_element_type=F32)

    @pl.when(i % 2 == 0)
    def _():
        step(0)

    @pl.when(i % 2 == 1)
    def _():
        step(1)

    @pl.when(i == last)
    def _():
        for sl in range(2):
            @pl.when(i % 2 == sl)
            def _():
                for_rows(lambda r: scatter(i - 1, 1 - sl, r).wait())
                for_rows(lambda r: gather(i + 1, 1 - sl, r).wait())


def _moe_ffn(h, block_expert, code_buf, wg, wu, wd, layer):
    m, d = h.shape
    de = wg.shape[3]
    nb = block_expert.shape[0]
    wmap = lambda i, be, cb: (layer, be[jnp.minimum(i, nb - 1)], 0, 0)
    buf = pltpu.VMEM((EXPERT_BLOCK, d), F32)
    grid_spec = pltpu.PrefetchScalarGridSpec(
        num_scalar_prefetch=2,
        grid=(nb + 1,),
        in_specs=[pl.BlockSpec(memory_space=pl.ANY),
                  pl.BlockSpec((1, 1, d, de), wmap),
                  pl.BlockSpec((1, 1, d, de), wmap),
                  pl.BlockSpec((1, 1, de, d), wmap)],
        out_specs=pl.BlockSpec(memory_space=pl.ANY),
        scratch_shapes=[buf, buf, buf, buf, pltpu.SemaphoreType.DMA((2,)), pltpu.SemaphoreType.DMA((2,))],
    )
    return pl.pallas_call(
        functools.partial(_moe_ffn_kernel, n_tok=m),
        grid_spec=grid_spec,
        out_shape=jax.ShapeDtypeStruct((TOP_K * m + 2 * EXPERT_BLOCK, d), F32),
        compiler_params=_cparams("arbitrary"),
        name="moe_ffn",
    )(block_expert, code_buf, h, wg, wu, wd)


def _moe_dispatch(expert):
    m = expert.shape[1]
    n_pairs = m * TOP_K
    e_flat = expert.T.reshape(-1)
    onehot = (e_flat[:, None] == jnp.arange(N_EXPERTS)[None, :]).astype(jnp.int32)
    csum = jnp.cumsum(onehot, axis=0)
    rank = jnp.take_along_axis(csum, e_flat[:, None], axis=1)[:, 0] - 1
    counts = csum[-1]
    padded = (counts + EXPERT_BLOCK - 1) // EXPERT_BLOCK * EXPERT_BLOCK
    pend = jnp.cumsum(padded)
    pstart = pend - padded
    dest = pstart[e_flat] + rank
    n_blocks = -(-(n_pairs + N_EXPERTS * (EXPERT_BLOCK - 1)) // EXPERT_BLOCK)
    n_rows = n_blocks * EXPERT_BLOCK
    assert 2 * EXPERT_BLOCK <= m <= CODE_K_UNIT
    pair = jnp.arange(n_pairs, dtype=jnp.int32)
    row = jnp.arange(n_rows + 3 * EXPERT_BLOCK, dtype=jnp.int32)
    pad_code = TOP_K * CODE_K_UNIT + row % (2 * EXPERT_BLOCK)
    code_buf = pad_code.at[dest + EXPERT_BLOCK].set((pair % TOP_K) * CODE_K_UNIT + pair // TOP_K)
    blk_start = jnp.arange(n_blocks) * EXPERT_BLOCK
    block_expert = jnp.minimum(jnp.searchsorted(pend, blk_start, side='right'), N_EXPERTS - 1).astype(jnp.int32)
    return block_expert, code_buf


def _moe(h2, router_w, router_b, wg, wu, wd, layer):
    expert, gate = _router(h2, router_w, router_b)
    block_expert, code_buf = _moe_dispatch(expert)
    f = _moe_ffn(h2, block_expert, code_buf, wg, wu, wd, layer)
    return f, jnp.pad(gate.T, ((0, 0), (0, HEAD_DIM - TOP_K)))


def kernel(x, c, ctx, c_ctx, ada_w, ada_b, ln_g, ln_b, attn_w_in, attn_q_g, attn_k_g, attn_w_out, dn_w_in, dn_conv, dn_a_log, dn_dt_bias, dn_norm_g, dn_w_out, hg_w_in, hg_lb, hg_norm_g, hg_w_out, router_w, router_b, moe_w_gate, moe_w_up, moe_w_down):
    b, l, d = x.shape
    lc = ctx.shape[1]
    t = lc + l
    m = b * t
    depth = ada_w.shape[0]
    alpha = (2.0 * depth) ** 0.25
    tm = _pick(math.gcd(lc, l), (256, 128, 64))
    tiles_per_seq, ctx_tiles = t // tm, lc // tm
    n_heads = d // HEAD_DIM

    mp = -(-(b + 1) // 8) * 8
    cc = jnp.concatenate([c, c_ctx[None, :], jnp.zeros((mp - b - 1, d), F32)], axis=0)
    mod = _ada_mod(cc, ada_w, ada_b)
    mod_l = mod[:, :b].reshape(depth, b, 1, N_MOD, d)
    mod_c = jnp.broadcast_to(mod[:, b].reshape(depth, 1, 1, N_MOD, d), (depth, b, 1, N_MOD, d))
    modtab = jnp.concatenate([mod_c, mod_l], axis=2).reshape(depth, 2 * b, N_MOD, d)

    lb_cum = jnp.cumsum(jax.nn.softmax(hg_lb.astype(F32), axis=1), axis=1)
    lb_cum = lb_cum - lb_cum[:, :1]

    xs, h = _modulate(ctx, x, modtab[0], tm, tiles_per_seq, ctx_tiles, 0, 1, BF16)
    cos, sin = _rope_tables(lc, l)
    rows = (tm, tiles_per_seq, ctx_tiles)

    moe_wg, moe_wu, moe_wd = moe_w_gate.astype(BF16), moe_w_up.astype(BF16), moe_w_down.astype(BF16)

    for i in range(depth):
        kind, j = i % 3, i // 3
        if kind == 0:
            p = _proj(h, attn_w_in[j], 0, attn_w_in.shape[2])
            o = _attention(p.reshape(b, t, -1), attn_q_g[j][None, :], attn_k_g[j][None, :], cos, sin, lc, tm)
            w_out = attn_w_out[j]
        elif kind == 1:
            n_conv = dn_conv.shape[2]
            hv = dn_a_log.shape[2]
            n_main = n_conv + hv * HEAD_DIM
            p = _proj(h, dn_w_in[j], 0, n_main).reshape(b, t, n_main)
            ba = _proj(h, dn_w_in[j], n_main, 4 * hv).reshape(b, t, 4 * hv)
            qkv = _dn_prep(p, dn_conv[j], lc, hv // 2)
            gates = _dn_gates(ba, dn_a_log[j], dn_dt_bias[j])
            gt = jnp.swapaxes(gates, 1, 2).reshape(b, 4 * hv, t // DN_CHUNK, DN_CHUNK)
            o = _dn_core(qkv, p, gt, dn_norm_g[j][None, :], lc, hv // 2, n_conv)
            w_out = dn_w_out[j]
        else:
            p = _proj(h, hg_w_in[j], 0, hg_w_in.shape[2]).reshape(b, t, -1)
            o = _hgrn2(p, lb_cum[:, i], hg_norm_g[j][None, :], lc, n_heads)
            w_out = hg_w_out[j]
        xs, h2 = _out_ln(o.reshape(m, -1), w_out.astype(BF16), xs, modtab[i], modtab[i],
                         ln_g[i, 0][None, :], ln_b[i, 0][None, :], *rows, alpha, 2, 3, 4, F32)
        f, rg = _moe(h2, router_w, router_b, moe_wg, moe_wu, moe_wd, i)
        nxt = modtab[min(i + 1, depth - 1)]
        xs, h = _comb_ln(f, rg, xs, modtab[i], nxt, ln_g[i, 1][None, :], ln_b[i, 1][None, :],
                         *rows, alpha, 5, 0, 1, BF16, latent_only=(i == depth - 1))
    return xs.reshape(b, l, d)
```

```python
import functools
import math

import jax
import jax.numpy as jnp
from jax import lax
from jax.experimental import pallas as pl
from jax.experimental.pallas import tpu as pltpu

F32 = jnp.float32
BF16 = jnp.bfloat16

NORM_EPS = 1e-6
HEAD_DIM = 128
GRID_W = 64
ROPE_BASE = 10000.0
ROPE_FREQS = HEAD_DIM // 4
ATTN_GROUP = 4
DN_CONV_W = 5
DN_CHUNK = 64
HG_CHUNK = 16
HG_SUPER = 64
N_EXPERTS = 16
N_GROUPS = 4
EXPERTS_PER_GROUP = N_EXPERTS // N_GROUPS
TOP_K = 2
EXPERT_BLOCK = 256
CODE_K_SHIFT = 16
CODE_K_UNIT = 1 << CODE_K_SHIFT
N_MOD = 6
VMEM_LIMIT = 56 * 1024 * 1024


def _cparams(*sem):
    return pltpu.CompilerParams(dimension_semantics=sem, vmem_limit_bytes=VMEM_LIMIT)


def _pick(n, prefs):
    for p in prefs:
        if n % p == 0:
            return p
    return n


def _bdot(a, b):
    return jnp.dot(a.astype(BF16), b.astype(BF16), preferred_element_type=F32)


def _bdot_nt(a, b):
    return lax.dot_general(a.astype(BF16), b.astype(BF16), (((1,), (1,)), ((), ())),
                           preferred_element_type=F32)


def _bdot_tn(a, b):
    return lax.dot_general(a.astype(BF16), b.astype(BF16), (((0,), (0,)), ((), ())),
                           preferred_element_type=F32)


def _silu(x):
    return x * jax.nn.sigmoid(x)


def _softplus(x):
    return jnp.maximum(x, 0.0) + jnp.log1p(jnp.exp(-jnp.abs(x)))


def _row_to_col(row, n):
    eye = lax.broadcasted_iota(jnp.int32, (n, n), 0) == lax.broadcasted_iota(jnp.int32, (n, n), 1)
    return jnp.sum(jnp.where(eye, row, 0.0), axis=1, keepdims=True)


def _ada_kernel(cc_ref, w_ref, b_ref, o_ref):
    a = _silu(cc_ref[...])
    o_ref[0] = _bdot(a, w_ref[0]) + b_ref[0]


def _ada_mod(cc, ada_w, ada_b):
    depth, d, n = ada_w.shape
    mp = cc.shape[0]
    tn = _pick(n, (1024, 512, 256, 128))
    return pl.pallas_call(
        _ada_kernel,
        grid=(depth, n // tn),
        in_specs=[pl.BlockSpec((mp, d), lambda i, j: (0, 0)),
                  pl.BlockSpec((1, d, tn), lambda i, j: (i, 0, j)),
                  pl.BlockSpec((1, 1, tn), lambda i, j: (i, 0, j))],
        out_specs=pl.BlockSpec((1, mp, tn), lambda i, j: (i, 0, j)),
        out_shape=jax.ShapeDtypeStruct((depth, mp, n), F32),
        compiler_params=_cparams("parallel", "parallel"),
        name="ada_mod",
    )(cc, ada_w, ada_b.reshape(depth, 1, n))


def _modulate_kernel(ctx_ref, lat_ref, mod_ref, xs_ref, h_ref, *, shift, scale, tiles_per_seq, ctx_tiles):
    m = mod_ref[0]
    is_ctx = pl.program_id(0) % tiles_per_seq < ctx_tiles
    x = jnp.where(is_ctx, ctx_ref[0], lat_ref[0])
    xs_ref[...] = x
    h_ref[...] = (x * (1.0 + m[scale:scale + 1]) + m[shift:shift + 1]).astype(h_ref.dtype)


def _mod_index(tiles_per_seq, ctx_tiles):
    def index(i):
        return ((i // tiles_per_seq) * 2 + jnp.where(i % tiles_per_seq < ctx_tiles, 0, 1), 0, 0)
    return index


def _modulate(ctx, x, modtab, tm, tiles_per_seq, ctx_tiles, shift, scale, dtype):
    b, _, d = x.shape
    midx = _mod_index(tiles_per_seq, ctx_tiles)
    m = b * tiles_per_seq * tm
    return pl.pallas_call(
        functools.partial(_modulate_kernel, shift=shift, scale=scale, tiles_per_seq=tiles_per_seq,
                          ctx_tiles=ctx_tiles),
        grid=(m // tm,),
        in_specs=[pl.BlockSpec((1, tm, d), lambda i: (i // tiles_per_seq,
                                                      jnp.minimum(i % tiles_per_seq, ctx_tiles - 1), 0)),
                  pl.BlockSpec((1, tm, d), lambda i: (i // tiles_per_seq,
                                                      jnp.maximum(i % tiles_per_seq - ctx_tiles, 0), 0)),
                  pl.BlockSpec((1, N_MOD, d), midx)],
        out_specs=[pl.BlockSpec((tm, d), lambda i: (i, 0)), pl.BlockSpec((tm, d), lambda i: (i, 0))],
        out_shape=[jax.ShapeDtypeStruct((m, d), F32), jax.ShapeDtypeStruct((m, d), dtype)],
        compiler_params=_cparams("parallel"),
        name="modulate",
    )(ctx, x, modtab)


def _proj_kernel(a_ref, w_ref, o_ref, wb_ref):
    @pl.when(pl.program_id(1) == 0)
    def _():
        wb_ref[...] = w_ref[...].astype(BF16)

    o_ref[...] = jnp.dot(a_ref[...], wb_ref[...], preferred_element_type=F32).astype(o_ref.dtype)


def _proj(a, w, col0, ncols, out_dtype=F32):
    m, k = a.shape
    tn = _pick(ncols, (1024, 512, 256, 128))
    assert col0 % tn == 0
    tm = _pick(m, (1024, 512, 256, 128, 64))
    c0 = col0 // tn
    return pl.pallas_call(
        _proj_kernel,
        grid=(ncols // tn, m // tm),
        in_specs=[pl.BlockSpec((tm, k), lambda j, i: (i, 0)),
                  pl.BlockSpec((k, tn), lambda j, i: (0, c0 + j))],
        out_specs=pl.BlockSpec((tm, tn), lambda j, i: (i, j)),
        out_shape=jax.ShapeDtypeStruct((m, ncols), out_dtype),
        scratch_shapes=[pltpu.VMEM((k, tn), BF16)],
        compiler_params=_cparams("parallel", "arbitrary"),
        name="proj",
    )(a, w)


def _ln_epilogue(y, x, mod, modn, lng, lnb, alpha, gate, shift, scale):
    r = alpha * x + mod[gate:gate + 1] * y
    rc = r - jnp.mean(r, axis=-1, keepdims=True)
    var = jnp.mean(rc * rc, axis=-1, keepdims=True)
    xn = rc * lax.rsqrt(var + NORM_EPS) * lng + lnb
    h = xn * (1.0 + modn[scale:scale + 1]) + modn[shift:shift + 1]
    return xn, h


def _out_ln_kernel(o_ref, w_ref, x_ref, mod_ref, modn_ref, lng_ref, lnb_ref, xn_ref, h_ref, acc_ref,
                   *, alpha, gate, shift, scale, nk):
    kk = pl.program_id(1)
    part = jnp.dot(o_ref[...], w_ref[...], preferred_element_type=F32)

    def finish(y):
        xn, h = _ln_epilogue(y, x_ref[...], mod_ref[0], modn_ref[0], lng_ref[...], lnb_ref[...],
                             alpha, gate, shift, scale)
        xn_ref[...] = xn
        h_ref[...] = h.astype(h_ref.dtype)

    if nk == 1:
        finish(part)
    else:
        @pl.when(kk == 0)
        def _():
            acc_ref[...] = part

        @pl.when((kk > 0) & (kk < nk - 1))
        def _():
            acc_ref[...] += part

        @pl.when(kk == nk - 1)
        def _():
            finish(acc_ref[...] + part)


def _out_ln(o, w, x, mod, modn, lng, lnb, tm, tiles_per_seq, ctx_tiles, alpha, gate, shift, scale, h_dtype):
    m, k = o.shape
    d = w.shape[1]
    tk = _pick(k, (4096, 2048, 1024, 512, 256, 128))
    midx = _mod_index(tiles_per_seq, ctx_tiles)
    return pl.pallas_call(
        functools.partial(_out_ln_kernel, alpha=alpha, gate=gate, shift=shift, scale=scale, nk=k // tk),
        grid=(m // tm, k // tk),
        in_specs=[pl.BlockSpec((tm, tk), lambda i, j: (i, j)),
                  pl.BlockSpec((tk, d), lambda i, j: (j, 0)),
                  pl.BlockSpec((tm, d), lambda i, j: (i, 0)),
                  pl.BlockSpec((1, N_MOD, d), lambda i, j: midx(i)),
                  pl.BlockSpec((1, N_MOD, d), lambda i, j: midx(i)),
                  pl.BlockSpec((1, d), lambda i, j: (0, 0)),
                  pl.BlockSpec((1, d), lambda i, j: (0, 0))],
        out_specs=[pl.BlockSpec((tm, d), lambda i, j: (i, 0)),
                   pl.BlockSpec((tm, d), lambda i, j: (i, 0))],
        out_shape=[jax.ShapeDtypeStruct((m, d), F32), jax.ShapeDtypeStruct((m, d), h_dtype)],
        scratch_shapes=[pltpu.VMEM((tm, d), F32)],
        compiler_params=_cparams("parallel", "arbitrary"),
        name="out_ln",
    )(o, w, x, mod, modn, lng, lnb)


def _comb_ln_kernel(f0_ref, f1_ref, rg_ref, x_ref, mod_ref, modn_ref, lng_ref, lnb_ref, xn_ref, h_ref,
                    *, alpha, gate, shift, scale):
    y = f0_ref[...] * rg_ref[:, 0:1] + f1_ref[...] * rg_ref[:, 1:2]
    xn, h = _ln_epilogue(y, x_ref[...], mod_ref[0], modn_ref[0], lng_ref[...], lnb_ref[...],
                         alpha, gate, shift, scale)
    xn_ref[...] = xn
    h_ref[...] = h.astype(h_ref.dtype)


def _comb_ln(f, rg, x, mod, modn, lng, lnb, tm, tiles_per_seq, ctx_tiles, alpha, gate, shift, scale, h_dtype,
             latent_only=False):
    m, d = x.shape
    midx = _mod_index(tiles_per_seq, ctx_tiles)
    nt = m // tm
    if latent_only:
        lat_tiles = tiles_per_seq - ctx_tiles
        n_out = nt // tiles_per_seq * lat_tiles
        src = lambda i: (i // lat_tiles) * tiles_per_seq + ctx_tiles + i % lat_tiles
    else:
        n_out = nt
        src = lambda i: i
    return pl.pallas_call(
        functools.partial(_comb_ln_kernel, alpha=alpha, gate=gate, shift=shift, scale=scale),
        grid=(n_out,),
        in_specs=[pl.BlockSpec((tm, d), lambda i: (src(i), 0)),
                  pl.BlockSpec((tm, d), lambda i: (nt + src(i), 0)),
                  pl.BlockSpec((tm, HEAD_DIM), lambda i: (src(i), 0)),
                  pl.BlockSpec((tm, d), lambda i: (src(i), 0)),
                  pl.BlockSpec((1, N_MOD, d), lambda i: midx(src(i))),
                  pl.BlockSpec((1, N_MOD, d), lambda i: midx(src(i))),
                  pl.BlockSpec((1, d), lambda i: (0, 0)),
                  pl.BlockSpec((1, d), lambda i: (0, 0))],
        out_specs=[pl.BlockSpec((tm, d), lambda i: (i, 0)),
                   pl.BlockSpec((tm, d), lambda i: (i, 0))],
        out_shape=[jax.ShapeDtypeStruct((n_out * tm, d), F32), jax.ShapeDtypeStruct((n_out * tm, d), h_dtype)],
        compiler_params=_cparams("parallel"),
        name="comb_ln",
    )(f, f, rg, x, mod, modn, lng, lnb)


def _rope_tables(lc, l):
    pos = jnp.arange(l)
    row = (pos // GRID_W).astype(F32)
    col = (pos % GRID_W).astype(F32)
    inv_freq = ROPE_BASE ** (-jnp.arange(ROPE_FREQS, dtype=F32) / ROPE_FREQS)
    ar = row[:, None] * inv_freq
    ac = col[:, None] * inv_freq
    cos = jnp.concatenate([jnp.cos(ar), jnp.cos(ar), jnp.cos(ac), jnp.cos(ac)], axis=1)
    sin = jnp.concatenate([-jnp.sin(ar), jnp.sin(ar), -jnp.sin(ac), jnp.sin(ac)], axis=1)
    cos = jnp.concatenate([jnp.ones((lc, HEAD_DIM), F32), cos], axis=0)
    sin = jnp.concatenate([jnp.zeros((lc, HEAD_DIM), F32), sin], axis=0)
    return cos, sin


def _norm_rope(x, g, cos, sin):
    y = x * lax.rsqrt(jnp.mean(x * x, axis=-1, keepdims=True) + NORM_EPS) * g
    lane = lax.broadcasted_iota(jnp.int32, y.shape, 1)
    partner = jnp.where(lane % (2 * ROPE_FREQS) < ROPE_FREQS,
                        pltpu.roll(y, HEAD_DIM - ROPE_FREQS, 1), pltpu.roll(y, ROPE_FREQS, 1))
    return y * cos + partner * sin


def _attn_kernel(q_ref, k_ref, v_ref, qg_ref, kg_ref, cos_ref, sin_ref, o_ref, kb_ref, vb_ref,
                 *, lc, tq, t):
    j = pl.program_id(2)

    @pl.when(j == 0)
    def _():
        kb_ref[...] = _norm_rope(k_ref[0], kg_ref[...], cos_ref[...], sin_ref[...]).astype(BF16)
        vb_ref[...] = v_ref[0].astype(BF16)

    r0 = pl.multiple_of(j * tq, tq)
    cos = cos_ref[pl.ds(r0, tq), :]
    sin = sin_ref[pl.ds(r0, tq), :]
    scale = HEAD_DIM ** -0.5

    def attend(n_keys):
        def scores(g):
            qh = _norm_rope(q_ref[0, :, g * HEAD_DIM:(g + 1) * HEAD_DIM], qg_ref[...], cos, sin)
            return _bdot_nt(qh, kb_ref[0:n_keys, :])

        s_next = scores(0)
        for g in range(ATTN_GROUP):
            s = s_next
            if g + 1 < ATTN_GROUP:
                s_next = scores(g + 1)
            e = jnp.exp((s - jnp.max(s, axis=-1, keepdims=True)) * scale)
            den = jnp.sum(e, axis=-1, keepdims=True)
            o = jnp.dot(e.astype(BF16), vb_ref[0:n_keys, :], preferred_element_type=F32) / den
            o_ref[0, :, g * HEAD_DIM:(g + 1) * HEAD_DIM] = o.astype(o_ref.dtype)

    @pl.when(j < lc // tq)
    def _():
        attend(lc)

    @pl.when(j >= lc // tq)
    def _():
        attend(t)


def _attention(p, qg, kg, cos, sin, lc, tq):
    b, t, n = p.shape
    hkv = n // HEAD_DIM // (ATTN_GROUP + 2)
    hq = hkv * ATTN_GROUP
    gw = ATTN_GROUP * HEAD_DIM
    return pl.pallas_call(
        functools.partial(_attn_kernel, lc=lc, tq=tq, t=t),
        grid=(b, hkv, t // tq),
        in_specs=[pl.BlockSpec((1, tq, gw), lambda bi, h, j: (bi, j, h)),
                  pl.BlockSpec((1, t, HEAD_DIM), lambda bi, h, j: (bi, 0, hq + h)),
                  pl.BlockSpec((1, t, HEAD_DIM), lambda bi, h, j: (bi, 0, hq + hkv + h)),
                  pl.BlockSpec((1, HEAD_DIM), lambda bi, h, j: (0, 0)),
                  pl.BlockSpec((1, HEAD_DIM), lambda bi, h, j: (0, 0)),
                  pl.BlockSpec((t, HEAD_DIM), lambda bi, h, j: (0, 0)),
                  pl.BlockSpec((t, HEAD_DIM), lambda bi, h, j: (0, 0))],
        out_specs=pl.BlockSpec((1, tq, gw), lambda bi, h, j: (bi, j, h)),
        out_shape=jax.ShapeDtypeStruct((b, t, hq * HEAD_DIM), BF16),
        scratch_shapes=[pltpu.VMEM((t, HEAD_DIM), BF16), pltpu.VMEM((t, HEAD_DIM), BF16)],
        compiler_params=_cparams("parallel", "parallel", "arbitrary"),
        name="attention",
    )(p, p, p, qg, kg, cos, sin)


def _dn_prep_kernel(p_ref, cw_ref, o_ref, *, lc, t, n_qk_tiles):
    c = pl.program_id(1)
    heads = p_ref.shape[2] // HEAD_DIM
    tt = math.gcd(lc, t - lc)
    tt = _pick(tt, (128, 64, 32, 16, 8))
    halo = 8
    zeros = jnp.zeros((halo, HEAD_DIM), F32)
    post = jnp.where(c * heads < n_qk_tiles, HEAD_DIM ** -0.5, 1.0)
    is_qk = c * heads < 2 * n_qk_tiles
    for r0 in range(0, t, tt):
        for hh in range(heads):
            cols = slice(hh * HEAD_DIM, (hh + 1) * HEAD_DIM)
            top = zeros if r0 in (0, lc) else p_ref[0, r0 - halo:r0, cols]
            bot = zeros if r0 + tt in (lc, t) else p_ref[0, r0 + tt:r0 + tt + halo, cols]
            win = jnp.concatenate([top, p_ref[0, r0:r0 + tt, cols], bot], axis=0)
            acc = win[halo:halo + tt] * cw_ref[DN_CONV_W // 2:DN_CONV_W // 2 + 1, cols]
            for jj in range(DN_CONV_W):
                off = jj - DN_CONV_W // 2
                if off != 0:
                    acc = acc + (pltpu.roll(win, (-off) % (tt + 2 * halo), 0)[halo:halo + tt]
                                 * cw_ref[jj:jj + 1, cols])
            y = _silu(acc)
            nrm = lax.rsqrt(jnp.sum(y * y, axis=-1, keepdims=True) + NORM_EPS)
            o_ref[0, r0:r0 + tt, cols] = y * jnp.where(is_qk, nrm * post, 1.0)


def _dn_prep(p, conv_w, lc, n_qk_tiles):
    b, t, _ = p.shape
    nch = conv_w.shape[1]
    cw = _pick(n_qk_tiles, (4, 2, 1)) * HEAD_DIM
    return pl.pallas_call(
        functools.partial(_dn_prep_kernel, lc=lc, t=t, n_qk_tiles=n_qk_tiles),
        grid=(b, nch // cw),
        in_specs=[pl.BlockSpec((1, t, cw), lambda bi, c: (bi, 0, c)),
                  pl.BlockSpec((DN_CONV_W, cw), lambda bi, c: (0, c))],
        out_specs=pl.BlockSpec((1, t, cw), lambda bi, c: (bi, 0, c)),
        out_shape=jax.ShapeDtypeStruct((b, t, nch), F32),
        compiler_params=_cparams("parallel", "parallel"),
        name="dn_prep",
    )(p, conv_w)


def _dn_gates_kernel(ba_ref, alog_ref, dtb_ref, o_ref, *, t, hv):
    x = ba_ref[0]
    lane = lax.broadcasted_iota(jnp.int32, (DN_CHUNK, 4 * hv), 1)
    g = -jnp.exp(alog_ref[...]) * _softplus(x + dtb_ref[...])
    ii = lax.broadcasted_iota(jnp.int32, (DN_CHUNK, DN_CHUNK), 0)
    jj = lax.broadcasted_iota(jnp.int32, (DN_CHUNK, DN_CHUNK), 1)
    lower = (ii >= jj).astype(F32)
    upper = (ii <= jj).astype(F32)
    for c in range(t // DN_CHUNK):
        rows = slice(c * DN_CHUNK, (c + 1) * DN_CHUNK)
        gch = g[rows]
        pre = jnp.dot(lower, gch, precision=lax.Precision.HIGHEST, preferred_element_type=F32)
        suf = jnp.dot(upper, gch, precision=lax.Precision.HIGHEST, preferred_element_type=F32)
        o_ref[0, rows, :] = jnp.where(lane < 2 * hv, jax.nn.sigmoid(x[rows]),
                                      jnp.where(lane < 3 * hv, pre, suf))


def _dn_gates(ba, a_log, dt_bias):
    b, t, n = ba.shape
    hv = n // 4
    zeros = jnp.zeros((1, 2 * hv), F32)
    alog = jnp.concatenate([zeros, a_log.reshape(1, 2 * hv)], axis=1)
    dtb = jnp.concatenate([zeros, dt_bias.reshape(1, 2 * hv)], axis=1)
    return pl.pallas_call(
        functools.partial(_dn_gates_kernel, t=t, hv=hv),
        grid=(b,),
        in_specs=[pl.BlockSpec((1, t, n), lambda bi: (bi, 0, 0)),
                  pl.BlockSpec((1, n), lambda bi: (0, 0)),
                  pl.BlockSpec((1, n), lambda bi: (0, 0))],
        out_specs=pl.BlockSpec((1, t, n), lambda bi: (bi, 0, 0)),
        out_shape=jax.ShapeDtypeStruct((b, t, n), F32),
        compiler_params=_cparams("parallel"),
        name="dn_gates",
    )(ba, alog, dtb)


def _dn_chunk_prep(kk, qk_raw, qc, kc, vc, beta_r, gc_r, reverse):
    c = DN_CHUNK
    ii = lax.broadcasted_iota(jnp.int32, (c, c), 0)
    jj = lax.broadcasted_iota(jnp.int32, (c, c), 1)
    incl = (ii <= jj) if reverse else (ii >= jj)
    strict = (ii < jj) if reverse else (ii > jj)
    beta_c = _row_to_col(beta_r, c)
    gc_c = _row_to_col(gc_r, c)
    g_end = gc_r[:, 0:1] if reverse else gc_r[:, c - 1:c]
    decay = jnp.exp(jnp.where(incl, gc_c - gc_r, -jnp.inf))
    m = jnp.where(strict, beta_c * kk * decay, 0.0)
    eg = jnp.exp(gc_c)
    x = jnp.concatenate([vc * beta_c, kc * (beta_c * eg)], axis=1)
    qk = jnp.where(incl, qk_raw * decay, 0.0)
    return (m, x, qk.astype(BF16), qc * eg, (kc * jnp.exp(g_end - gc_c)).astype(BF16), jnp.exp(g_end))


def _unit_triangular_solve(ms, xs, between=()):
    pws = ms
    ns = [-m for m in ms]
    between = list(between)
    for _ in range(int(math.log2(DN_CHUNK)) - 1):
        pws = [_bdot(p, p) for p in pws]
        ns = [n + p + _bdot(p, n) for p, n in zip(pws, ns)]
        if between:
            between.pop(0)()
    xs = [x + _bdot(n, x) for n, x in zip(ns, xs)]
    for fn in between:
        fn()
    return xs


def _dn_core_kernel(q_ref, k_ref, v_ref, z_ref, gt_ref, ng_ref, o_ref,
                    a_ref, b_ref, c_ref, ge_ref, s_ref, acc_ref, *, lc, t, hv):
    qh = pl.program_id(1)
    c64 = DN_CHUNK
    nc = t // c64
    ncc = lc // c64
    combos = [(sub, d) for sub in range(2) for d in range(2)]

    cpi = _pick(nc, (6, 4, 3, 2))

    def chunk_of(d, n):
        return n if d == 0 else jnp.where(n < ncc, ncc - 1 - n, nc - 1 - (n - ncc))

    def phase_b_step(n):
        cs = [chunk_of(d, n) for _, d in combos]
        ss = [s_ref[ci] for ci in range(4)]
        sb = [s.astype(BF16) for s in ss]
        m1 = [jnp.dot(a_ref[ci, cs[ci]], sb[ci], preferred_element_type=F32) for ci in range(4)]
        m2 = [jnp.dot(c_ref[ci, pl.ds(pl.multiple_of(cs[ci] * c64, c64), c64), :], sb[ci],
                      preferred_element_type=F32) for ci in range(4)]
        for ci, (sub, d) in enumerate(combos):
            s_ref[ci] = ss[ci] * ge_ref[ci, pl.ds(cs[ci], 1), :][:, 0:1] - m1[ci] + b_ref[ci, cs[ci]]
            acc_ref[d, pl.ds(pl.multiple_of(cs[ci] * c64, c64), c64), sub * HEAD_DIM:(sub + 1) * HEAD_DIM] += m2[ci]

    def phase_a(it, between=()):
        prepped, where = [], []
        for j in range(cpi):
            for d in range(2):
                c = chunk_of(d, it * cpi + j)
                r0 = pl.multiple_of(c * c64, c64)
                rows = pl.ds(r0, c64)
                qc, kc = q_ref[0, rows, :], k_ref[0, rows, :]
                gram = _bdot_nt(jnp.concatenate([kc, qc], axis=0), kc)
                kk, qk_raw = gram[:c64], gram[c64:]
                for sub in range(2):
                    head = qh * 2 + sub
                    beta_r = gt_ref[0, d * hv + head, pl.ds(c, 1), :]
                    gc_r = gt_ref[0, (2 + d) * hv + head, pl.ds(c, 1), :]
                    vc = v_ref[0, rows, sub * HEAD_DIM:(sub + 1) * HEAD_DIM]
                    prepped.append(_dn_chunk_prep(kk, qk_raw, qc, kc, vc, beta_r, gc_r, d == 1))
                    where.append((sub * 2 + d, c, rows, sub, d))
        sols = _unit_triangular_solve([p[0] for p in prepped], [p[1] for p in prepped], between)
        kdx = [_bdot_tn(p[4], x) for p, x in zip(prepped, sols)]
        qkx = [_bdot(p[2], x) for p, x in zip(prepped, sols)]
        for (ci, c, rows, sub, d), kx, qx, (_, _, _, qd, _, ge) in zip(where, kdx, qkx, prepped):
            b_ref[ci, c] = kx[:, :HEAD_DIM]
            a_ref[ci, c] = kx[:, HEAD_DIM:].astype(BF16)
            c_ref[ci, rows, :] = (qd - qx[:, HEAD_DIM:]).astype(BF16)
            acc_ref[d, rows, sub * HEAD_DIM:(sub + 1) * HEAD_DIM] = qx[:, :HEAD_DIM]
            ge_ref[ci, pl.ds(c, 1), :] = jnp.broadcast_to(ge, (1, HEAD_DIM))

    s_ref[...] = jnp.zeros_like(s_ref)
    phase_a(0)

    def pipelined(it, carry):
        phase_a(it, [functools.partial(phase_b_step, (it - 1) * cpi + j) for j in range(cpi)])
        return carry

    lax.fori_loop(1, nc // cpi, pipelined, 0)
    for j in range(cpi):
        phase_b_step(nc - cpi + j)

    for sub in range(2):
        cols = slice(sub * HEAD_DIM, (sub + 1) * HEAD_DIM)
        o = acc_ref[0, :, cols] + acc_ref[1, :, cols]
        o = o * lax.rsqrt(jnp.mean(o * o, axis=-1, keepdims=True) + NORM_EPS) * ng_ref[...]
        o_ref[0, :, cols] = (o * _silu(z_ref[0, :, cols])).astype(o_ref.dtype)


def _dn_core(qkv, pz, gt, norm_g, lc, n_qk_heads, z_col0):
    b, t, _ = qkv.shape
    hv = 2 * n_qk_heads
    vw = 2 * HEAD_DIM
    nc = t // DN_CHUNK
    return pl.pallas_call(
        functools.partial(_dn_core_kernel, lc=lc, t=t, hv=hv),
        grid=(b, n_qk_heads),
        in_specs=[pl.BlockSpec((1, t, HEAD_DIM), lambda bi, h: (bi, 0, h)),
                  pl.BlockSpec((1, t, HEAD_DIM), lambda bi, h: (bi, 0, n_qk_heads + h)),
                  pl.BlockSpec((1, t, vw), lambda bi, h: (bi, 0, n_qk_heads + h)),
                  pl.BlockSpec((1, t, vw), lambda bi, h: (bi, 0, z_col0 // vw + h)),
                  pl.BlockSpec((1, 4 * hv, nc, DN_CHUNK), lambda bi, h: (bi, 0, 0, 0)),
                  pl.BlockSpec((1, HEAD_DIM), lambda bi, h: (0, 0))],
        out_specs=pl.BlockSpec((1, t, vw), lambda bi, h: (bi, 0, h)),
        out_shape=jax.ShapeDtypeStruct((b, t, hv * HEAD_DIM), BF16),
        scratch_shapes=[pltpu.VMEM((4, nc, HEAD_DIM, HEAD_DIM), BF16),
                        pltpu.VMEM((4, nc, HEAD_DIM, HEAD_DIM), F32),
                        pltpu.VMEM((4, t, HEAD_DIM), BF16),
                        pltpu.VMEM((4, -(-nc // 8) * 8, HEAD_DIM), F32),
                        pltpu.VMEM((4, HEAD_DIM, HEAD_DIM), F32),
                        pltpu.VMEM((2, t, vw), F32)],
        compiler_params=_cparams("parallel", "parallel"),
        name="dn_core",
    )(qkv, qkv, qkv, pz, gt, norm_g)


def _hg_kernel(q_ref, i_ref, gate_ref, ff_ref, fb_ref, lb_ref, ng_ref, o_ref,
               lf_ref, k_ref, st_ref, acc_ref, *, lc, t):
    cs, c16 = HG_SUPER, HG_CHUNK
    nsub = cs // c16
    nc = t // cs
    ncc = lc // cs
    ii = lax.broadcasted_iota(jnp.int32, (cs, cs), 0)
    jj = lax.broadcasted_iota(jnp.int32, (cs, cs), 1)
    krow = lax.broadcasted_iota(jnp.int32, (cs, HEAD_DIM), 0)
    incl = [ii >= jj, ii <= jj]
    sums = [jnp.concatenate([m.astype(BF16), (m & (ii // c16 == jj // c16)).astype(BF16),
                             jnp.ones((8, cs), BF16)], axis=0) for m in incl]
    for d in range(2):
        f_ref = ff_ref if d == 0 else fb_ref
        lb = lb_ref[d:d + 1, :]
        fg = lb + (1.0 - lb) * jax.nn.sigmoid(f_ref[0])
        k_ref[d] = 1.0 - fg
        lf_ref[d] = jnp.log(fg)
    st_ref[...] = jnp.zeros_like(st_ref)
    per_it = _pick(nc, (4, 3, 2))

    def body(it, carry):
        jobs = []
        for u in range(per_it):
            n = it * per_it + u
            jobs.append((0, pl.multiple_of(n * cs, cs)))
            jobs.append((1, pl.multiple_of(jnp.where(n < ncc, ncc - 1 - n, nc - 1 - (n - ncc)) * cs, cs)))
        lfs = [lf_ref[d, pl.ds(r0, cs), :] for d, r0 in jobs]
        kcs = [k_ref[d, pl.ds(r0, cs), :] for d, r0 in jobs]
        qcs = [q_ref[0, pl.ds(r0, cs), :] for d, r0 in jobs]
        vvs = [i_ref[0, pl.ds(r0, cs), :].astype(BF16) for d, r0 in jobs]
        scans = []
        for (d, _), lf in zip(jobs, lfs):
            hi = lf.astype(BF16)
            r1 = lf - hi.astype(F32)
            mid = r1.astype(BF16)
            lo = (r1 - mid.astype(F32)).astype(BF16)
            sc = jnp.dot(sums[d], jnp.concatenate([hi, mid, lo], axis=1), preferred_element_type=F32)
            scans.append(sc[:, :HEAD_DIM] + sc[:, HEAD_DIM:2 * HEAD_DIM] + sc[:, 2 * HEAD_DIM:])
        gcs = [sc[:cs] for sc in scans]
        tots = [sc[2 * cs:2 * cs + 1] for sc in scans]
        qas = [q * jnp.exp(sc[cs:2 * cs]) for q, sc in zip(qcs, scans)]
        qds = [q * jnp.exp(g) for q, g in zip(qcs, gcs)]
        kds = [k * jnp.exp(tt - g) for k, tt, g in zip(kcs, tots, gcs)]
        upd = [_bdot_tn(v, kd) for v, kd in zip(vvs, kds)]
        scaled = []
        for (d, _), gc, kc in zip(jobs, gcs, kcs):
            for a in range(nsub):
                if d == 0:
                    ref = gc[a * c16 - 1:a * c16, :] if a > 0 else None
                    allowed = krow < (a + 1) * c16
                else:
                    ref = gc[(a + 1) * c16:(a + 1) * c16 + 1, :] if a < nsub - 1 else None
                    allowed = krow >= a * c16
                expo = -gc if ref is None else ref - gc
                scaled.append(kc * jnp.exp(jnp.where(allowed, expo, -jnp.inf)))
        parts = [_bdot_nt(qas[j][a * c16:(a + 1) * c16], scaled[j * nsub + a])
                 for j in range(len(jobs)) for a in range(nsub)]
        ps = [jnp.where(incl[d], jnp.concatenate(parts[j * nsub:(j + 1) * nsub], axis=0), 0.0)
              for j, (d, _) in enumerate(jobs)]
        intra = [_bdot(p, v) for p, v in zip(ps, vvs)]
        sts = [st_ref[d] for d in range(2)]
        for j, (d, r0) in enumerate(jobs):
            acc_ref[d, pl.ds(r0, cs), :] = intra[j] + _bdot_nt(qds[j], sts[d])
            sts[d] = sts[d] * jnp.exp(tots[j]) + upd[j]
        for d in range(2):
            st_ref[d] = sts[d]
        return carry

    lax.fori_loop(0, nc // per_it, body, 0)
    o = acc_ref[0] + acc_ref[1]
    o = o * lax.rsqrt(jnp.mean(o * o, axis=-1, keepdims=True) + NORM_EPS) * ng_ref[...]
    o_ref[0] = (o * jax.nn.sigmoid(gate_ref[0])).astype(o_ref.dtype)


def _hgrn2(p, lb, norm_g, lc, n_heads):
    b, t, _ = p.shape
    assert lc % HG_SUPER == 0 and t % HG_SUPER == 0
    blk = lambda k: pl.BlockSpec((1, t, HEAD_DIM), lambda bi, h, k=k: (bi, 0, k * n_heads + h))
    seq = lambda dt: pltpu.VMEM((2, t, HEAD_DIM), dt)
    return pl.pallas_call(
        functools.partial(_hg_kernel, lc=lc, t=t),
        grid=(b, n_heads),
        in_specs=[blk(0), blk(1), blk(2), blk(3), blk(4),
                  pl.BlockSpec((2, HEAD_DIM), lambda bi, h: (0, h)),
                  pl.BlockSpec((1, HEAD_DIM), lambda bi, h: (0, 0))],
        out_specs=pl.BlockSpec((1, t, HEAD_DIM), lambda bi, h: (bi, 0, h)),
        out_shape=jax.ShapeDtypeStruct((b, t, n_heads * HEAD_DIM), BF16),
        scratch_shapes=[seq(F32), seq(F32), pltpu.VMEM((2, HEAD_DIM, HEAD_DIM), F32), seq(F32)],
        compiler_params=_cparams("parallel", "parallel"),
        name="hgrn2",
    )(p, p, p, p, p, lb, norm_g)


def _split_bf16(x):
    hi = x.astype(BF16)
    return hi, (x - hi.astype(F32)).astype(BF16)


def _router_kernel(h_ref, w_ref, b_ref, e_ref, g_ref):
    h_hi, h_lo = _split_bf16(h_ref[...])
    w_hi, w_lo = _split_bf16(w_ref[...])
    nt = (((1,), (1,)), ((), ()))
    logits = (lax.dot_general(w_hi, h_hi, nt, preferred_element_type=F32)
              + lax.dot_general(w_hi, h_lo, nt, preferred_element_type=F32)
              + lax.dot_general(w_lo, h_hi, nt, preferred_element_type=F32))
    scores = jax.nn.sigmoid(logits)
    sel = scores + b_ref[...]
    epg = EXPERTS_PER_GROUP
    rows = lambda a, e: a[e:e + 1, :]
    best_gs = None
    for g in range(N_GROUPS):
        gs = None
        for a in range(epg):
            for c in range(a + 1, epg):
                pair = rows(sel, g * epg + a) + rows(sel, g * epg + c)
                gs = pair if gs is None else jnp.maximum(gs, pair)
        if best_gs is None:
            best_gs, group = gs, jnp.zeros(gs.shape, jnp.int32)
        else:
            upd = gs > best_gs
            best_gs = jnp.where(upd, gs, best_gs)
            group = jnp.where(upd, g, group)

    def in_group(a, l):
        v = rows(a, l)
        for g in range(1, N_GROUPS):
            v = jnp.where(group == g, rows(a, g * epg + l), v)
        return v

    sel_l = [in_group(sel, l) for l in range(epg)]
    sc_l = [in_group(scores, l) for l in range(epg)]

    def top1(exclude):
        best, idx, sc = None, None, None
        for l in range(epg):
            v = sel_l[l] if exclude is None else jnp.where(exclude == l, -jnp.inf, sel_l[l])
            if best is None:
                best, idx, sc = v, jnp.zeros(v.shape, jnp.int32), sc_l[l]
            else:
                upd = v > best
                best = jnp.where(upd, v, best)
                idx = jnp.where(upd, l, idx)
                sc = jnp.where(upd, sc_l[l], sc)
        return idx, sc

    l0, s0 = top1(None)
    l1, s1 = top1(l0)
    e_ref[0:1, :] = group * epg + l0
    e_ref[1:2, :] = group * epg + l1
    den = s0 + s1
    g_ref[0:1, :] = s0 / den
    g_ref[1:2, :] = s1 / den


def _router(h, router_w, router_b):
    m, d = h.shape
    tm = _pick(m, (512, 256, 128))
    ne = router_w.shape[1]
    return pl.pallas_call(
        _router_kernel,
        grid=(m // tm,),
        in_specs=[pl.BlockSpec((tm, d), lambda i: (i, 0)),
                  pl.BlockSpec((ne, d), lambda i: (0, 0)),
                  pl.BlockSpec((ne, 1), lambda i: (0, 0))],
        out_specs=[pl.BlockSpec((TOP_K, tm), lambda i: (0, i)),
                   pl.BlockSpec((TOP_K, tm), lambda i: (0, i))],
        out_shape=[jax.ShapeDtypeStruct((TOP_K, m), jnp.int32), jax.ShapeDtypeStruct((TOP_K, m), F32)],
        compiler_params=_cparams("parallel"),
        name="router",
    )(h, router_w.T, router_b.reshape(ne, 1))


def _moe_ffn_kernel(bexp_ref, code_ref, h_hbm, wg_ref, wu_ref, wd_ref, f_hbm,
                    x0, x1, y0, y1, sem_in, sem_out, *, n_tok):
    i = pl.program_id(0)
    last = pl.num_programs(0) - 1
    eb = EXPERT_BLOCK
    xbufs, ybufs = (x0, x1), (y0, y1)

    def gather(blk, sl, r):
        tok = code_ref[(blk + 1) * eb + r] & (CODE_K_UNIT - 1)
        return pltpu.make_async_copy(h_hbm.at[pl.ds(tok, 1)], xbufs[sl].at[pl.ds(r, 1)], sem_in.at[sl])

    def scatter(blk, sl, r):
        code = code_ref[(blk + 1) * eb + r]
        dst = lax.shift_right_logical(code, CODE_K_SHIFT) * n_tok + (code & (CODE_K_UNIT - 1))
        return pltpu.make_async_copy(ybufs[sl].at[pl.ds(r, 1)], f_hbm.at[pl.ds(dst, 1)], sem_out.at[sl])

    def for_rows(fn):
        def body(r, carry):
            fn(r)
            return carry
        lax.fori_loop(0, eb, body, 0, unroll=8)

    @pl.when(i == 0)
    def _():
        for sl in range(2):
            ybufs[sl][...] = jnp.zeros_like(ybufs[sl])
            spare = pltpu.make_async_copy(ybufs[sl], f_hbm.at[pl.ds(TOP_K * n_tok + sl * eb, eb)],
                                          sem_out.at[sl])
            spare.start()
            spare.wait()
        for_rows(lambda r: gather(0, 0, r).start())

    def step(sl):
        @pl.when(i >= 1)
        def _():
            for_rows(lambda r: scatter(i - 2, sl, r).wait())
        for_rows(lambda r: gather(i, sl, r).wait())
        x = xbufs[sl][...].astype(BF16)
        for r in range(eb):
            gather(i + 1, 1 - sl, r).start()
        a = _silu(jnp.dot(x, wg_ref[0, 0], preferred_element_type=F32)) \
            * jnp.dot(x, wu_ref[0, 0], preferred_element_type=F32)
        for r in range(eb):
            scatter(i - 1, 1 - sl, r).start()
        ybufs[sl][...] = jnp.dot(a.astype(BF16), wd_ref[0, 0], preferred_element_type=F32)

    @pl.when(i % 2 == 0)
    def _():
        step(0)

    @pl.when(i % 2 == 1)
    def _():
        step(1)

    @pl.when(i == last)
    def _():
        for sl in range(2):
            @pl.when(i % 2 == sl)
            def _():
                for_rows(lambda r: scatter(i - 1, 1 - sl, r).wait())
                for_rows(lambda r: gather(i + 1, 1 - sl, r).wait())


def _moe_ffn(h, block_expert, code_buf, wg, wu, wd, layer):
    m, d = h.shape
    de = wg.shape[3]
    nb = block_expert.shape[0]
    wmap = lambda i, be, cb: (layer, be[jnp.minimum(i, nb - 1)], 0, 0)
    buf = pltpu.VMEM((EXPERT_BLOCK, d), F32)
    grid_spec = pltpu.PrefetchScalarGridSpec(
        num_scalar_prefetch=2,
        grid=(nb + 1,),
        in_specs=[pl.BlockSpec(memory_space=pl.ANY),
                  pl.BlockSpec((1, 1, d, de), wmap),
                  pl.BlockSpec((1, 1, d, de), wmap),
                  pl.BlockSpec((1, 1, de, d), wmap)],
        out_specs=pl.BlockSpec(memory_space=pl.ANY),
        scratch_shapes=[buf, buf, buf, buf, pltpu.SemaphoreType.DMA((2,)), pltpu.SemaphoreType.DMA((2,))],
    )
    return pl.pallas_call(
        functools.partial(_moe_ffn_kernel, n_tok=m),
        grid_spec=grid_spec,
        out_shape=jax.ShapeDtypeStruct((TOP_K * m + 2 * EXPERT_BLOCK, d), F32),
        compiler_params=_cparams("arbitrary"),
        name="moe_ffn",
    )(block_expert, code_buf, h, wg, wu, wd)


def _moe_dispatch(expert):
    m = expert.shape[1]
    n_pairs = m * TOP_K
    e_flat = expert.T.reshape(-1)
    onehot = (e_flat[:, None] == jnp.arange(N_EXPERTS)[None, :]).astype(jnp.int32)
    csum = jnp.cumsum(onehot, axis=0)
    rank = jnp.take_along_axis(csum, e_flat[:, None], axis=1)[:, 0] - 1
    counts = csum[-1]
    padded = (counts + EXPERT_BLOCK - 1) // EXPERT_BLOCK * EXPERT_BLOCK
    pend = jnp.cumsum(padded)
    pstart = pend - padded
    dest = pstart[e_flat] + rank
    n_blocks = -(-(n_pairs + N_EXPERTS * (EXPERT_BLOCK - 1)) // EXPERT_BLOCK)
    n_rows = n_blocks * EXPERT_BLOCK
    assert 2 * EXPERT_BLOCK <= m <= CODE_K_UNIT
    pair = jnp.arange(n_pairs, dtype=jnp.int32)
    row = jnp.arange(n_rows + 3 * EXPERT_BLOCK, dtype=jnp.int32)
    pad_code = TOP_K * CODE_K_UNIT + row % (2 * EXPERT_BLOCK)
    code_buf = pad_code.at[dest + EXPERT_BLOCK].set((pair % TOP_K) * CODE_K_UNIT + pair // TOP_K)
    blk_start = jnp.arange(n_blocks) * EXPERT_BLOCK
    block_expert = jnp.minimum(jnp.searchsorted(pend, blk_start, side='right'), N_EXPERTS - 1).astype(jnp.int32)
    return block_expert, code_buf


def _moe(h2, router_w, router_b, wg, wu, wd, layer):
    expert, gate = _router(h2, router_w, router_b)
    block_expert, code_buf = _moe_dispatch(expert)
    f = _moe_ffn(h2, block_expert, code_buf, wg, wu, wd, layer)
    return f, jnp.pad(gate.T, ((0, 0), (0, HEAD_DIM - TOP_K)))


def kernel(x, c, ctx, c_ctx, ada_w, ada_b, ln_g, ln_b, attn_w_in, attn_q_g, attn_k_g, attn_w_out, dn_w_in, dn_conv, dn_a_log, dn_dt_bias, dn_norm_g, dn_w_out, hg_w_in, hg_lb, hg_norm_g, hg_w_out, router_w, router_b, moe_w_gate, moe_w_up, moe_w_down):
    b, l, d = x.shape
    lc = ctx.shape[1]
    t = lc + l
    m = b * t
    depth = ada_w.shape[0]
    alpha = (2.0 * depth) ** 0.25
    tm = _pick(math.gcd(lc, l), (256, 128, 64))
    tiles_per_seq, ctx_tiles = t // tm, lc // tm
    n_heads = d // HEAD_DIM

    mp = -(-(b + 1) // 8) * 8
    cc = jnp.concatenate([c, c_ctx[None, :], jnp.zeros((mp - b - 1, d), F32)], axis=0)
    mod = _ada_mod(cc, ada_w, ada_b)
    mod_l = mod[:, :b].reshape(depth, b, 1, N_MOD, d)
    mod_c = jnp.broadcast_to(mod[:, b].reshape(depth, 1, 1, N_MOD, d), (depth, b, 1, N_MOD, d))
    modtab = jnp.concatenate([mod_c, mod_l], axis=2).reshape(depth, 2 * b, N_MOD, d)

    lb_cum = jnp.cumsum(jax.nn.softmax(hg_lb.astype(F32), axis=1), axis=1)
    lb_cum = lb_cum - lb_cum[:, :1]

    xs, h = _modulate(ctx, x, modtab[0], tm, tiles_per_seq, ctx_tiles, 0, 1, BF16)
    cos, sin = _rope_tables(lc, l)
    rows = (tm, tiles_per_seq, ctx_tiles)

    moe_wg, moe_wu, moe_wd = moe_w_gate.astype(BF16), moe_w_up.astype(BF16), moe_w_down.astype(BF16)

    for i in range(depth):
        kind, j = i % 3, i // 3
        if kind == 0:
            p = _proj(h, attn_w_in[j], 0, attn_w_in.shape[2])
            o = _attention(p.reshape(b, t, -1), attn_q_g[j][None, :], attn_k_g[j][None, :], cos, sin, lc, tm)
            w_out = attn_w_out[j]
        elif kind == 1:
            n_conv = dn_conv.shape[2]
            hv = dn_a_log.shape[2]
            n_main = n_conv + hv * HEAD_DIM
            p = _proj(h, dn_w_in[j], 0, n_main).reshape(b, t, n_main)
            ba = _proj(h, dn_w_in[j], n_main, 4 * hv).reshape(b, t, 4 * hv)
            qkv = _dn_prep(p, dn_conv[j], lc, hv // 2)
            gates = _dn_gates(ba, dn_a_log[j], dn_dt_bias[j])
            gt = jnp.swapaxes(gates, 1, 2).reshape(b, 4 * hv, t // DN_CHUNK, DN_CHUNK)
            o = _dn_core(qkv, p, gt, dn_norm_g[j][None, :], lc, hv // 2, n_conv)
            w_out = dn_w_out[j]
        else:
            p = _proj(h, hg_w_in[j], 0, hg_w_in.shape[2]).reshape(b, t, -1)
            o = _hgrn2(p, lb_cum[:, i], hg_norm_g[j][None, :], lc, n_heads)
            w_out = hg_w_out[j]
        xs, h2 = _out_ln(o.reshape(m, -1), w_out.astype(BF16), xs, modtab[i], modtab[i],
                         ln_g[i, 0][None, :], ln_b[i, 0][None, :], *rows, alpha, 2, 3, 4, F32)
        f, rg = _moe(h2, router_w, router_b, moe_wg, moe_wu, moe_wd, i)
        nxt = modtab[min(i + 1, depth - 1)]
        xs, h = _comb_ln(f, rg, xs, modtab[i], nxt, ln_g[i, 1][None, :], ln_b[i, 1][None, :],
                         *rows, alpha, 5, 0, 1, BF16, latent_only=(i == depth - 1))
    return xs.reshape(b, l, d)
```

```python
import functools
import math

import jax
import jax.numpy as jnp
from jax import lax
from jax.experimental import pallas as pl
from jax.experimental.pallas import tpu as pltpu

F32 = jnp.float32
BF16 = jnp.bfloat16

NORM_EPS = 1e-6
HEAD_DIM = 128
GRID_W = 64
ROPE_BASE = 10000.0
ROPE_FREQS = HEAD_DIM // 4
ATTN_GROUP = 4
DN_CONV_W = 5
DN_CHUNK = 64
HG_CHUNK = 16
HG_SUPER = 64
N_EXPERTS = 16
N_GROUPS = 4
EXPERTS_PER_GROUP = N_EXPERTS // N_GROUPS
TOP_K = 2
EXPERT_BLOCK = 256
CODE_K_SHIFT = 16
CODE_K_UNIT = 1 << CODE_K_SHIFT
N_MOD = 6
VMEM_LIMIT = 56 * 1024 * 1024


def _cparams(*sem):
    return pltpu.CompilerParams(dimension_semantics=sem, vmem_limit_bytes=VMEM_LIMIT)


def _pick(n, prefs):
    for p in prefs:
        if n % p == 0:
            return p
    return n


def _bdot(a, b):
    return jnp.dot(a.astype(BF16), b.astype(BF16), preferred_element_type=F32)


def _bdot_nt(a, b):
    return lax.dot_general(a.astype(BF16), b.astype(BF16), (((1,), (1,)), ((), ())),
                           preferred_element_type=F32)


def _bdot_tn(a, b):
    return lax.dot_general(a.astype(BF16), b.astype(BF16), (((0,), (0,)), ((), ())),
                           preferred_element_type=F32)


def _silu(x):
    return x * jax.nn.sigmoid(x)


def _softplus(x):
    return jnp.maximum(x, 0.0) + jnp.log1p(jnp.exp(-jnp.abs(x)))


def _row_to_col(row, n):
    eye = lax.broadcasted_iota(jnp.int32, (n, n), 0) == lax.broadcasted_iota(jnp.int32, (n, n), 1)
    return jnp.sum(jnp.where(eye, row, 0.0), axis=1, keepdims=True)


def _ada_kernel(cc_ref, w_ref, b_ref, o_ref):
    a = _silu(cc_ref[...])
    o_ref[0] = _bdot(a, w_ref[0]) + b_ref[0]


def _ada_mod(cc, ada_w, ada_b):
    depth, d, n = ada_w.shape
    mp = cc.shape[0]
    tn = _pick(n, (1024, 512, 256, 128))
    return pl.pallas_call(
        _ada_kernel,
        grid=(depth, n // tn),
        in_specs=[pl.BlockSpec((mp, d), lambda i, j: (0, 0)),
                  pl.BlockSpec((1, d, tn), lambda i, j: (i, 0, j)),
                  pl.BlockSpec((1, 1, tn), lambda i, j: (i, 0, j))],
        out_specs=pl.BlockSpec((1, mp, tn), lambda i, j: (i, 0, j)),
        out_shape=jax.ShapeDtypeStruct((depth, mp, n), F32),
        compiler_params=_cparams("parallel", "parallel"),
        name="ada_mod",
    )(cc, ada_w, ada_b.reshape(depth, 1, n))


def _modulate_kernel(ctx_ref, lat_ref, mod_ref, xs_ref, h_ref, *, shift, scale, tiles_per_seq, ctx_tiles):
    m = mod_ref[0]
    is_ctx = pl.program_id(0) % tiles_per_seq < ctx_tiles
    x = jnp.where(is_ctx, ctx_ref[0], lat_ref[0])
    xs_ref[...] = x
    h_ref[...] = (x * (1.0 + m[scale:scale + 1]) + m[shift:shift + 1]).astype(h_ref.dtype)


def _mod_index(tiles_per_seq, ctx_tiles):
    def index(i):
        return ((i // tiles_per_seq) * 2 + jnp.where(i % tiles_per_seq < ctx_tiles, 0, 1), 0, 0)
    return index


def _modulate(ctx, x, modtab, tm, tiles_per_seq, ctx_tiles, shift, scale, dtype):
    b, _, d = x.shape
    midx = _mod_index(tiles_per_seq, ctx_tiles)
    m = b * tiles_per_seq * tm
    return pl.pallas_call(
        functools.partial(_modulate_kernel, shift=shift, scale=scale, tiles_per_seq=tiles_per_seq,
                          ctx_tiles=ctx_tiles),
        grid=(m // tm,),
        in_specs=[pl.BlockSpec((1, tm, d), lambda i: (i // tiles_per_seq,
                                                      jnp.minimum(i % tiles_per_seq, ctx_tiles - 1), 0)),
                  pl.BlockSpec((1, tm, d), lambda i: (i // tiles_per_seq,
                                                      jnp.maximum(i % tiles_per_seq - ctx_tiles, 0), 0)),
                  pl.BlockSpec((1, N_MOD, d), midx)],
        out_specs=[pl.BlockSpec((tm, d), lambda i: (i, 0)), pl.BlockSpec((tm, d), lambda i: (i, 0))],
        out_shape=[jax.ShapeDtypeStruct((m, d), F32), jax.ShapeDtypeStruct((m, d), dtype)],
        compiler_params=_cparams("parallel"),
        name="modulate",
    )(ctx, x, modtab)


def _proj_kernel(a_ref, w_ref, o_ref, wb_ref):
    @pl.when(pl.program_id(1) == 0)
    def _():
        wb_ref[...] = w_ref[...].astype(BF16)

    o_ref[...] = jnp.dot(a_ref[...], wb_ref[...], preferred_element_type=F32).astype(o_ref.dtype)


def _proj(a, w, col0, ncols, out_dtype=F32):
    m, k = a.shape
    tn = _pick(ncols, (1024, 512, 256, 128))
    assert col0 % tn == 0
    tm = _pick(m, (1024, 512, 256, 128, 64))
    c0 = col0 // tn
    return pl.pallas_call(
        _proj_kernel,
        grid=(ncols // tn, m // tm),
        in_specs=[pl.BlockSpec((tm, k), lambda j, i: (i, 0)),
                  pl.BlockSpec((k, tn), lambda j, i: (0, c0 + j))],
        out_specs=pl.BlockSpec((tm, tn), lambda j, i: (i, j)),
        out_shape=jax.ShapeDtypeStruct((m, ncols), out_dtype),
        scratch_shapes=[pltpu.VMEM((k, tn), BF16)],
        compiler_params=_cparams("parallel", "arbitrary"),
        name="proj",
    )(a, w)


def _ln_epilogue(y, x, mod, modn, lng, lnb, alpha, gate, shift, scale):
    r = alpha * x + mod[gate:gate + 1] * y
    rc = r - jnp.mean(r, axis=-1, keepdims=True)
    var = jnp.mean(rc * rc, axis=-1, keepdims=True)
    xn = rc * lax.rsqrt(var + NORM_EPS) * lng + lnb
    h = xn * (1.0 + modn[scale:scale + 1]) + modn[shift:shift + 1]
    return xn, h


def _out_ln_kernel(o_ref, w_ref, x_ref, mod_ref, modn_ref, lng_ref, lnb_ref, xn_ref, h_ref, acc_ref,
                   *, alpha, gate, shift, scale, nk):
    kk = pl.program_id(1)
    part = jnp.dot(o_ref[...], w_ref[...], preferred_element_type=F32)

    def finish(y):
        xn, h = _ln_epilogue(y, x_ref[...], mod_ref[0], modn_ref[0], lng_ref[...], lnb_ref[...],
                             alpha, gate, shift, scale)
        xn_ref[...] = xn
        h_ref[...] = h.astype(h_ref.dtype)

    if nk == 1:
        finish(part)
    else:
        @pl.when(kk == 0)
        def _():
            acc_ref[...] = part

        @pl.when((kk > 0) & (kk < nk - 1))
        def _():
            acc_ref[...] += part

        @pl.when(kk == nk - 1)
        def _():
            finish(acc_ref[...] + part)


def _out_ln(o, w, x, mod, modn, lng, lnb, tm, tiles_per_seq, ctx_tiles, alpha, gate, shift, scale, h_dtype):
    m, k = o.shape
    d = w.shape[1]
    tk = _pick(k, (4096, 2048, 1024, 512, 256, 128))
    midx = _mod_index(tiles_per_seq, ctx_tiles)
    return pl.pallas_call(
        functools.partial(_out_ln_kernel, alpha=alpha, gate=gate, shift=shift, scale=scale, nk=k // tk),
        grid=(m // tm, k // tk),
        in_specs=[pl.BlockSpec((tm, tk), lambda i, j: (i, j)),
                  pl.BlockSpec((tk, d), lambda i, j: (j, 0)),
                  pl.BlockSpec((tm, d), lambda i, j: (i, 0)),
                  pl.BlockSpec((1, N_MOD, d), lambda i, j: midx(i)),
                  pl.BlockSpec((1, N_MOD, d), lambda i, j: midx(i)),
                  pl.BlockSpec((1, d), lambda i, j: (0, 0)),
                  pl.BlockSpec((1, d), lambda i, j: (0, 0))],
        out_specs=[pl.BlockSpec((tm, d), lambda i, j: (i, 0)),
                   pl.BlockSpec((tm, d), lambda i, j: (i, 0))],
        out_shape=[jax.ShapeDtypeStruct((m, d), F32), jax.ShapeDtypeStruct((m, d), h_dtype)],
        scratch_shapes=[pltpu.VMEM((tm, d), F32)],
        compiler_params=_cparams("parallel", "arbitrary"),
        name="out_ln",
    )(o, w, x, mod, modn, lng, lnb)


def _comb_ln_kernel(f0_ref, f1_ref, rg_ref, x_ref, mod_ref, modn_ref, lng_ref, lnb_ref, xn_ref, h_ref,
                    *, alpha, gate, shift, scale):
    y = f0_ref[...] * rg_ref[:, 0:1] + f1_ref[...] * rg_ref[:, 1:2]
    xn, h = _ln_epilogue(y, x_ref[...], mod_ref[0], modn_ref[0], lng_ref[...], lnb_ref[...],
                         alpha, gate, shift, scale)
    xn_ref[...] = xn
    h_ref[...] = h.astype(h_ref.dtype)


def _comb_ln(f, rg, x, mod, modn, lng, lnb, tm, tiles_per_seq, ctx_tiles, alpha, gate, shift, scale, h_dtype,
             latent_only=False):
    m, d = x.shape
    midx = _mod_index(tiles_per_seq, ctx_tiles)
    nt = m // tm
    if latent_only:
        lat_tiles = tiles_per_seq - ctx_tiles
        n_out = nt // tiles_per_seq * lat_tiles
        src = lambda i: (i // lat_tiles) * tiles_per_seq + ctx_tiles + i % lat_tiles
    else:
        n_out = nt
        src = lambda i: i
    return pl.pallas_call(
        functools.partial(_comb_ln_kernel, alpha=alpha, gate=gate, shift=shift, scale=scale),
        grid=(n_out,),
        in_specs=[pl.BlockSpec((tm, d), lambda i: (src(i), 0)),
                  pl.BlockSpec((tm, d), lambda i: (nt + src(i), 0)),
                  pl.BlockSpec((tm, HEAD_DIM), lambda i: (src(i), 0)),
                  pl.BlockSpec((tm, d), lambda i: (src(i), 0)),
                  pl.BlockSpec((1, N_MOD, d), lambda i: midx(src(i))),
                  pl.BlockSpec((1, N_MOD, d), lambda i: midx(src(i))),
                  pl.BlockSpec((1, d), lambda i: (0, 0)),
                  pl.BlockSpec((1, d), lambda i: (0, 0))],
        out_specs=[pl.BlockSpec((tm, d), lambda i: (i, 0)),
                   pl.BlockSpec((tm, d), lambda i: (i, 0))],
        out_shape=[jax.ShapeDtypeStruct((n_out * tm, d), F32), jax.ShapeDtypeStruct((n_out * tm, d), h_dtype)],
        compiler_params=_cparams("parallel"),
        name="comb_ln",
    )(f, f, rg, x, mod, modn, lng, lnb)


def _rope_tables(lc, l):
    pos = jnp.arange(l)
    row = (pos // GRID_W).astype(F32)
    col = (pos % GRID_W).astype(F32)
    inv_freq = ROPE_BASE ** (-jnp.arange(ROPE_FREQS, dtype=F32) / ROPE_FREQS)
    ar = row[:, None] * inv_freq
    ac = col[:, None] * inv_freq
    cos = jnp.concatenate([jnp.cos(ar), jnp.cos(ar), jnp.cos(ac), jnp.cos(ac)], axis=1)
    sin = jnp.concatenate([-jnp.sin(ar), jnp.sin(ar), -jnp.sin(ac), jnp.sin(ac)], axis=1)
    cos = jnp.concatenate([jnp.ones((lc, HEAD_DIM), F32), cos], axis=0)
    sin = jnp.concatenate([jnp.zeros((lc, HEAD_DIM), F32), sin], axis=0)
    return cos, sin


def _norm_rope(x, g, cos, sin):
    y = x * lax.rsqrt(jnp.mean(x * x, axis=-1, keepdims=True) + NORM_EPS) * g
    lane = lax.broadcasted_iota(jnp.int32, y.shape, 1)
    partner = jnp.where(lane % (2 * ROPE_FREQS) < ROPE_FREQS,
                        pltpu.roll(y, HEAD_DIM - ROPE_FREQS, 1), pltpu.roll(y, ROPE_FREQS, 1))
    return y * cos + partner * sin


def _attn_kernel(q_ref, k_ref, v_ref, qg_ref, kg_ref, cos_ref, sin_ref, o_ref, kb_ref, vb_ref,
                 *, lc, tq, t):
    j = pl.program_id(2)

    @pl.when(j == 0)
    def _():
        kb_ref[...] = _norm_rope(k_ref[0], kg_ref[...], cos_ref[...], sin_ref[...]).astype(BF16)
        vb_ref[...] = v_ref[0].astype(BF16)

    r0 = pl.multiple_of(j * tq, tq)
    cos = cos_ref[pl.ds(r0, tq), :]
    sin = sin_ref[pl.ds(r0, tq), :]
    scale = HEAD_DIM ** -0.5

    def attend(n_keys):
        def scores(g):
            qh = _norm_rope(q_ref[0, :, g * HEAD_DIM:(g + 1) * HEAD_DIM], qg_ref[...], cos, sin)
            return _bdot_nt(qh, kb_ref[0:n_keys, :])

        s_next = scores(0)
        for g in range(ATTN_GROUP):
            s = s_next
            if g + 1 < ATTN_GROUP:
                s_next = scores(g + 1)
            e = jnp.exp((s - jnp.max(s, axis=-1, keepdims=True)) * scale)
            den = jnp.sum(e, axis=-1, keepdims=True)
            o = jnp.dot(e.astype(BF16), vb_ref[0:n_keys, :], preferred_element_type=F32) / den
            o_ref[0, :, g * HEAD_DIM:(g + 1) * HEAD_DIM] = o.astype(o_ref.dtype)

    @pl.when(j < lc // tq)
    def _():
        attend(lc)

    @pl.when(j >= lc // tq)
    def _():
        attend(t)


def _attention(p, qg, kg, cos, sin, lc, tq):
    b, t, n = p.shape
    hkv = n // HEAD_DIM // (ATTN_GROUP + 2)
    hq = hkv * ATTN_GROUP
    gw = ATTN_GROUP * HEAD_DIM
    return pl.pallas_call(
        functools.partial(_attn_kernel, lc=lc, tq=tq, t=t),
        grid=(b, hkv, t // tq),
        in_specs=[pl.BlockSpec((1, tq, gw), lambda bi, h, j: (bi, j, h)),
                  pl.BlockSpec((1, t, HEAD_DIM), lambda bi, h, j: (bi, 0, hq + h)),
                  pl.BlockSpec((1, t, HEAD_DIM), lambda bi, h, j: (bi, 0, hq + hkv + h)),
                  pl.BlockSpec((1, HEAD_DIM), lambda bi, h, j: (0, 0)),
                  pl.BlockSpec((1, HEAD_DIM), lambda bi, h, j: (0, 0)),
                  pl.BlockSpec((t, HEAD_DIM), lambda bi, h, j: (0, 0)),
                  pl.BlockSpec((t, HEAD_DIM), lambda bi, h, j: (0, 0))],
        out_specs=pl.BlockSpec((1, tq, gw), lambda bi, h, j: (bi, j, h)),
        out_shape=jax.ShapeDtypeStruct((b, t, hq * HEAD_DIM), BF16),
        scratch_shapes=[pltpu.VMEM((t, HEAD_DIM), BF16), pltpu.VMEM((t, HEAD_DIM), BF16)],
        compiler_params=_cparams("parallel", "parallel", "arbitrary"),
        name="attention",
    )(p, p, p, qg, kg, cos, sin)


def _dn_prep_kernel(p_ref, cw_ref, o_ref, *, lc, t, n_qk_tiles):
    c = pl.program_id(1)
    heads = p_ref.shape[2] // HEAD_DIM
    tt = math.gcd(lc, t - lc)
    tt = _pick(tt, (128, 64, 32, 16, 8))
    halo = 8
    zeros = jnp.zeros((halo, HEAD_DIM), F32)
    post = jnp.where(c * heads < n_qk_tiles, HEAD_DIM ** -0.5, 1.0)
    is_qk = c * heads < 2 * n_qk_tiles
    for r0 in range(0, t, tt):
        for hh in range(heads):
            cols = slice(hh * HEAD_DIM, (hh + 1) * HEAD_DIM)
            top = zeros if r0 in (0, lc) else p_ref[0, r0 - halo:r0, cols]
            bot = zeros if r0 + tt in (lc, t) else p_ref[0, r0 + tt:r0 + tt + halo, cols]
            win = jnp.concatenate([top, p_ref[0, r0:r0 + tt, cols], bot], axis=0)
            acc = win[halo:halo + tt] * cw_ref[DN_CONV_W // 2:DN_CONV_W // 2 + 1, cols]
            for jj in range(DN_CONV_W):
                off = jj - DN_CONV_W // 2
                if off != 0:
                    acc = acc + (pltpu.roll(win, (-off) % (tt + 2 * halo), 0)[halo:halo + tt]
                                 * cw_ref[jj:jj + 1, cols])
            y = _silu(acc)
            nrm = lax.rsqrt(jnp.sum(y * y, axis=-1, keepdims=True) + NORM_EPS)
            o_ref[0, r0:r0 + tt, cols] = y * jnp.where(is_qk, nrm * post, 1.0)


def _dn_prep(p, conv_w, lc, n_qk_tiles):
    b, t, _ = p.shape
    nch = conv_w.shape[1]
    cw = _pick(n_qk_tiles, (4, 2, 1)) * HEAD_DIM
    return pl.pallas_call(
        functools.partial(_dn_prep_kernel, lc=lc, t=t, n_qk_tiles=n_qk_tiles),
        grid=(b, nch // cw),
        in_specs=[pl.BlockSpec((1, t, cw), lambda bi, c: (bi, 0, c)),
                  pl.BlockSpec((DN_CONV_W, cw), lambda bi, c: (0, c))],
        out_specs=pl.BlockSpec((1, t, cw), lambda bi, c: (bi, 0, c)),
        out_shape=jax.ShapeDtypeStruct((b, t, nch), F32),
        compiler_params=_cparams("parallel", "parallel"),
        name="dn_prep",
    )(p, conv_w)


def _dn_gates_kernel(ba_ref, alog_ref, dtb_ref, o_ref, *, t, hv):
    x = ba_ref[0]
    lane = lax.broadcasted_iota(jnp.int32, (DN_CHUNK, 4 * hv), 1)
    g = -jnp.exp(alog_ref[...]) * _softplus(x + dtb_ref[...])
    ii = lax.broadcasted_iota(jnp.int32, (DN_CHUNK, DN_CHUNK), 0)
    jj = lax.broadcasted_iota(jnp.int32, (DN_CHUNK, DN_CHUNK), 1)
    lower = (ii >= jj).astype(F32)
    upper = (ii <= jj).astype(F32)
    for c in range(t // DN_CHUNK):
        rows = slice(c * DN_CHUNK, (c + 1) * DN_CHUNK)
        gch = g[rows]
        pre = jnp.dot(lower, gch, precision=lax.Precision.HIGHEST, preferred_element_type=F32)
        suf = jnp.dot(upper, gch, precision=lax.Precision.HIGHEST, preferred_element_type=F32)
        o_ref[0, rows, :] = jnp.where(lane < 2 * hv, jax.nn.sigmoid(x[rows]),
                                      jnp.where(lane < 3 * hv, pre, suf))


def _dn_gates(ba, a_log, dt_bias):
    b, t, n = ba.shape
    hv = n // 4
    zeros = jnp.zeros((1, 2 * hv), F32)
    alog = jnp.concatenate([zeros, a_log.reshape(1, 2 * hv)], axis=1)
    dtb = jnp.concatenate([zeros, dt_bias.reshape(1, 2 * hv)], axis=1)
    return pl.pallas_call(
        functools.partial(_dn_gates_kernel, t=t, hv=hv),
        grid=(b,),
        in_specs=[pl.BlockSpec((1, t, n), lambda bi: (bi, 0, 0)),
                  pl.BlockSpec((1, n), lambda bi: (0, 0)),
                  pl.BlockSpec((1, n), lambda bi: (0, 0))],
        out_specs=pl.BlockSpec((1, t, n), lambda bi: (bi, 0, 0)),
        out_shape=jax.ShapeDtypeStruct((b, t, n), F32),
        compiler_params=_cparams("parallel"),
        name="dn_gates",
    )(ba, alog, dtb)


def _dn_chunk_prep(kk, qk_raw, qc, kc, vc, beta_r, gc_r, reverse):
    c = DN_CHUNK
    ii = lax.broadcasted_iota(jnp.int32, (c, c), 0)
    jj = lax.broadcasted_iota(jnp.int32, (c, c), 1)
    incl = (ii <= jj) if reverse else (ii >= jj)
    strict = (ii < jj) if reverse else (ii > jj)
    beta_c = _row_to_col(beta_r, c)
    gc_c = _row_to_col(gc_r, c)
    g_end = gc_r[:, 0:1] if reverse else gc_r[:, c - 1:c]
    decay = jnp.exp(jnp.where(incl, gc_c - gc_r, -jnp.inf))
    m = jnp.where(strict, beta_c * kk * decay, 0.0)
    eg = jnp.exp(gc_c)
    x = jnp.concatenate([vc * beta_c, kc * (beta_c * eg)], axis=1)
    qk = jnp.where(incl, qk_raw * decay, 0.0)
    return (m, x, qk.astype(BF16), qc * eg, (kc * jnp.exp(g_end - gc_c)).astype(BF16), jnp.exp(g_end))


def _unit_triangular_solve(ms, xs, between=()):
    pws = ms
    ns = [-m for m in ms]
    between = list(between)
    for _ in range(int(math.log2(DN_CHUNK)) - 1):
        pws = [_bdot(p, p) for p in pws]
        ns = [n + p + _bdot(p, n) for p, n in zip(pws, ns)]
        if between:
            between.pop(0)()
    xs = [x + _bdot(n, x) for n, x in zip(ns, xs)]
    for fn in between:
        fn()
    return xs


def _dn_core_kernel(q_ref, k_ref, v_ref, z_ref, gt_ref, ng_ref, o_ref,
                    a_ref, b_ref, c_ref, ge_ref, s_ref, acc_ref, *, lc, t, hv):
    qh = pl.program_id(1)
    c64 = DN_CHUNK
    nc = t // c64
    ncc = lc // c64
    combos = [(sub, d) for sub in range(2) for d in range(2)]

    cpi = _pick(nc, (6, 4, 3, 2))

    def chunk_of(d, n):
        return n if d == 0 else jnp.where(n < ncc, ncc - 1 - n, nc - 1 - (n - ncc))

    def phase_b_step(n):
        cs = [chunk_of(d, n) for _, d in combos]
        ss = [s_ref[ci] for ci in range(4)]
        sb = [s.astype(BF16) for s in ss]
        m1 = [jnp.dot(a_ref[ci, cs[ci]], sb[ci], preferred_element_type=F32) for ci in range(4)]
        m2 = [jnp.dot(c_ref[ci, pl.ds(pl.multiple_of(cs[ci] * c64, c64), c64), :], sb[ci],
                      preferred_element_type=F32) for ci in range(4)]
        for ci, (sub, d) in enumerate(combos):
            s_ref[ci] = ss[ci] * ge_ref[ci, pl.ds(cs[ci], 1), :][:, 0:1] - m1[ci] + b_ref[ci, cs[ci]]
            acc_ref[d, pl.ds(pl.multiple_of(cs[ci] * c64, c64), c64), sub * HEAD_DIM:(sub + 1) * HEAD_DIM] += m2[ci]

    def phase_a(it, between=()):
        prepped, where = [], []
        for j in range(cpi):
            for d in range(2):
                c = chunk_of(d, it * cpi + j)
                r0 = pl.multiple_of(c * c64, c64)
                rows = pl.ds(r0, c64)
                qc, kc = q_ref[0, rows, :], k_ref[0, rows, :]
                gram = _bdot_nt(jnp.concatenate([kc, qc], axis=0), kc)
                kk, qk_raw = gram[:c64], gram[c64:]
                for sub in range(2):
                    head = qh * 2 + sub
                    beta_r = gt_ref[0, d * hv + head, pl.ds(c, 1), :]
                    gc_r = gt_ref[0, (2 + d) * hv + head, pl.ds(c, 1), :]
                    vc = v_ref[0, rows, sub * HEAD_DIM:(sub + 1) * HEAD_DIM]
                    prepped.append(_dn_chunk_prep(kk, qk_raw, qc, kc, vc, beta_r, gc_r, d == 1))
                    where.append((sub * 2 + d, c, rows, sub, d))
        sols = _unit_triangular_solve([p[0] for p in prepped], [p[1] for p in prepped], between)
        kdx = [_bdot_tn(p[4], x) for p, x in zip(prepped, sols)]
        qkx = [_bdot(p[2], x) for p, x in zip(prepped, sols)]
        for (ci, c, rows, sub, d), kx, qx, (_, _, _, qd, _, ge) in zip(where, kdx, qkx, prepped):
            b_ref[ci, c] = kx[:, :HEAD_DIM]
            a_ref[ci, c] = kx[:, HEAD_DIM:].astype(BF16)
            c_ref[ci, rows, :] = (qd - qx[:, HEAD_DIM:]).astype(BF16)
            acc_ref[d, rows, sub * HEAD_DIM:(sub + 1) * HEAD_DIM] = qx[:, :HEAD_DIM]
            ge_ref[ci, pl.ds(c, 1), :] = jnp.broadcast_to(ge, (1, HEAD_DIM))

    s_ref[...] = jnp.zeros_like(s_ref)
    phase_a(0)

    def pipelined(it, carry):
        phase_a(it, [functools.partial(phase_b_step, (it - 1) * cpi + j) for j in range(cpi)])
        return carry

    lax.fori_loop(1, nc // cpi, pipelined, 0)
    for j in range(cpi):
        phase_b_step(nc - cpi + j)

    for sub in range(2):
        cols = slice(sub * HEAD_DIM, (sub + 1) * HEAD_DIM)
        o = acc_ref[0, :, cols] + acc_ref[1, :, cols]
        o = o * lax.rsqrt(jnp.mean(o * o, axis=-1, keepdims=True) + NORM_EPS) * ng_ref[...]
        o_ref[0, :, cols] = (o * _silu(z_ref[0, :, cols])).astype(o_ref.dtype)


def _dn_core(qkv, pz, gt, norm_g, lc, n_qk_heads, z_col0):
    b, t, _ = qkv.shape
    hv = 2 * n_qk_heads
    vw = 2 * HEAD_DIM
    nc = t // DN_CHUNK
    return pl.pallas_call(
        functools.partial(_dn_core_kernel, lc=lc, t=t, hv=hv),
        grid=(b, n_qk_heads),
        in_specs=[pl.BlockSpec((1, t, HEAD_DIM), lambda bi, h: (bi, 0, h)),
                  pl.BlockSpec((1, t, HEAD_DIM), lambda bi, h: (bi, 0, n_qk_heads + h)),
                  pl.BlockSpec((1, t, vw), lambda bi, h: (bi, 0, n_qk_heads + h)),
                  pl.BlockSpec((1, t, vw), lambda bi, h: (bi, 0, z_col0 // vw + h)),
                  pl.BlockSpec((1, 4 * hv, nc, DN_CHUNK), lambda bi, h: (bi, 0, 0, 0)),
                  pl.BlockSpec((1, HEAD_DIM), lambda bi, h: (0, 0))],
        out_specs=pl.BlockSpec((1, t, vw), lambda bi, h: (bi, 0, h)),
        out_shape=jax.ShapeDtypeStruct((b, t, hv * HEAD_DIM), BF16),
        scratch_shapes=[pltpu.VMEM((4, nc, HEAD_DIM, HEAD_DIM), BF16),
                        pltpu.VMEM((4, nc, HEAD_DIM, HEAD_DIM), F32),
                        pltpu.VMEM((4, t, HEAD_DIM), BF16),
                        pltpu.VMEM((4, -(-nc // 8) * 8, HEAD_DIM), F32),
                        pltpu.VMEM((4, HEAD_DIM, HEAD_DIM), F32),
                        pltpu.VMEM((2, t, vw), F32)],
        compiler_params=_cparams("parallel", "parallel"),
        name="dn_core",
    )(qkv, qkv, qkv, pz, gt, norm_g)


def _hg_kernel(q_ref, i_ref, gate_ref, ff_ref, fb_ref, lb_ref, ng_ref, o_ref,
               lf_ref, k_ref, st_ref, acc_ref, *, lc, t):
    cs, c16 = HG_SUPER, HG_CHUNK
    nsub = cs // c16
    nc = t // cs
    ncc = lc // cs
    ii = lax.broadcasted_iota(jnp.int32, (cs, cs), 0)
    jj = lax.broadcasted_iota(jnp.int32, (cs, cs), 1)
    krow = lax.broadcasted_iota(jnp.int32, (cs, HEAD_DIM), 0)
    incl = [ii >= jj, ii <= jj]
    sums = [jnp.concatenate([m.astype(BF16), (m & (ii // c16 == jj // c16)).astype(BF16),
                             jnp.ones((8, cs), BF16)], axis=0) for m in incl]
    for d in range(2):
        f_ref = ff_ref if d == 0 else fb_ref
        lb = lb_ref[d:d + 1, :]
        fg = lb + (1.0 - lb) * jax.nn.sigmoid(f_ref[0])
        k_ref[d] = 1.0 - fg
        lf_ref[d] = jnp.log(fg)
    st_ref[...] = jnp.zeros_like(st_ref)
    per_it = _pick(nc, (4, 3, 2))

    def scan_stage(jobs):
        scans = []
        for d, r0 in jobs:
            lf = lf_ref[d, pl.ds(r0, cs), :]
            hi = lf.astype(BF16)
            r1 = lf - hi.astype(F32)
            mid = r1.astype(BF16)
            lo = (r1 - mid.astype(F32)).astype(BF16)
            sc = jnp.dot(sums[d], jnp.concatenate([hi, mid, lo], axis=1), preferred_element_type=F32)
            scans.append(sc[:, :HEAD_DIM] + sc[:, HEAD_DIM:2 * HEAD_DIM] + sc[:, 2 * HEAD_DIM:])
        return scans

    def score_stage(jobs, scans):
        kcs = [k_ref[d, pl.ds(r0, cs), :] for d, r0 in jobs]
        qcs = [q_ref[0, pl.ds(r0, cs), :] for d, r0 in jobs]
        vvs = [i_ref[0, pl.ds(r0, cs), :].astype(BF16) for d, r0 in jobs]
        gcs = [sc[:cs] for sc in scans]
        tots = [sc[2 * cs:2 * cs + 1] for sc in scans]
        qas = [q * jnp.exp(sc[cs:2 * cs]) for q, sc in zip(qcs, scans)]
        qds = [q * jnp.exp(g) for q, g in zip(qcs, gcs)]
        kds = [k * jnp.exp(tt - g) for k, tt, g in zip(kcs, tots, gcs)]
        upd = [_bdot_tn(v, kd) for v, kd in zip(vvs, kds)]
        scaled = []
        for (d, _), gc, kc in zip(jobs, gcs, kcs):
            for a in range(nsub):
                if d == 0:
                    ref = gc[a * c16 - 1:a * c16, :] if a > 0 else None
                    allowed = krow < (a + 1) * c16
                else:
                    ref = gc[(a + 1) * c16:(a + 1) * c16 + 1, :] if a < nsub - 1 else None
                    allowed = krow >= a * c16
                expo = -gc if ref is None else ref - gc
                scaled.append(kc * jnp.exp(jnp.where(allowed, expo, -jnp.inf)))
        parts = [_bdot_nt(qas[j][a * c16:(a + 1) * c16], scaled[j * nsub + a])
                 for j in range(len(jobs)) for a in range(nsub)]
        return vvs, tots, qds, upd, parts

    def output_stage(jobs, staged, sts):
        vvs, tots, qds, upd, parts = staged
        ps = [jnp.where(incl[d], jnp.concatenate(parts[j * nsub:(j + 1) * nsub], axis=0), 0.0)
              for j, (d, _) in enumerate(jobs)]
        intra = [_bdot(p, v) for p, v in zip(ps, vvs)]
        for j, (d, r0) in enumerate(jobs):
            acc_ref[d, pl.ds(r0, cs), :] = intra[j] + _bdot_nt(qds[j], sts[d])
            sts[d] = sts[d] * jnp.exp(tots[j]) + upd[j]

    def body(it, carry):
        jobs = []
        for u in range(per_it):
            n = it * per_it + u
            jobs.append((0, pl.multiple_of(n * cs, cs)))
            jobs.append((1, pl.multiple_of(jnp.where(n < ncc, ncc - 1 - n, nc - 1 - (n - ncc)) * cs, cs)))
        half = len(jobs) // 2 if len(jobs) >= 4 else len(jobs)
        first, second = jobs[:half], jobs[half:]
        scans1 = scan_stage(first)
        scans2 = scan_stage(second)
        staged1 = score_stage(first, scans1)
        staged2 = score_stage(second, scans2) if second else None
        sts = [st_ref[d] for d in range(2)]
        output_stage(first, staged1, sts)
        if second:
            output_stage(second, staged2, sts)
        for d in range(2):
            st_ref[d] = sts[d]
        return carry

    lax.fori_loop(0, nc // per_it, body, 0)
    o = acc_ref[0] + acc_ref[1]
    o = o * lax.rsqrt(jnp.mean(o * o, axis=-1, keepdims=True) + NORM_EPS) * ng_ref[...]
    o_ref[0] = (o * jax.nn.sigmoid(gate_ref[0])).astype(o_ref.dtype)


def _hgrn2(p, lb, norm_g, lc, n_heads):
    b, t, _ = p.shape
    assert lc % HG_SUPER == 0 and t % HG_SUPER == 0
    blk = lambda k: pl.BlockSpec((1, t, HEAD_DIM), lambda bi, h, k=k: (bi, 0, k * n_heads + h))
    seq = lambda dt: pltpu.VMEM((2, t, HEAD_DIM), dt)
    return pl.pallas_call(
        functools.partial(_hg_kernel, lc=lc, t=t),
        grid=(b, n_heads),
        in_specs=[blk(0), blk(1), blk(2), blk(3), blk(4),
                  pl.BlockSpec((2, HEAD_DIM), lambda bi, h: (0, h)),
                  pl.BlockSpec((1, HEAD_DIM), lambda bi, h: (0, 0))],
        out_specs=pl.BlockSpec((1, t, HEAD_DIM), lambda bi, h: (bi, 0, h)),
        out_shape=jax.ShapeDtypeStruct((b, t, n_heads * HEAD_DIM), BF16),
        scratch_shapes=[seq(F32), seq(F32), pltpu.VMEM((2, HEAD_DIM, HEAD_DIM), F32), seq(F32)],
        compiler_params=_cparams("parallel", "parallel"),
        name="hgrn2",
    )(p, p, p, p, p, lb, norm_g)


def _split_bf16(x):
    hi = x.astype(BF16)
    return hi, (x - hi.astype(F32)).astype(BF16)


def _router_kernel(h_ref, w_ref, b_ref, e_ref, g_ref):
    h_hi, h_lo = _split_bf16(h_ref[...])
    w_hi, w_lo = _split_bf16(w_ref[...])
    nt = (((1,), (1,)), ((), ()))
    logits = (lax.dot_general(w_hi, h_hi, nt, preferred_element_type=F32)
              + lax.dot_general(w_hi, h_lo, nt, preferred_element_type=F32)
              + lax.dot_general(w_lo, h_hi, nt, preferred_element_type=F32))
    scores = jax.nn.sigmoid(logits)
    sel = scores + b_ref[...]
    epg = EXPERTS_PER_GROUP
    rows = lambda a, e: a[e:e + 1, :]
    best_gs = None
    for g in range(N_GROUPS):
        gs = None
        for a in range(epg):
            for c in range(a + 1, epg):
                pair = rows(sel, g * epg + a) + rows(sel, g * epg + c)
                gs = pair if gs is None else jnp.maximum(gs, pair)
        if best_gs is None:
            best_gs, group = gs, jnp.zeros(gs.shape, jnp.int32)
        else:
            upd = gs > best_gs
            best_gs = jnp.where(upd, gs, best_gs)
            group = jnp.where(upd, g, group)

    def in_group(a, l):
        v = rows(a, l)
        for g in range(1, N_GROUPS):
            v = jnp.where(group == g, rows(a, g * epg + l), v)
        return v

    sel_l = [in_group(sel, l) for l in range(epg)]
    sc_l = [in_group(scores, l) for l in range(epg)]

    def top1(exclude):
        best, idx, sc = None, None, None
        for l in range(epg):
            v = sel_l[l] if exclude is None else jnp.where(exclude == l, -jnp.inf, sel_l[l])
            if best is None:
                best, idx, sc = v, jnp.zeros(v.shape, jnp.int32), sc_l[l]
            else:
                upd = v > best
                best = jnp.where(upd, v, best)
                idx = jnp.where(upd, l, idx)
                sc = jnp.where(upd, sc_l[l], sc)
        return idx, sc

    l0, s0 = top1(None)
    l1, s1 = top1(l0)
    e_ref[0:1, :] = group * epg + l0
    e_ref[1:2, :] = group * epg + l1
    den = s0 + s1
    g_ref[0:1, :] = s0 / den
    g_ref[1:2, :] = s1 / den


def _router(h, router_w, router_b):
    m, d = h.shape
    tm = _pick(m, (512, 256, 128))
    ne = router_w.shape[1]
    return pl.pallas_call(
        _router_kernel,
        grid=(m // tm,),
        in_specs=[pl.BlockSpec((tm, d), lambda i: (i, 0)),
                  pl.BlockSpec((ne, d), lambda i: (0, 0)),
                  pl.BlockSpec((ne, 1), lambda i: (0, 0))],
        out_specs=[pl.BlockSpec((TOP_K, tm), lambda i: (0, i)),
                   pl.BlockSpec((TOP_K, tm), lambda i: (0, i))],
        out_shape=[jax.ShapeDtypeStruct((TOP_K, m), jnp.int32), jax.ShapeDtypeStruct((TOP_K, m), F32)],
        compiler_params=_cparams("parallel"),
        name="router",
    )(h, router_w.T, router_b.reshape(ne, 1))


def _moe_ffn_kernel(bexp_ref, code_ref, h_hbm, wg_ref, wu_ref, wd_ref, f_hbm,
                    x0, x1, x2, y0, y1, y2, sem_in, sem_out, *, n_tok):
    i = pl.program_id(0)
    last = pl.num_programs(0) - 1
    eb = EXPERT_BLOCK
    xbufs, ybufs = (x0, x1, x2), (y0, y1, y2)
    ns = len(xbufs)

    def gather(blk, sl, r):
        tok = code_ref[(blk + 1) * eb + r] & (CODE_K_UNIT - 1)
        return pltpu.make_async_copy(h_hbm.at[pl.ds(tok, 1)], xbufs[sl].at[pl.ds(r, 1)], sem_in.at[sl])

    def scatter(blk, sl, r):
        code = code_ref[(blk + 1) * eb + r]
        dst = lax.shift_right_logical(code, CODE_K_SHIFT) * n_tok + (code & (CODE_K_UNIT - 1))
        return pltpu.make_async_copy(ybufs[sl].at[pl.ds(r, 1)], f_hbm.at[pl.ds(dst, 1)], sem_out.at[sl])

    def for_rows(fn):
        def body(r, carry):
            fn(r)
            return carry
        lax.fori_loop(0, eb, body, 0, unroll=8)

    @pl.when(i == 0)
    def _():
        for sl in range(ns):
            ybufs[sl][...] = jnp.zeros_like(ybufs[sl])
        for sl in range(2):
            spare = pltpu.make_async_copy(ybufs[sl], f_hbm.at[pl.ds(TOP_K * n_tok + sl * eb, eb)],
                                          sem_out.at[sl])
            spare.start()
            spare.wait()
        for_rows(lambda r: gather(0, 0, r).start())
        for_rows(lambda r: gather(1, 1, r).start())

    def step(sl):
        ahead, behind = (sl + 2) % ns, (sl + 2) % ns
        @pl.when(i >= 2)
        def _():
            for_rows(lambda r: scatter(i - 3, sl, r).wait())
        for_rows(lambda r: gather(i, sl, r).wait())
        x = xbufs[sl][...].astype(BF16)
        for r in range(eb):
            gather(i + 2, ahead, r).start()
        a = _silu(jnp.dot(x, wg_ref[0, 0], preferred_element_type=F32)) \
            * jnp.dot(x, wu_ref[0, 0], preferred_element_type=F32)
        for r in range(eb):
            scatter(i - 1, behind, r).start()
        ybufs[sl][...] = jnp.dot(a.astype(BF16), wd_ref[0, 0], preferred_element_type=F32)

    for sl in range(ns):
        @pl.when(i % ns == sl)
        def _(sl=sl):
            step(sl)

    @pl.when(i == last)
    def _():
        for sl in range(ns):
            @pl.when(i % ns == sl)
            def _(sl=sl):
                for_rows(lambda r: scatter(i - 2, (sl + 1) % ns, r).wait())
                for_rows(lambda r: scatter(i - 1, (sl + 2) % ns, r).wait())
                for_rows(lambda r: gather(i + 1, (sl + 1) % ns, r).wait())
                for_rows(lambda r: gather(i + 2, (sl + 2) % ns, r).wait())


def _moe_ffn(h, block_expert, code_buf, wg, wu, wd, layer):
    m, d = h.shape
    de = wg.shape[3]
    nb = block_expert.shape[0]
    wmap = lambda i, be, cb: (layer, be[jnp.minimum(i, nb - 1)], 0, 0)
    buf = pltpu.VMEM((EXPERT_BLOCK, d), F32)
    grid_spec = pltpu.PrefetchScalarGridSpec(
        num_scalar_prefetch=2,
        grid=(nb + 1,),
        in_specs=[pl.BlockSpec(memory_space=pl.ANY),
                  pl.BlockSpec((1, 1, d, de), wmap),
                  pl.BlockSpec((1, 1, d, de), wmap),
                  pl.BlockSpec((1, 1, de, d), wmap)],
        out_specs=pl.BlockSpec(memory_space=pl.ANY),
        scratch_shapes=[buf] * 6 + [pltpu.SemaphoreType.DMA((3,)), pltpu.SemaphoreType.DMA((3,))],
    )
    return pl.pallas_call(
        functools.partial(_moe_ffn_kernel, n_tok=m),
        grid_spec=grid_spec,
        out_shape=jax.ShapeDtypeStruct((TOP_K * m + 2 * EXPERT_BLOCK, d), F32),
        compiler_params=_cparams("arbitrary"),
        name="moe_ffn",
    )(block_expert, code_buf, h, wg, wu, wd)


def _moe_dispatch(expert):
    m = expert.shape[1]
    n_pairs = m * TOP_K
    e_flat = expert.T.reshape(-1)
    onehot = (e_flat[:, None] == jnp.arange(N_EXPERTS)[None, :]).astype(jnp.int32)
    csum = jnp.cumsum(onehot, axis=0)
    rank = jnp.take_along_axis(csum, e_flat[:, None], axis=1)[:, 0] - 1
    counts = csum[-1]
    padded = (counts + EXPERT_BLOCK - 1) // EXPERT_BLOCK * EXPERT_BLOCK
    pend = jnp.cumsum(padded)
    pstart = pend - padded
    dest = pstart[e_flat] + rank
    n_blocks = -(-(n_pairs + N_EXPERTS * (EXPERT_BLOCK - 1)) // EXPERT_BLOCK)
    n_rows = n_blocks * EXPERT_BLOCK
    assert 2 * EXPERT_BLOCK <= m <= CODE_K_UNIT
    pair = jnp.arange(n_pairs, dtype=jnp.int32)
    row = jnp.arange(n_rows + 4 * EXPERT_BLOCK, dtype=jnp.int32)
    pad_code = TOP_K * CODE_K_UNIT + row % (2 * EXPERT_BLOCK)
    code_buf = pad_code.at[dest + EXPERT_BLOCK].set((pair % TOP_K) * CODE_K_UNIT + pair // TOP_K)
    blk_start = jnp.arange(n_blocks) * EXPERT_BLOCK
    block_expert = jnp.minimum(jnp.searchsorted(pend, blk_start, side='right'), N_EXPERTS - 1).astype(jnp.int32)
    return block_expert, code_buf


def _moe(h2, router_w, router_b, wg, wu, wd, layer):
    expert, gate = _router(h2, router_w, router_b)
    block_expert, code_buf = _moe_dispatch(expert)
    f = _moe_ffn(h2, block_expert, code_buf, wg, wu, wd, layer)
    return f, jnp.pad(gate.T, ((0, 0), (0, HEAD_DIM - TOP_K)))


def kernel(x, c, ctx, c_ctx, ada_w, ada_b, ln_g, ln_b, attn_w_in, attn_q_g, attn_k_g, attn_w_out, dn_w_in, dn_conv, dn_a_log, dn_dt_bias, dn_norm_g, dn_w_out, hg_w_in, hg_lb, hg_norm_g, hg_w_out, router_w, router_b, moe_w_gate, moe_w_up, moe_w_down):
    b, l, d = x.shape
    lc = ctx.shape[1]
    t = lc + l
    m = b * t
    depth = ada_w.shape[0]
    alpha = (2.0 * depth) ** 0.25
    tm = _pick(math.gcd(lc, l), (256, 128, 64))
    tiles_per_seq, ctx_tiles = t // tm, lc // tm
    n_heads = d // HEAD_DIM

    mp = -(-(b + 1) // 8) * 8
    cc = jnp.concatenate([c, c_ctx[None, :], jnp.zeros((mp - b - 1, d), F32)], axis=0)
    mod = _ada_mod(cc, ada_w, ada_b)
    mod_l = mod[:, :b].reshape(depth, b, 1, N_MOD, d)
    mod_c = jnp.broadcast_to(mod[:, b].reshape(depth, 1, 1, N_MOD, d), (depth, b, 1, N_MOD, d))
    modtab = jnp.concatenate([mod_c, mod_l], axis=2).reshape(depth, 2 * b, N_MOD, d)

    lb_cum = jnp.cumsum(jax.nn.softmax(hg_lb.astype(F32), axis=1), axis=1)
    lb_cum = lb_cum - lb_cum[:, :1]

    xs, h = _modulate(ctx, x, modtab[0], tm, tiles_per_seq, ctx_tiles, 0, 1, BF16)
    cos, sin = _rope_tables(lc, l)
    rows = (tm, tiles_per_seq, ctx_tiles)

    moe_wg, moe_wu, moe_wd = moe_w_gate.astype(BF16), moe_w_up.astype(BF16), moe_w_down.astype(BF16)

    for i in range(depth):
        kind, j = i % 3, i // 3
        if kind == 0:
            p = _proj(h, attn_w_in[j], 0, attn_w_in.shape[2])
            o = _attention(p.reshape(b, t, -1), attn_q_g[j][None, :], attn_k_g[j][None, :], cos, sin, lc, tm)
            w_out = attn_w_out[j]
        elif kind == 1:
            n_conv = dn_conv.shape[2]
            hv = dn_a_log.shape[2]
            n_main = n_conv + hv * HEAD_DIM
            p = _proj(h, dn_w_in[j], 0, n_main).reshape(b, t, n_main)
            ba = _proj(h, dn_w_in[j], n_main, 4 * hv).reshape(b, t, 4 * hv)
            qkv = _dn_prep(p, dn_conv[j], lc, hv // 2)
            gates = _dn_gates(ba, dn_a_log[j], dn_dt_bias[j])
            gt = jnp.swapaxes(gates, 1, 2).reshape(b, 4 * hv, t // DN_CHUNK, DN_CHUNK)
            o = _dn_core(qkv, p, gt, dn_norm_g[j][None, :], lc, hv // 2, n_conv)
            w_out = dn_w_out[j]
        else:
            p = _proj(h, hg_w_in[j], 0, hg_w_in.shape[2]).reshape(b, t, -1)
            o = _hgrn2(p, lb_cum[:, i], hg_norm_g[j][None, :], lc, n_heads)
            w_out = hg_w_out[j]
        xs, h2 = _out_ln(o.reshape(m, -1), w_out.astype(BF16), xs, modtab[i], modtab[i],
                         ln_g[i, 0][None, :], ln_b[i, 0][None, :], *rows, alpha, 2, 3, 4, F32)
        f, rg = _moe(h2, router_w, router_b, moe_wg, moe_wu, moe_wd, i)
        nxt = modtab[min(i + 1, depth - 1)]
        xs, h = _comb_ln(f, rg, xs, modtab[i], nxt, ln_g[i, 1][None, :], ln_b[i, 1][None, :],
                         *rows, alpha, 5, 0, 1, BF16, latent_only=(i == depth - 1))
    return xs.reshape(b, l, d)
```

```python
import functools
import math

import jax
import jax.numpy as jnp
from jax import lax
from jax.experimental import pallas as pl
from jax.experimental.pallas import tpu as pltpu

F32 = jnp.float32
BF16 = jnp.bfloat16

NORM_EPS = 1e-6
HEAD_DIM = 128
GRID_W = 64
ROPE_BASE = 10000.0
ROPE_FREQS = HEAD_DIM // 4
ATTN_GROUP = 4
DN_CONV_W = 5
DN_CHUNK = 64
HG_CHUNK = 16
HG_SUPER = 64
N_EXPERTS = 16
N_GROUPS = 4
EXPERTS_PER_GROUP = N_EXPERTS // N_GROUPS
TOP_K = 2
EXPERT_BLOCK = 256
CODE_K_SHIFT = 16
CODE_K_UNIT = 1 << CODE_K_SHIFT
N_MOD = 6
VMEM_LIMIT = 56 * 1024 * 1024


def _cparams(*sem):
    return pltpu.CompilerParams(dimension_semantics=sem, vmem_limit_bytes=VMEM_LIMIT)


def _pick(n, prefs):
    for p in prefs:
        if n % p == 0:
            return p
    return n


def _bdot(a, b):
    return jnp.dot(a.astype(BF16), b.astype(BF16), preferred_element_type=F32)


def _bdot_nt(a, b):
    return lax.dot_general(a.astype(BF16), b.astype(BF16), (((1,), (1,)), ((), ())),
                           preferred_element_type=F32)


def _bdot_tn(a, b):
    return lax.dot_general(a.astype(BF16), b.astype(BF16), (((0,), (0,)), ((), ())),
                           preferred_element_type=F32)


def _sigmoid(x):
    return 0.5 * (1.0 + jnp.tanh(0.5 * x))


def _silu(x):
    return x * _sigmoid(x)


def _softplus(x):
    return jnp.maximum(x, 0.0) + jnp.log1p(jnp.exp(-jnp.abs(x)))


def _row_to_col(row, n):
    eye = lax.broadcasted_iota(jnp.int32, (n, n), 0) == lax.broadcasted_iota(jnp.int32, (n, n), 1)
    return jnp.sum(jnp.where(eye, row, 0.0), axis=1, keepdims=True)


def _ada_kernel(cc_ref, w_ref, b_ref, o_ref):
    a = _silu(cc_ref[...])
    o_ref[0] = _bdot(a, w_ref[0]) + b_ref[0]


def _ada_mod(cc, ada_w, ada_b):
    depth, d, n = ada_w.shape
    mp = cc.shape[0]
    tn = _pick(n, (1024, 512, 256, 128))
    return pl.pallas_call(
        _ada_kernel,
        grid=(depth, n // tn),
        in_specs=[pl.BlockSpec((mp, d), lambda i, j: (0, 0)),
                  pl.BlockSpec((1, d, tn), lambda i, j: (i, 0, j)),
                  pl.BlockSpec((1, 1, tn), lambda i, j: (i, 0, j))],
        out_specs=pl.BlockSpec((1, mp, tn), lambda i, j: (i, 0, j)),
        out_shape=jax.ShapeDtypeStruct((depth, mp, n), F32),
        compiler_params=_cparams("parallel", "parallel"),
        name="ada_mod",
    )(cc, ada_w, ada_b.reshape(depth, 1, n))


def _modulate_kernel(ctx_ref, lat_ref, mod_ref, xs_ref, h_ref, *, shift, scale, tiles_per_seq, ctx_tiles):
    m = mod_ref[0]
    is_ctx = pl.program_id(0) % tiles_per_seq < ctx_tiles
    x = jnp.where(is_ctx, ctx_ref[0], lat_ref[0])
    xs_ref[...] = x
    h_ref[...] = (x * (1.0 + m[scale:scale + 1]) + m[shift:shift + 1]).astype(h_ref.dtype)


def _mod_index(tiles_per_seq, ctx_tiles):
    def index(i):
        return ((i // tiles_per_seq) * 2 + jnp.where(i % tiles_per_seq < ctx_tiles, 0, 1), 0, 0)
    return index


def _modulate(ctx, x, modtab, tm, tiles_per_seq, ctx_tiles, shift, scale, dtype):
    b, _, d = x.shape
    midx = _mod_index(tiles_per_seq, ctx_tiles)
    m = b * tiles_per_seq * tm
    return pl.pallas_call(
        functools.partial(_modulate_kernel, shift=shift, scale=scale, tiles_per_seq=tiles_per_seq,
                          ctx_tiles=ctx_tiles),
        grid=(m // tm,),
        in_specs=[pl.BlockSpec((1, tm, d), lambda i: (i // tiles_per_seq,
                                                      jnp.minimum(i % tiles_per_seq, ctx_tiles - 1), 0)),
                  pl.BlockSpec((1, tm, d), lambda i: (i // tiles_per_seq,
                                                      jnp.maximum(i % tiles_per_seq - ctx_tiles, 0), 0)),
                  pl.BlockSpec((1, N_MOD, d), midx)],
        out_specs=[pl.BlockSpec((tm, d), lambda i: (i, 0)), pl.BlockSpec((tm, d), lambda i: (i, 0))],
        out_shape=[jax.ShapeDtypeStruct((m, d), F32), jax.ShapeDtypeStruct((m, d), dtype)],
        compiler_params=_cparams("parallel"),
        name="modulate",
    )(ctx, x, modtab)


def _proj_kernel(a_ref, w_ref, o_ref, wb_ref):
    @pl.when(pl.program_id(1) == 0)
    def _():
        wb_ref[...] = w_ref[...].astype(BF16)

    o_ref[...] = jnp.dot(a_ref[...], wb_ref[...], preferred_element_type=F32).astype(o_ref.dtype)


def _proj(a, w, col0, ncols, out_dtype=F32):
    m, k = a.shape
    tn = _pick(ncols, (1024, 512, 256, 128))
    assert col0 % tn == 0
    tm = _pick(m, (1024, 512, 256, 128, 64))
    c0 = col0 // tn
    return pl.pallas_call(
        _proj_kernel,
        grid=(ncols // tn, m // tm),
        in_specs=[pl.BlockSpec((tm, k), lambda j, i: (i, 0)),
                  pl.BlockSpec((k, tn), lambda j, i: (0, c0 + j))],
        out_specs=pl.BlockSpec((tm, tn), lambda j, i: (i, j)),
        out_shape=jax.ShapeDtypeStruct((m, ncols), out_dtype),
        scratch_shapes=[pltpu.VMEM((k, tn), BF16)],
        compiler_params=_cparams("parallel", "arbitrary"),
        name="proj",
    )(a, w)


def _ln_epilogue(y, x, mod, modn, lng, lnb, alpha, gate, shift, scale):
    r = alpha * x + mod[gate:gate + 1] * y
    rc = r - jnp.mean(r, axis=-1, keepdims=True)
    var = jnp.mean(rc * rc, axis=-1, keepdims=True)
    xn = rc * lax.rsqrt(var + NORM_EPS) * lng + lnb
    h = xn * (1.0 + modn[scale:scale + 1]) + modn[shift:shift + 1]
    return xn, h


def _out_ln_kernel(o_ref, w_ref, x_ref, mod_ref, modn_ref, lng_ref, lnb_ref, xn_ref, h_ref, acc_ref,
                   *, alpha, gate, shift, scale, nk):
    kk = pl.program_id(1)
    part = jnp.dot(o_ref[...], w_ref[...], preferred_element_type=F32)

    def finish(y):
        xn, h = _ln_epilogue(y, x_ref[...], mod_ref[0], modn_ref[0], lng_ref[...], lnb_ref[...],
                             alpha, gate, shift, scale)
        xn_ref[...] = xn
        h_ref[...] = h.astype(h_ref.dtype)

    if nk == 1:
        finish(part)
    else:
        @pl.when(kk == 0)
        def _():
            acc_ref[...] = part

        @pl.when((kk > 0) & (kk < nk - 1))
        def _():
            acc_ref[...] += part

        @pl.when(kk == nk - 1)
        def _():
            finish(acc_ref[...] + part)


def _out_ln(o, w, x, mod, modn, lng, lnb, tm, tiles_per_seq, ctx_tiles, alpha, gate, shift, scale, h_dtype):
    m, k = o.shape
    d = w.shape[1]
    tk = _pick(k, (4096, 2048, 1024, 512, 256, 128))
    midx = _mod_index(tiles_per_seq, ctx_tiles)
    return pl.pallas_call(
        functools.partial(_out_ln_kernel, alpha=alpha, gate=gate, shift=shift, scale=scale, nk=k // tk),
        grid=(m // tm, k // tk),
        in_specs=[pl.BlockSpec((tm, tk), lambda i, j: (i, j)),
                  pl.BlockSpec((tk, d), lambda i, j: (j, 0)),
                  pl.BlockSpec((tm, d), lambda i, j: (i, 0)),
                  pl.BlockSpec((1, N_MOD, d), lambda i, j: midx(i)),
                  pl.BlockSpec((1, N_MOD, d), lambda i, j: midx(i)),
                  pl.BlockSpec((1, d), lambda i, j: (0, 0)),
                  pl.BlockSpec((1, d), lambda i, j: (0, 0))],
        out_specs=[pl.BlockSpec((tm, d), lambda i, j: (i, 0)),
                   pl.BlockSpec((tm, d), lambda i, j: (i, 0))],
        out_shape=[jax.ShapeDtypeStruct((m, d), F32), jax.ShapeDtypeStruct((m, d), h_dtype)],
        scratch_shapes=[pltpu.VMEM((tm, d), F32)],
        compiler_params=_cparams("parallel", "arbitrary"),
        name="out_ln",
    )(o, w, x, mod, modn, lng, lnb)


def _comb_ln_kernel(f0_ref, f1_ref, rg_ref, x_ref, mod_ref, modn_ref, lng_ref, lnb_ref, xn_ref, h_ref,
                    *, alpha, gate, shift, scale):
    y = f0_ref[...] * rg_ref[:, 0:1] + f1_ref[...] * rg_ref[:, 1:2]
    xn, h = _ln_epilogue(y, x_ref[...], mod_ref[0], modn_ref[0], lng_ref[...], lnb_ref[...],
                         alpha, gate, shift, scale)
    xn_ref[...] = xn
    h_ref[...] = h.astype(h_ref.dtype)


def _comb_ln(f, rg, x, mod, modn, lng, lnb, tm, tiles_per_seq, ctx_tiles, alpha, gate, shift, scale, h_dtype,
             latent_only=False):
    m, d = x.shape
    midx = _mod_index(tiles_per_seq, ctx_tiles)
    nt = m // tm
    if latent_only:
        lat_tiles = tiles_per_seq - ctx_tiles
        n_out = nt // tiles_per_seq * lat_tiles
        src = lambda i: (i // lat_tiles) * tiles_per_seq + ctx_tiles + i % lat_tiles
    else:
        n_out = nt
        src = lambda i: i
    return pl.pallas_call(
        functools.partial(_comb_ln_kernel, alpha=alpha, gate=gate, shift=shift, scale=scale),
        grid=(n_out,),
        in_specs=[pl.BlockSpec((tm, d), lambda i: (src(i), 0)),
                  pl.BlockSpec((tm, d), lambda i: (nt + src(i), 0)),
                  pl.BlockSpec((tm, HEAD_DIM), lambda i: (src(i), 0)),
                  pl.BlockSpec((tm, d), lambda i: (src(i), 0)),
                  pl.BlockSpec((1, N_MOD, d), lambda i: midx(src(i))),
                  pl.BlockSpec((1, N_MOD, d), lambda i: midx(src(i))),
                  pl.BlockSpec((1, d), lambda i: (0, 0)),
                  pl.BlockSpec((1, d), lambda i: (0, 0))],
        out_specs=[pl.BlockSpec((tm, d), lambda i: (i, 0)),
                   pl.BlockSpec((tm, d), lambda i: (i, 0))],
        out_shape=[jax.ShapeDtypeStruct((n_out * tm, d), F32), jax.ShapeDtypeStruct((n_out * tm, d), h_dtype)],
        compiler_params=_cparams("parallel"),
        name="comb_ln",
    )(f, f, rg, x, mod, modn, lng, lnb)


def _rope_tables(lc, l):
    pos = jnp.arange(l)
    row = (pos // GRID_W).astype(F32)
    col = (pos % GRID_W).astype(F32)
    inv_freq = ROPE_BASE ** (-jnp.arange(ROPE_FREQS, dtype=F32) / ROPE_FREQS)
    ar = row[:, None] * inv_freq
    ac = col[:, None] * inv_freq
    cos = jnp.concatenate([jnp.cos(ar), jnp.cos(ar), jnp.cos(ac), jnp.cos(ac)], axis=1)
    sin = jnp.concatenate([-jnp.sin(ar), jnp.sin(ar), -jnp.sin(ac), jnp.sin(ac)], axis=1)
    cos = jnp.concatenate([jnp.ones((lc, HEAD_DIM), F32), cos], axis=0)
    sin = jnp.concatenate([jnp.zeros((lc, HEAD_DIM), F32), sin], axis=0)
    return cos, sin


def _norm_rope(x, g, cos, sin):
    y = x * lax.rsqrt(jnp.mean(x * x, axis=-1, keepdims=True) + NORM_EPS) * g
    lane = lax.broadcasted_iota(jnp.int32, y.shape, 1)
    partner = jnp.where(lane % (2 * ROPE_FREQS) < ROPE_FREQS,
                        pltpu.roll(y, HEAD_DIM - ROPE_FREQS, 1), pltpu.roll(y, ROPE_FREQS, 1))
    return y * cos + partner * sin


def _attn_kernel(q_ref, k_ref, v_ref, qg_ref, kg_ref, cos_ref, sin_ref, o_ref, kb_ref, vb_ref,
                 *, lc, tq, t):
    j = pl.program_id(2)

    @pl.when(j == 0)
    def _():
        kb_ref[...] = _norm_rope(k_ref[0], kg_ref[...], cos_ref[...], sin_ref[...]).astype(BF16)
        vb_ref[...] = v_ref[0].astype(BF16)

    r0 = pl.multiple_of(j * tq, tq)
    cos = cos_ref[pl.ds(r0, tq), :]
    sin = sin_ref[pl.ds(r0, tq), :]
    scale = HEAD_DIM ** -0.5

    def attend(n_keys):
        def scores(g):
            qh = _norm_rope(q_ref[0, :, g * HEAD_DIM:(g + 1) * HEAD_DIM], qg_ref[...], cos, sin)
            return _bdot_nt(qh, kb_ref[0:n_keys, :])

        s_next = scores(0)
        for g in range(ATTN_GROUP):
            s = s_next
            if g + 1 < ATTN_GROUP:
                s_next = scores(g + 1)
            e = jnp.exp((s - jnp.max(s, axis=-1, keepdims=True)) * scale)
            den = jnp.sum(e, axis=-1, keepdims=True)
            o = jnp.dot(e.astype(BF16), vb_ref[0:n_keys, :], preferred_element_type=F32) / den
            o_ref[0, :, g * HEAD_DIM:(g + 1) * HEAD_DIM] = o.astype(o_ref.dtype)

    @pl.when(j < lc // tq)
    def _():
        attend(lc)

    @pl.when(j >= lc // tq)
    def _():
        attend(t)


def _attention(p, qg, kg, cos, sin, lc, tq):
    b, t, n = p.shape
    hkv = n // HEAD_DIM // (ATTN_GROUP + 2)
    hq = hkv * ATTN_GROUP
    gw = ATTN_GROUP * HEAD_DIM
    return pl.pallas_call(
        functools.partial(_attn_kernel, lc=lc, tq=tq, t=t),
        grid=(b, hkv, t // tq),
        in_specs=[pl.BlockSpec((1, tq, gw), lambda bi, h, j: (bi, j, h)),
                  pl.BlockSpec((1, t, HEAD_DIM), lambda bi, h, j: (bi, 0, hq + h)),
                  pl.BlockSpec((1, t, HEAD_DIM), lambda bi, h, j: (bi, 0, hq + hkv + h)),
                  pl.BlockSpec((1, HEAD_DIM), lambda bi, h, j: (0, 0)),
                  pl.BlockSpec((1, HEAD_DIM), lambda bi, h, j: (0, 0)),
                  pl.BlockSpec((t, HEAD_DIM), lambda bi, h, j: (0, 0)),
                  pl.BlockSpec((t, HEAD_DIM), lambda bi, h, j: (0, 0))],
        out_specs=pl.BlockSpec((1, tq, gw), lambda bi, h, j: (bi, j, h)),
        out_shape=jax.ShapeDtypeStruct((b, t, hq * HEAD_DIM), BF16),
        scratch_shapes=[pltpu.VMEM((t, HEAD_DIM), BF16), pltpu.VMEM((t, HEAD_DIM), BF16)],
        compiler_params=_cparams("parallel", "parallel", "arbitrary"),
        name="attention",
    )(p, p, p, qg, kg, cos, sin)


def _dn_prep_kernel(p_ref, cw_ref, o_ref, *, lc, t, n_qk_tiles):
    c = pl.program_id(1)
    heads = p_ref.shape[2] // HEAD_DIM
    tt = math.gcd(lc, t - lc)
    tt = _pick(tt, (128, 64, 32, 16, 8))
    halo = 8
    zeros = jnp.zeros((halo, HEAD_DIM), F32)
    post = jnp.where(c * heads < n_qk_tiles, HEAD_DIM ** -0.5, 1.0)
    is_qk = c * heads < 2 * n_qk_tiles
    for r0 in range(0, t, tt):
        for hh in range(heads):
            cols = slice(hh * HEAD_DIM, (hh + 1) * HEAD_DIM)
            top = zeros if r0 in (0, lc) else p_ref[0, r0 - halo:r0, cols]
            bot = zeros if r0 + tt in (lc, t) else p_ref[0, r0 + tt:r0 + tt + halo, cols]
            win = jnp.concatenate([top, p_ref[0, r0:r0 + tt, cols], bot], axis=0)
            acc = win[halo:halo + tt] * cw_ref[DN_CONV_W // 2:DN_CONV_W // 2 + 1, cols]
            for jj in range(DN_CONV_W):
                off = jj - DN_CONV_W // 2
                if off != 0:
                    acc = acc + (pltpu.roll(win, (-off) % (tt + 2 * halo), 0)[halo:halo + tt]
                                 * cw_ref[jj:jj + 1, cols])
            y = _silu(acc)
            nrm = lax.rsqrt(jnp.sum(y * y, axis=-1, keepdims=True) + NORM_EPS)
            o_ref[0, r0:r0 + tt, cols] = y * jnp.where(is_qk, nrm * post, 1.0)


def _dn_prep(p, conv_w, lc, n_qk_tiles):
    b, t, _ = p.shape
    nch = conv_w.shape[1]
    cw = _pick(n_qk_tiles, (4, 2, 1)) * HEAD_DIM
    return pl.pallas_call(
        functools.partial(_dn_prep_kernel, lc=lc, t=t, n_qk_tiles=n_qk_tiles),
        grid=(b, nch // cw),
        in_specs=[pl.BlockSpec((1, t, cw), lambda bi, c: (bi, 0, c)),
                  pl.BlockSpec((DN_CONV_W, cw), lambda bi, c: (0, c))],
        out_specs=pl.BlockSpec((1, t, cw), lambda bi, c: (bi, 0, c)),
        out_shape=jax.ShapeDtypeStruct((b, t, nch), F32),
        compiler_params=_cparams("parallel", "parallel"),
        name="dn_prep",
    )(p, conv_w)


def _dn_gates_kernel(ba_ref, alog_ref, dtb_ref, o_ref, *, t, hv):
    x = ba_ref[0]
    lane = lax.broadcasted_iota(jnp.int32, (DN_CHUNK, 4 * hv), 1)
    g = -jnp.exp(alog_ref[...]) * _softplus(x + dtb_ref[...])
    ii = lax.broadcasted_iota(jnp.int32, (DN_CHUNK, DN_CHUNK), 0)
    jj = lax.broadcasted_iota(jnp.int32, (DN_CHUNK, DN_CHUNK), 1)
    lower = (ii >= jj).astype(F32)
    upper = (ii <= jj).astype(F32)
    for c in range(t // DN_CHUNK):
        rows = slice(c * DN_CHUNK, (c + 1) * DN_CHUNK)
        gch = g[rows]
        pre = jnp.dot(lower, gch, precision=lax.Precision.HIGHEST, preferred_element_type=F32)
        suf = jnp.dot(upper, gch, precision=lax.Precision.HIGHEST, preferred_element_type=F32)
        o_ref[0, rows, :] = jnp.where(lane < 2 * hv, jax.nn.sigmoid(x[rows]),
                                      jnp.where(lane < 3 * hv, pre, suf))


def _dn_gates(ba, a_log, dt_bias):
    b, t, n = ba.shape
    hv = n // 4
    zeros = jnp.zeros((1, 2 * hv), F32)
    alog = jnp.concatenate([zeros, a_log.reshape(1, 2 * hv)], axis=1)
    dtb = jnp.concatenate([zeros, dt_bias.reshape(1, 2 * hv)], axis=1)
    return pl.pallas_call(
        functools.partial(_dn_gates_kernel, t=t, hv=hv),
        grid=(b,),
        in_specs=[pl.BlockSpec((1, t, n), lambda bi: (bi, 0, 0)),
                  pl.BlockSpec((1, n), lambda bi: (0, 0)),
                  pl.BlockSpec((1, n), lambda bi: (0, 0))],
        out_specs=pl.BlockSpec((1, t, n), lambda bi: (bi, 0, 0)),
        out_shape=jax.ShapeDtypeStruct((b, t, n), F32),
        compiler_params=_cparams("parallel"),
        name="dn_gates",
    )(ba, alog, dtb)


def _dn_chunk_prep(kk, qk_raw, qc, kc, vc, beta_r, gc_r, reverse):
    c = DN_CHUNK
    ii = lax.broadcasted_iota(jnp.int32, (c, c), 0)
    jj = lax.broadcasted_iota(jnp.int32, (c, c), 1)
    incl = (ii <= jj) if reverse else (ii >= jj)
    strict = (ii < jj) if reverse else (ii > jj)
    beta_c = _row_to_col(beta_r, c)
    gc_c = _row_to_col(gc_r, c)
    g_end = gc_r[:, 0:1] if reverse else gc_r[:, c - 1:c]
    decay = jnp.exp(jnp.where(incl, gc_c - gc_r, -jnp.inf))
    m = jnp.where(strict, beta_c * kk * decay, 0.0)
    eg = jnp.exp(gc_c)
    x = jnp.concatenate([vc * beta_c, kc * (beta_c * eg)], axis=1)
    qk = jnp.where(incl, qk_raw * decay, 0.0)
    return (m, x, qk.astype(BF16), qc * eg, (kc * jnp.exp(g_end - gc_c)).astype(BF16), jnp.exp(g_end))


def _unit_triangular_solve(ms, xs, between=()):
    pws = ms
    ns = [-m for m in ms]
    between = list(between)
    for _ in range(int(math.log2(DN_CHUNK)) - 1):
        pws = [_bdot(p, p) for p in pws]
        ns = [n + p + _bdot(p, n) for p, n in zip(pws, ns)]
        if between:
            between.pop(0)()
    xs = [x + _bdot(n, x) for n, x in zip(ns, xs)]
    for fn in between:
        fn()
    return xs


def _dn_core_kernel(q_ref, k_ref, v_ref, z_ref, gt_ref, ng_ref, o_ref,
                    a_ref, b_ref, c_ref, ge_ref, s_ref, acc_ref, *, lc, t, hv):
    qh = pl.program_id(1)
    c64 = DN_CHUNK
    nc = t // c64
    ncc = lc // c64
    combos = [(sub, d) for sub in range(2) for d in range(2)]

    cpi = _pick(nc, (6, 4, 3, 2))

    def chunk_of(d, n):
        return n if d == 0 else jnp.where(n < ncc, ncc - 1 - n, nc - 1 - (n - ncc))

    def phase_b_step(n):
        cs = [chunk_of(d, n) for _, d in combos]
        ss = [s_ref[ci] for ci in range(4)]
        sb = [s.astype(BF16) for s in ss]
        m1 = [jnp.dot(a_ref[ci, cs[ci]], sb[ci], preferred_element_type=F32) for ci in range(4)]
        m2 = [jnp.dot(c_ref[ci, pl.ds(pl.multiple_of(cs[ci] * c64, c64), c64), :], sb[ci],
                      preferred_element_type=F32) for ci in range(4)]
        for ci, (sub, d) in enumerate(combos):
            s_ref[ci] = ss[ci] * ge_ref[ci, pl.ds(cs[ci], 1), :][:, 0:1] - m1[ci] + b_ref[ci, cs[ci]]
            acc_ref[d, pl.ds(pl.multiple_of(cs[ci] * c64, c64), c64), sub * HEAD_DIM:(sub + 1) * HEAD_DIM] += m2[ci]

    def phase_a(it, between=()):
        prepped, where = [], []
        for j in range(cpi):
            for d in range(2):
                c = chunk_of(d, it * cpi + j)
                r0 = pl.multiple_of(c * c64, c64)
                rows = pl.ds(r0, c64)
                qc, kc = q_ref[0, rows, :], k_ref[0, rows, :]
                gram = _bdot_nt(jnp.concatenate([kc, qc], axis=0), kc)
                kk, qk_raw = gram[:c64], gram[c64:]
                for sub in range(2):
                    head = qh * 2 + sub
                    beta_r = gt_ref[0, d * hv + head, pl.ds(c, 1), :]
                    gc_r = gt_ref[0, (2 + d) * hv + head, pl.ds(c, 1), :]
                    vc = v_ref[0, rows, sub * HEAD_DIM:(sub + 1) * HEAD_DIM]
                    prepped.append(_dn_chunk_prep(kk, qk_raw, qc, kc, vc, beta_r, gc_r, d == 1))
                    where.append((sub * 2 + d, c, rows, sub, d))
        sols = _unit_triangular_solve([p[0] for p in prepped], [p[1] for p in prepped], between)
        kdx = [_bdot_tn(p[4], x) for p, x in zip(prepped, sols)]
        qkx = [_bdot(p[2], x) for p, x in zip(prepped, sols)]
        for (ci, c, rows, sub, d), kx, qx, (_, _, _, qd, _, ge) in zip(where, kdx, qkx, prepped):
            b_ref[ci, c] = kx[:, :HEAD_DIM]
            a_ref[ci, c] = kx[:, HEAD_DIM:].astype(BF16)
            c_ref[ci, rows, :] = (qd - qx[:, HEAD_DIM:]).astype(BF16)
            acc_ref[d, rows, sub * HEAD_DIM:(sub + 1) * HEAD_DIM] = qx[:, :HEAD_DIM]
            ge_ref[ci, pl.ds(c, 1), :] = jnp.broadcast_to(ge, (1, HEAD_DIM))

    s_ref[...] = jnp.zeros_like(s_ref)
    phase_a(0)

    def pipelined(it, carry):
        phase_a(it, [functools.partial(phase_b_step, (it - 1) * cpi + j) for j in range(cpi)])
        return carry

    lax.fori_loop(1, nc // cpi, pipelined, 0)
    for j in range(cpi):
        phase_b_step(nc - cpi + j)

    for sub in range(2):
        cols = slice(sub * HEAD_DIM, (sub + 1) * HEAD_DIM)
        o = acc_ref[0, :, cols] + acc_ref[1, :, cols]
        o = o * lax.rsqrt(jnp.mean(o * o, axis=-1, keepdims=True) + NORM_EPS) * ng_ref[...]
        o_ref[0, :, cols] = (o * _silu(z_ref[0, :, cols])).astype(o_ref.dtype)


def _dn_core(qkv, pz, gt, norm_g, lc, n_qk_heads, z_col0):
    b, t, _ = qkv.shape
    hv = 2 * n_qk_heads
    vw = 2 * HEAD_DIM
    nc = t // DN_CHUNK
    return pl.pallas_call(
        functools.partial(_dn_core_kernel, lc=lc, t=t, hv=hv),
        grid=(b, n_qk_heads),
        in_specs=[pl.BlockSpec((1, t, HEAD_DIM), lambda bi, h: (bi, 0, h)),
                  pl.BlockSpec((1, t, HEAD_DIM), lambda bi, h: (bi, 0, n_qk_heads + h)),
                  pl.BlockSpec((1, t, vw), lambda bi, h: (bi, 0, n_qk_heads + h)),
                  pl.BlockSpec((1, t, vw), lambda bi, h: (bi, 0, z_col0 // vw + h)),
                  pl.BlockSpec((1, 4 * hv, nc, DN_CHUNK), lambda bi, h: (bi, 0, 0, 0)),
                  pl.BlockSpec((1, HEAD_DIM), lambda bi, h: (0, 0))],
        out_specs=pl.BlockSpec((1, t, vw), lambda bi, h: (bi, 0, h)),
        out_shape=jax.ShapeDtypeStruct((b, t, hv * HEAD_DIM), BF16),
        scratch_shapes=[pltpu.VMEM((4, nc, HEAD_DIM, HEAD_DIM), BF16),
                        pltpu.VMEM((4, nc, HEAD_DIM, HEAD_DIM), F32),
                        pltpu.VMEM((4, t, HEAD_DIM), BF16),
                        pltpu.VMEM((4, -(-nc // 8) * 8, HEAD_DIM), F32),
                        pltpu.VMEM((4, HEAD_DIM, HEAD_DIM), F32),
                        pltpu.VMEM((2, t, vw), F32)],
        compiler_params=_cparams("parallel", "parallel"),
        name="dn_core",
    )(qkv, qkv, qkv, pz, gt, norm_g)


def _hg_kernel(q_ref, i_ref, gate_ref, ff_ref, fb_ref, lb_ref, ng_ref, o_ref,
               lf_ref, k_ref, st_ref, acc_ref, *, lc, t):
    cs, c16 = HG_SUPER, HG_CHUNK
    nsub = cs // c16
    nc = t // cs
    ncc = lc // cs
    ii = lax.broadcasted_iota(jnp.int32, (cs, cs), 0)
    jj = lax.broadcasted_iota(jnp.int32, (cs, cs), 1)
    krow = lax.broadcasted_iota(jnp.int32, (cs, HEAD_DIM), 0)
    incl = [ii >= jj, ii <= jj]
    sums = [jnp.concatenate([m.astype(BF16), (m & (ii // c16 == jj // c16)).astype(BF16),
                             jnp.ones((8, cs), BF16)], axis=0) for m in incl]
    for d in range(2):
        f_ref = ff_ref if d == 0 else fb_ref
        lb = lb_ref[d:d + 1, :]
        fg = lb + (1.0 - lb) * _sigmoid(f_ref[0])
        k_ref[d] = 1.0 - fg
        lf_ref[d] = jnp.log(fg)
    st_ref[...] = jnp.zeros_like(st_ref)
    per_it = _pick(nc, (4, 3, 2))

    def scan_stage(jobs):
        scans = []
        for d, r0 in jobs:
            lf = lf_ref[d, pl.ds(r0, cs), :]
            hi = lf.astype(BF16)
            r1 = lf - hi.astype(F32)
            mid = r1.astype(BF16)
            lo = (r1 - mid.astype(F32)).astype(BF16)
            sc = jnp.dot(sums[d], jnp.concatenate([hi, mid, lo], axis=1), preferred_element_type=F32)
            scans.append(sc[:, :HEAD_DIM] + sc[:, HEAD_DIM:2 * HEAD_DIM] + sc[:, 2 * HEAD_DIM:])
        return scans

    def score_stage(jobs, scans):
        kcs = [k_ref[d, pl.ds(r0, cs), :] for d, r0 in jobs]
        qcs = [q_ref[0, pl.ds(r0, cs), :] for d, r0 in jobs]
        vvs = [i_ref[0, pl.ds(r0, cs), :].astype(BF16) for d, r0 in jobs]
        gcs = [sc[:cs] for sc in scans]
        tots = [sc[2 * cs:2 * cs + 1] for sc in scans]
        qas = [q * jnp.exp(sc[cs:2 * cs]) for q, sc in zip(qcs, scans)]
        qds = [q * jnp.exp(g) for q, g in zip(qcs, gcs)]
        kds = [k * jnp.exp(tt - g) for k, tt, g in zip(kcs, tots, gcs)]
        upd = [_bdot_tn(v, kd) for v, kd in zip(vvs, kds)]
        scaled = []
        for (d, _), gc, kc in zip(jobs, gcs, kcs):
            for a in range(nsub):
                if d == 0:
                    ref = gc[a * c16 - 1:a * c16, :] if a > 0 else None
                    allowed = krow < (a + 1) * c16
                else:
                    ref = gc[(a + 1) * c16:(a + 1) * c16 + 1, :] if a < nsub - 1 else None
                    allowed = krow >= a * c16
                expo = -gc if ref is None else ref - gc
                scaled.append(kc * jnp.exp(jnp.where(allowed, expo, -jnp.inf)))
        parts = [_bdot_nt(qas[j][a * c16:(a + 1) * c16], scaled[j * nsub + a])
                 for j in range(len(jobs)) for a in range(nsub)]
        return vvs, tots, qds, upd, parts

    def output_stage(jobs, staged, sts):
        vvs, tots, qds, upd, parts = staged
        ps = [jnp.where(incl[d], jnp.concatenate(parts[j * nsub:(j + 1) * nsub], axis=0), 0.0)
              for j, (d, _) in enumerate(jobs)]
        intra = [_bdot(p, v) for p, v in zip(ps, vvs)]
        for j, (d, r0) in enumerate(jobs):
            acc_ref[d, pl.ds(r0, cs), :] = intra[j] + _bdot_nt(qds[j], sts[d])
            sts[d] = sts[d] * jnp.exp(tots[j]) + upd[j]

    def body(it, carry):
        jobs = []
        for u in range(per_it):
            n = it * per_it + u
            jobs.append((0, pl.multiple_of(n * cs, cs)))
            jobs.append((1, pl.multiple_of(jnp.where(n < ncc, ncc - 1 - n, nc - 1 - (n - ncc)) * cs, cs)))
        half = len(jobs) // 2 if len(jobs) >= 4 else len(jobs)
        first, second = jobs[:half], jobs[half:]
        scans1 = scan_stage(first)
        scans2 = scan_stage(second)
        staged1 = score_stage(first, scans1)
        staged2 = score_stage(second, scans2) if second else None
        sts = [st_ref[d] for d in range(2)]
        output_stage(first, staged1, sts)
        if second:
            output_stage(second, staged2, sts)
        for d in range(2):
            st_ref[d] = sts[d]
        return carry

    lax.fori_loop(0, nc // per_it, body, 0)
    o = acc_ref[0] + acc_ref[1]
    o = o * lax.rsqrt(jnp.mean(o * o, axis=-1, keepdims=True) + NORM_EPS) * ng_ref[...]
    o_ref[0] = (o * _sigmoid(gate_ref[0])).astype(o_ref.dtype)


def _hgrn2(p, lb, norm_g, lc, n_heads):
    b, t, _ = p.shape
    assert lc % HG_SUPER == 0 and t % HG_SUPER == 0
    blk = lambda k: pl.BlockSpec((1, t, HEAD_DIM), lambda bi, h, k=k: (bi, 0, k * n_heads + h))
    seq = lambda dt: pltpu.VMEM((2, t, HEAD_DIM), dt)
    return pl.pallas_call(
        functools.partial(_hg_kernel, lc=lc, t=t),
        grid=(b, n_heads),
        in_specs=[blk(0), blk(1), blk(2), blk(3), blk(4),
                  pl.BlockSpec((2, HEAD_DIM), lambda bi, h: (0, h)),
                  pl.BlockSpec((1, HEAD_DIM), lambda bi, h: (0, 0))],
        out_specs=pl.BlockSpec((1, t, HEAD_DIM), lambda bi, h: (bi, 0, h)),
        out_shape=jax.ShapeDtypeStruct((b, t, n_heads * HEAD_DIM), BF16),
        scratch_shapes=[seq(F32), seq(F32), pltpu.VMEM((2, HEAD_DIM, HEAD_DIM), F32), seq(F32)],
        compiler_params=_cparams("parallel", "parallel"),
        name="hgrn2",
    )(p, p, p, p, p, lb, norm_g)


def _split_bf16(x):
    hi = x.astype(BF16)
    return hi, (x - hi.astype(F32)).astype(BF16)


def _router_kernel(h_ref, w_ref, b_ref, e_ref, g_ref):
    h_hi, h_lo = _split_bf16(h_ref[...])
    w_hi, w_lo = _split_bf16(w_ref[...])
    nt = (((1,), (1,)), ((), ()))
    logits = (lax.dot_general(w_hi, h_hi, nt, preferred_element_type=F32)
              + lax.dot_general(w_hi, h_lo, nt, preferred_element_type=F32)
              + lax.dot_general(w_lo, h_hi, nt, preferred_element_type=F32))
    scores = jax.nn.sigmoid(logits)
    sel = scores + b_ref[...]
    epg = EXPERTS_PER_GROUP
    rows = lambda a, e: a[e:e + 1, :]
    best_gs = None
    for g in range(N_GROUPS):
        gs = None
        for a in range(epg):
            for c in range(a + 1, epg):
                pair = rows(sel, g * epg + a) + rows(sel, g * epg + c)
                gs = pair if gs is None else jnp.maximum(gs, pair)
        if best_gs is None:
            best_gs, group = gs, jnp.zeros(gs.shape, jnp.int32)
        else:
            upd = gs > best_gs
            best_gs = jnp.where(upd, gs, best_gs)
            group = jnp.where(upd, g, group)

    def in_group(a, l):
        v = rows(a, l)
        for g in range(1, N_GROUPS):
            v = jnp.where(group == g, rows(a, g * epg + l), v)
        return v

    sel_l = [in_group(sel, l) for l in range(epg)]
    sc_l = [in_group(scores, l) for l in range(epg)]

    def top1(exclude):
        best, idx, sc = None, None, None
        for l in range(epg):
            v = sel_l[l] if exclude is None else jnp.where(exclude == l, -jnp.inf, sel_l[l])
            if best is None:
                best, idx, sc = v, jnp.zeros(v.shape, jnp.int32), sc_l[l]
            else:
                upd = v > best
                best = jnp.where(upd, v, best)
                idx = jnp.where(upd, l, idx)
                sc = jnp.where(upd, sc_l[l], sc)
        return idx, sc

    l0, s0 = top1(None)
    l1, s1 = top1(l0)
    e_ref[0:1, :] = group * epg + l0
    e_ref[1:2, :] = group * epg + l1
    den = s0 + s1
    g_ref[0:1, :] = s0 / den
    g_ref[1:2, :] = s1 / den


def _router(h, router_w, router_b):
    m, d = h.shape
    tm = _pick(m, (512, 256, 128))
    ne = router_w.shape[1]
    return pl.pallas_call(
        _router_kernel,
        grid=(m // tm,),
        in_specs=[pl.BlockSpec((tm, d), lambda i: (i, 0)),
                  pl.BlockSpec((ne, d), lambda i: (0, 0)),
                  pl.BlockSpec((ne, 1), lambda i: (0, 0))],
        out_specs=[pl.BlockSpec((TOP_K, tm), lambda i: (0, i)),
                   pl.BlockSpec((TOP_K, tm), lambda i: (0, i))],
        out_shape=[jax.ShapeDtypeStruct((TOP_K, m), jnp.int32), jax.ShapeDtypeStruct((TOP_K, m), F32)],
        compiler_params=_cparams("parallel"),
        name="router",
    )(h, router_w.T, router_b.reshape(ne, 1))


def _moe_ffn_kernel(bexp_ref, code_ref, h_hbm, wg_ref, wu_ref, wd_ref, f_hbm,
                    x0, x1, x2, y0, y1, y2, sem_in, sem_out, *, n_tok):
    i = pl.program_id(0)
    last = pl.num_programs(0) - 1
    eb = EXPERT_BLOCK
    xbufs, ybufs = (x0, x1, x2), (y0, y1, y2)
    ns = len(xbufs)

    def gather(blk, sl, r):
        tok = code_ref[(blk + 1) * eb + r] & (CODE_K_UNIT - 1)
        return pltpu.make_async_copy(h_hbm.at[pl.ds(tok, 1)], xbufs[sl].at[pl.ds(r, 1)], sem_in.at[sl])

    def scatter(blk, sl, r):
        code = code_ref[(blk + 1) * eb + r]
        dst = lax.shift_right_logical(code, CODE_K_SHIFT) * n_tok + (code & (CODE_K_UNIT - 1))
        return pltpu.make_async_copy(ybufs[sl].at[pl.ds(r, 1)], f_hbm.at[pl.ds(dst, 1)], sem_out.at[sl])

    def for_rows(fn):
        def body(r, carry):
            fn(r)
            return carry
        lax.fori_loop(0, eb, body, 0, unroll=8)

    @pl.when(i == 0)
    def _():
        for sl in range(ns):
            ybufs[sl][...] = jnp.zeros_like(ybufs[sl])
        for sl in range(2):
            spare = pltpu.make_async_copy(ybufs[sl], f_hbm.at[pl.ds(TOP_K * n_tok + sl * eb, eb)],
                                          sem_out.at[sl])
            spare.start()
            spare.wait()
        for_rows(lambda r: gather(0, 0, r).start())
        for_rows(lambda r: gather(1, 1, r).start())

    def step(sl):
        ahead, behind = (sl + 2) % ns, (sl + 2) % ns
        @pl.when(i >= 2)
        def _():
            for_rows(lambda r: scatter(i - 3, sl, r).wait())
        for_rows(lambda r: gather(i, sl, r).wait())
        x = xbufs[sl][...].astype(BF16)
        for r in range(eb):
            gather(i + 2, ahead, r).start()
        a = _silu(jnp.dot(x, wg_ref[0, 0], preferred_element_type=F32)) \
            * jnp.dot(x, wu_ref[0, 0], preferred_element_type=F32)
        for r in range(eb):
            scatter(i - 1, behind, r).start()
        ybufs[sl][...] = jnp.dot(a.astype(BF16), wd_ref[0, 0], preferred_element_type=F32)

    for sl in range(ns):
        @pl.when(i % ns == sl)
        def _(sl=sl):
            step(sl)

    @pl.when(i == last)
    def _():
        for sl in range(ns):
            @pl.when(i % ns == sl)
            def _(sl=sl):
                for_rows(lambda r: scatter(i - 2, (sl + 1) % ns, r).wait())
                for_rows(lambda r: scatter(i - 1, (sl + 2) % ns, r).wait())
                for_rows(lambda r: gather(i + 1, (sl + 1) % ns, r).wait())
                for_rows(lambda r: gather(i + 2, (sl + 2) % ns, r).wait())


def _moe_ffn(h, block_expert, code_buf, wg, wu, wd, layer):
    m, d = h.shape
    de = wg.shape[3]
    nb = block_expert.shape[0]
    wmap = lambda i, be, cb: (layer, be[jnp.minimum(i, nb - 1)], 0, 0)
    buf = pltpu.VMEM((EXPERT_BLOCK, d), F32)
    grid_spec = pltpu.PrefetchScalarGridSpec(
        num_scalar_prefetch=2,
        grid=(nb + 1,),
        in_specs=[pl.BlockSpec(memory_space=pl.ANY),
                  pl.BlockSpec((1, 1, d, de), wmap),
                  pl.BlockSpec((1, 1, d, de), wmap),
                  pl.BlockSpec((1, 1, de, d), wmap)],
        out_specs=pl.BlockSpec(memory_space=pl.ANY),
        scratch_shapes=[buf] * 6 + [pltpu.SemaphoreType.DMA((3,)), pltpu.SemaphoreType.DMA((3,))],
    )
    return pl.pallas_call(
        functools.partial(_moe_ffn_kernel, n_tok=m),
        grid_spec=grid_spec,
        out_shape=jax.ShapeDtypeStruct((TOP_K * m + 2 * EXPERT_BLOCK, d), F32),
        compiler_params=_cparams("arbitrary"),
        name="moe_ffn",
    )(block_expert, code_buf, h, wg, wu, wd)


def _moe_dispatch(expert):
    m = expert.shape[1]
    n_pairs = m * TOP_K
    e_flat = expert.T.reshape(-1)
    onehot = (e_flat[:, None] == jnp.arange(N_EXPERTS)[None, :]).astype(jnp.int32)
    csum = jnp.cumsum(onehot, axis=0)
    rank = jnp.take_along_axis(csum, e_flat[:, None], axis=1)[:, 0] - 1
    counts = csum[-1]
    padded = (counts + EXPERT_BLOCK - 1) // EXPERT_BLOCK * EXPERT_BLOCK
    pend = jnp.cumsum(padded)
    pstart = pend - padded
    dest = pstart[e_flat] + rank
    n_blocks = -(-(n_pairs + N_EXPERTS * (EXPERT_BLOCK - 1)) // EXPERT_BLOCK)
    n_rows = n_blocks * EXPERT_BLOCK
    assert 2 * EXPERT_BLOCK <= m <= CODE_K_UNIT
    pair = jnp.arange(n_pairs, dtype=jnp.int32)
    row = jnp.arange(n_rows + 4 * EXPERT_BLOCK, dtype=jnp.int32)
    pad_code = TOP_K * CODE_K_UNIT + row % (2 * EXPERT_BLOCK)
    code_buf = pad_code.at[dest + EXPERT_BLOCK].set((pair % TOP_K) * CODE_K_UNIT + pair // TOP_K)
    blk_start = jnp.arange(n_blocks) * EXPERT_BLOCK
    block_expert = jnp.minimum(jnp.searchsorted(pend, blk_start, side='right'), N_EXPERTS - 1).astype(jnp.int32)
    return block_expert, code_buf


def _moe(h2, router_w, router_b, wg, wu, wd, layer):
    expert, gate = _router(h2, router_w, router_b)
    block_expert, code_buf = _moe_dispatch(expert)
    f = _moe_ffn(h2, block_expert, code_buf, wg, wu, wd, layer)
    return f, jnp.pad(gate.T, ((0, 0), (0, HEAD_DIM - TOP_K)))


def kernel(x, c, ctx, c_ctx, ada_w, ada_b, ln_g, ln_b, attn_w_in, attn_q_g, attn_k_g, attn_w_out, dn_w_in, dn_conv, dn_a_log, dn_dt_bias, dn_norm_g, dn_w_out, hg_w_in, hg_lb, hg_norm_g, hg_w_out, router_w, router_b, moe_w_gate, moe_w_up, moe_w_down):
    b, l, d = x.shape
    lc = ctx.shape[1]
    t = lc + l
    m = b * t
    depth = ada_w.shape[0]
    alpha = (2.0 * depth) ** 0.25
    tm = _pick(math.gcd(lc, l), (256, 128, 64))
    tiles_per_seq, ctx_tiles = t // tm, lc // tm
    n_heads = d // HEAD_DIM

    mp = -(-(b + 1) // 8) * 8
    cc = jnp.concatenate([c, c_ctx[None, :], jnp.zeros((mp - b - 1, d), F32)], axis=0)
    mod = _ada_mod(cc, ada_w, ada_b)
    mod_l = mod[:, :b].reshape(depth, b, 1, N_MOD, d)
    mod_c = jnp.broadcast_to(mod[:, b].reshape(depth, 1, 1, N_MOD, d), (depth, b, 1, N_MOD, d))
    modtab = jnp.concatenate([mod_c, mod_l], axis=2).reshape(depth, 2 * b, N_MOD, d)

    lb_cum = jnp.cumsum(jax.nn.softmax(hg_lb.astype(F32), axis=1), axis=1)
    lb_cum = lb_cum - lb_cum[:, :1]

    xs, h = _modulate(ctx, x, modtab[0], tm, tiles_per_seq, ctx_tiles, 0, 1, BF16)
    cos, sin = _rope_tables(lc, l)
    rows = (tm, tiles_per_seq, ctx_tiles)

    moe_wg, moe_wu, moe_wd = moe_w_gate.astype(BF16), moe_w_up.astype(BF16), moe_w_down.astype(BF16)

    for i in range(depth):
        kind, j = i % 3, i // 3
        if kind == 0:
            p = _proj(h, attn_w_in[j], 0, attn_w_in.shape[2])
            o = _attention(p.reshape(b, t, -1), attn_q_g[j][None, :], attn_k_g[j][None, :], cos, sin, lc, tm)
            w_out = attn_w_out[j]
        elif kind == 1:
            n_conv = dn_conv.shape[2]
            hv = dn_a_log.shape[2]
            n_main = n_conv + hv * HEAD_DIM
            p = _proj(h, dn_w_in[j], 0, n_main).reshape(b, t, n_main)
            ba = _proj(h, dn_w_in[j], n_main, 4 * hv).reshape(b, t, 4 * hv)
            qkv = _dn_prep(p, dn_conv[j], lc, hv // 2)
            gates = _dn_gates(ba, dn_a_log[j], dn_dt_bias[j])
            gt = jnp.swapaxes(gates, 1, 2).reshape(b, 4 * hv, t // DN_CHUNK, DN_CHUNK)
            o = _dn_core(qkv, p, gt, dn_norm_g[j][None, :], lc, hv // 2, n_conv)
            w_out = dn_w_out[j]
        else:
            p = _proj(h, hg_w_in[j], 0, hg_w_in.shape[2]).reshape(b, t, -1)
            o = _hgrn2(p, lb_cum[:, i], hg_norm_g[j][None, :], lc, n_heads)
            w_out = hg_w_out[j]
        xs, h2 = _out_ln(o.reshape(m, -1), w_out.astype(BF16), xs, modtab[i], modtab[i],
                         ln_g[i, 0][None, :], ln_b[i, 0][None, :], *rows, alpha, 2, 3, 4, F32)
        f, rg = _moe(h2, router_w, router_b, moe_wg, moe_wu, moe_wd, i)
        nxt = modtab[min(i + 1, depth - 1)]
        xs, h = _comb_ln(f, rg, xs, modtab[i], nxt, ln_g[i, 1][None, :], ln_b[i, 1][None, :],
                         *rows, alpha, 5, 0, 1, BF16, latent_only=(i == depth - 1))
    return xs.reshape(b, l, d)
```

```python
import functools
import math

import jax
import jax.numpy as jnp
from jax import lax
from jax.experimental import pallas as pl
from jax.experimental.pallas import tpu as pltpu

F32 = jnp.float32
BF16 = jnp.bfloat16

NORM_EPS = 1e-6
HEAD_DIM = 128
GRID_W = 64
ROPE_BASE = 10000.0
ROPE_FREQS = HEAD_DIM // 4
ATTN_GROUP = 4
DN_CONV_W = 5
DN_CHUNK = 64
HG_CHUNK = 16
HG_SUPER = 64
N_EXPERTS = 16
N_GROUPS = 4
EXPERTS_PER_GROUP = N_EXPERTS // N_GROUPS
TOP_K = 2
EXPERT_BLOCK = 256
CODE_K_SHIFT = 16
CODE_K_UNIT = 1 << CODE_K_SHIFT
N_MOD = 6
VMEM_LIMIT = 56 * 1024 * 1024


def _cparams(*sem):
    return pltpu.CompilerParams(dimension_semantics=sem, vmem_limit_bytes=VMEM_LIMIT)


def _pick(n, prefs):
    for p in prefs:
        if n % p == 0:
            return p
    return n


def _bdot(a, b):
    return jnp.dot(a.astype(BF16), b.astype(BF16), preferred_element_type=F32)


def _bdot_nt(a, b):
    return lax.dot_general(a.astype(BF16), b.astype(BF16), (((1,), (1,)), ((), ())),
                           preferred_element_type=F32)


def _bdot_tn(a, b):
    return lax.dot_general(a.astype(BF16), b.astype(BF16), (((0,), (0,)), ((), ())),
                           preferred_element_type=F32)


def _sigmoid(x):
    return 0.5 * (1.0 + jnp.tanh(0.5 * x))


def _silu(x):
    return x * _sigmoid(x)


def _softplus(x):
    return jnp.maximum(x, 0.0) + jnp.log1p(jnp.exp(-jnp.abs(x)))


def _row_to_col(row, n):
    eye = lax.broadcasted_iota(jnp.int32, (n, n), 0) == lax.broadcasted_iota(jnp.int32, (n, n), 1)
    return jnp.sum(jnp.where(eye, row, 0.0), axis=1, keepdims=True)


def _ada_kernel(cc_ref, w_ref, b_ref, o_ref):
    a = _silu(cc_ref[...])
    o_ref[0] = _bdot(a, w_ref[0]) + b_ref[0]


def _ada_mod(cc, ada_w, ada_b):
    depth, d, n = ada_w.shape
    mp = cc.shape[0]
    tn = _pick(n, (1024, 512, 256, 128))
    return pl.pallas_call(
        _ada_kernel,
        grid=(depth, n // tn),
        in_specs=[pl.BlockSpec((mp, d), lambda i, j: (0, 0)),
                  pl.BlockSpec((1, d, tn), lambda i, j: (i, 0, j)),
                  pl.BlockSpec((1, 1, tn), lambda i, j: (i, 0, j))],
        out_specs=pl.BlockSpec((1, mp, tn), lambda i, j: (i, 0, j)),
        out_shape=jax.ShapeDtypeStruct((depth, mp, n), F32),
        compiler_params=_cparams("parallel", "parallel"),
        name="ada_mod",
    )(cc, ada_w, ada_b.reshape(depth, 1, n))


def _modulate_kernel(ctx_ref, lat_ref, mod_ref, xs_ref, h_ref, *, shift, scale, tiles_per_seq, ctx_tiles):
    m = mod_ref[0]
    is_ctx = pl.program_id(0) % tiles_per_seq < ctx_tiles
    x = jnp.where(is_ctx, ctx_ref[0], lat_ref[0])
    xs_ref[...] = x
    h_ref[...] = (x * (1.0 + m[scale:scale + 1]) + m[shift:shift + 1]).astype(h_ref.dtype)


def _mod_index(tiles_per_seq, ctx_tiles):
    def index(i):
        return ((i // tiles_per_seq) * 2 + jnp.where(i % tiles_per_seq < ctx_tiles, 0, 1), 0, 0)
    return index


def _modulate(ctx, x, modtab, tm, tiles_per_seq, ctx_tiles, shift, scale, dtype):
    b, _, d = x.shape
    midx = _mod_index(tiles_per_seq, ctx_tiles)
    m = b * tiles_per_seq * tm
    return pl.pallas_call(
        functools.partial(_modulate_kernel, shift=shift, scale=scale, tiles_per_seq=tiles_per_seq,
                          ctx_tiles=ctx_tiles),
        grid=(m // tm,),
        in_specs=[pl.BlockSpec((1, tm, d), lambda i: (i // tiles_per_seq,
                                                      jnp.minimum(i % tiles_per_seq, ctx_tiles - 1), 0)),
                  pl.BlockSpec((1, tm, d), lambda i: (i // tiles_per_seq,
                                                      jnp.maximum(i % tiles_per_seq - ctx_tiles, 0), 0)),
                  pl.BlockSpec((1, N_MOD, d), midx)],
        out_specs=[pl.BlockSpec((tm, d), lambda i: (i, 0)), pl.BlockSpec((tm, d), lambda i: (i, 0))],
        out_shape=[jax.ShapeDtypeStruct((m, d), F32), jax.ShapeDtypeStruct((m, d), dtype)],
        compiler_params=_cparams("parallel"),
        name="modulate",
    )(ctx, x, modtab)


def _proj_kernel(a_ref, w_ref, o_ref, wb_ref):
    @pl.when(pl.program_id(1) == 0)
    def _():
        wb_ref[...] = w_ref[...].astype(BF16)

    o_ref[...] = jnp.dot(a_ref[...], wb_ref[...], preferred_element_type=F32).astype(o_ref.dtype)


def _proj(a, w, col0, ncols, out_dtype=F32):
    m, k = a.shape
    tn = _pick(ncols, (1024, 512, 256, 128))
    assert col0 % tn == 0
    tm = _pick(m, (1024, 512, 256, 128, 64))
    c0 = col0 // tn
    return pl.pallas_call(
        _proj_kernel,
        grid=(ncols // tn, m // tm),
        in_specs=[pl.BlockSpec((tm, k), lambda j, i: (i, 0)),
                  pl.BlockSpec((k, tn), lambda j, i: (0, c0 + j))],
        out_specs=pl.BlockSpec((tm, tn), lambda j, i: (i, j)),
        out_shape=jax.ShapeDtypeStruct((m, ncols), out_dtype),
        scratch_shapes=[pltpu.VMEM((k, tn), BF16)],
        compiler_params=_cparams("parallel", "arbitrary"),
        name="proj",
    )(a, w)


def _ln_epilogue(y, x, mod, modn, lng, lnb, alpha, gate, shift, scale):
    r = alpha * x + mod[gate:gate + 1] * y
    rc = r - jnp.mean(r, axis=-1, keepdims=True)
    var = jnp.mean(rc * rc, axis=-1, keepdims=True)
    xn = rc * lax.rsqrt(var + NORM_EPS) * lng + lnb
    h = xn * (1.0 + modn[scale:scale + 1]) + modn[shift:shift + 1]
    return xn, h


def _out_ln_kernel(o_ref, w_ref, x_ref, mod_ref, modn_ref, lng_ref, lnb_ref, xn_ref, h_ref, acc_ref,
                   *, alpha, gate, shift, scale, nk):
    kk = pl.program_id(1)
    part = jnp.dot(o_ref[...], w_ref[...], preferred_element_type=F32)

    def finish(y):
        xn, h = _ln_epilogue(y, x_ref[...], mod_ref[0], modn_ref[0], lng_ref[...], lnb_ref[...],
                             alpha, gate, shift, scale)
        xn_ref[...] = xn
        h_ref[...] = h.astype(h_ref.dtype)

    if nk == 1:
        finish(part)
    else:
        @pl.when(kk == 0)
        def _():
            acc_ref[...] = part

        @pl.when((kk > 0) & (kk < nk - 1))
        def _():
            acc_ref[...] += part

        @pl.when(kk == nk - 1)
        def _():
            finish(acc_ref[...] + part)


def _out_ln(o, w, x, mod, modn, lng, lnb, tm, tiles_per_seq, ctx_tiles, alpha, gate, shift, scale, h_dtype):
    m, k = o.shape
    d = w.shape[1]
    tk = _pick(k, (4096, 2048, 1024, 512, 256, 128))
    midx = _mod_index(tiles_per_seq, ctx_tiles)
    return pl.pallas_call(
        functools.partial(_out_ln_kernel, alpha=alpha, gate=gate, shift=shift, scale=scale, nk=k // tk),
        grid=(m // tm, k // tk),
        in_specs=[pl.BlockSpec((tm, tk), lambda i, j: (i, j)),
                  pl.BlockSpec((tk, d), lambda i, j: (j, 0)),
                  pl.BlockSpec((tm, d), lambda i, j: (i, 0)),
                  pl.BlockSpec((1, N_MOD, d), lambda i, j: midx(i)),
                  pl.BlockSpec((1, N_MOD, d), lambda i, j: midx(i)),
                  pl.BlockSpec((1, d), lambda i, j: (0, 0)),
                  pl.BlockSpec((1, d), lambda i, j: (0, 0))],
        out_specs=[pl.BlockSpec((tm, d), lambda i, j: (i, 0)),
                   pl.BlockSpec((tm, d), lambda i, j: (i, 0))],
        out_shape=[jax.ShapeDtypeStruct((m, d), F32), jax.ShapeDtypeStruct((m, d), h_dtype)],
        scratch_shapes=[pltpu.VMEM((tm, d), F32)],
        compiler_params=_cparams("parallel", "arbitrary"),
        name="out_ln",
    )(o, w, x, mod, modn, lng, lnb)


def _comb_ln_kernel(f0_ref, f1_ref, rg_ref, x_ref, mod_ref, modn_ref, lng_ref, lnb_ref, xn_ref, h_ref,
                    *, alpha, gate, shift, scale):
    y = f0_ref[...] * rg_ref[:, 0:1] + f1_ref[...] * rg_ref[:, 1:2]
    xn, h = _ln_epilogue(y, x_ref[...], mod_ref[0], modn_ref[0], lng_ref[...], lnb_ref[...],
                         alpha, gate, shift, scale)
    xn_ref[...] = xn
    h_ref[...] = h.astype(h_ref.dtype)


def _comb_ln(f, rg, x, mod, modn, lng, lnb, tm, tiles_per_seq, ctx_tiles, alpha, gate, shift, scale, h_dtype,
             latent_only=False):
    m, d = x.shape
    midx = _mod_index(tiles_per_seq, ctx_tiles)
    nt = m // tm
    if latent_only:
        lat_tiles = tiles_per_seq - ctx_tiles
        n_out = nt // tiles_per_seq * lat_tiles
        src = lambda i: (i // lat_tiles) * tiles_per_seq + ctx_tiles + i % lat_tiles
    else:
        n_out = nt
        src = lambda i: i
    return pl.pallas_call(
        functools.partial(_comb_ln_kernel, alpha=alpha, gate=gate, shift=shift, scale=scale),
        grid=(n_out,),
        in_specs=[pl.BlockSpec((tm, d), lambda i: (src(i), 0)),
                  pl.BlockSpec((tm, d), lambda i: (nt + src(i), 0)),
                  pl.BlockSpec((tm, HEAD_DIM), lambda i: (src(i), 0)),
                  pl.BlockSpec((tm, d), lambda i: (src(i), 0)),
                  pl.BlockSpec((1, N_MOD, d), lambda i: midx(src(i))),
                  pl.BlockSpec((1, N_MOD, d), lambda i: midx(src(i))),
                  pl.BlockSpec((1, d), lambda i: (0, 0)),
                  pl.BlockSpec((1, d), lambda i: (0, 0))],
        out_specs=[pl.BlockSpec((tm, d), lambda i: (i, 0)),
                   pl.BlockSpec((tm, d), lambda i: (i, 0))],
        out_shape=[jax.ShapeDtypeStruct((n_out * tm, d), F32), jax.ShapeDtypeStruct((n_out * tm, d), h_dtype)],
        compiler_params=_cparams("parallel"),
        name="comb_ln",
    )(f, f, rg, x, mod, modn, lng, lnb)


def _rope_tables(lc, l):
    pos = jnp.arange(l)
    row = (pos // GRID_W).astype(F32)
    col = (pos % GRID_W).astype(F32)
    inv_freq = ROPE_BASE ** (-jnp.arange(ROPE_FREQS, dtype=F32) / ROPE_FREQS)
    ar = row[:, None] * inv_freq
    ac = col[:, None] * inv_freq
    cos = jnp.concatenate([jnp.cos(ar), jnp.cos(ar), jnp.cos(ac), jnp.cos(ac)], axis=1)
    sin = jnp.concatenate([-jnp.sin(ar), jnp.sin(ar), -jnp.sin(ac), jnp.sin(ac)], axis=1)
    cos = jnp.concatenate([jnp.ones((lc, HEAD_DIM), F32), cos], axis=0)
    sin = jnp.concatenate([jnp.zeros((lc, HEAD_DIM), F32), sin], axis=0)
    return cos, sin


def _norm_rope(x, g, cos, sin):
    y = x * lax.rsqrt(jnp.mean(x * x, axis=-1, keepdims=True) + NORM_EPS) * g
    lane = lax.broadcasted_iota(jnp.int32, y.shape, 1)
    partner = jnp.where(lane % (2 * ROPE_FREQS) < ROPE_FREQS,
                        pltpu.roll(y, HEAD_DIM - ROPE_FREQS, 1), pltpu.roll(y, ROPE_FREQS, 1))
    return y * cos + partner * sin


def _attn_kernel(q_ref, k_ref, v_ref, qg_ref, kg_ref, cos_ref, sin_ref, o_ref, kb_ref, vb_ref,
                 *, lc, tq, t):
    j = pl.program_id(2)

    @pl.when(j == 0)
    def _():
        kb_ref[...] = _norm_rope(k_ref[0], kg_ref[...], cos_ref[...], sin_ref[...]).astype(BF16)
        vb_ref[...] = v_ref[0].astype(BF16)

    r0 = pl.multiple_of(j * tq, tq)
    cos = cos_ref[pl.ds(r0, tq), :]
    sin = sin_ref[pl.ds(r0, tq), :]
    scale = HEAD_DIM ** -0.5

    def attend(n_keys):
        def scores(g):
            qh = _norm_rope(q_ref[0, :, g * HEAD_DIM:(g + 1) * HEAD_DIM], qg_ref[...], cos, sin)
            return _bdot_nt(qh, kb_ref[0:n_keys, :])

        s_next = scores(0)
        for g in range(ATTN_GROUP):
            s = s_next
            if g + 1 < ATTN_GROUP:
                s_next = scores(g + 1)
            e = jnp.exp((s - jnp.max(s, axis=-1, keepdims=True)) * scale)
            den = jnp.sum(e, axis=-1, keepdims=True)
            o = jnp.dot(e.astype(BF16), vb_ref[0:n_keys, :], preferred_element_type=F32) / den
            o_ref[0, :, g * HEAD_DIM:(g + 1) * HEAD_DIM] = o.astype(o_ref.dtype)

    @pl.when(j < lc // tq)
    def _():
        attend(lc)

    @pl.when(j >= lc // tq)
    def _():
        attend(t)


def _attention(p, qg, kg, cos, sin, lc, tq):
    b, t, n = p.shape
    hkv = n // HEAD_DIM // (ATTN_GROUP + 2)
    hq = hkv * ATTN_GROUP
    gw = ATTN_GROUP * HEAD_DIM
    return pl.pallas_call(
        functools.partial(_attn_kernel, lc=lc, tq=tq, t=t),
        grid=(b, hkv, t // tq),
        in_specs=[pl.BlockSpec((1, tq, gw), lambda bi, h, j: (bi, j, h)),
                  pl.BlockSpec((1, t, HEAD_DIM), lambda bi, h, j: (bi, 0, hq + h)),
                  pl.BlockSpec((1, t, HEAD_DIM), lambda bi, h, j: (bi, 0, hq + hkv + h)),
                  pl.BlockSpec((1, HEAD_DIM), lambda bi, h, j: (0, 0)),
                  pl.BlockSpec((1, HEAD_DIM), lambda bi, h, j: (0, 0)),
                  pl.BlockSpec((t, HEAD_DIM), lambda bi, h, j: (0, 0)),
                  pl.BlockSpec((t, HEAD_DIM), lambda bi, h, j: (0, 0))],
        out_specs=pl.BlockSpec((1, tq, gw), lambda bi, h, j: (bi, j, h)),
        out_shape=jax.ShapeDtypeStruct((b, t, hq * HEAD_DIM), BF16),
        scratch_shapes=[pltpu.VMEM((t, HEAD_DIM), BF16), pltpu.VMEM((t, HEAD_DIM), BF16)],
        compiler_params=_cparams("parallel", "parallel", "arbitrary"),
        name="attention",
    )(p, p, p, qg, kg, cos, sin)


def _dn_prep_kernel(p_ref, cw_ref, o_ref, *, lc, t, n_qk_tiles):
    c = pl.program_id(1)
    heads = p_ref.shape[2] // HEAD_DIM
    tt = math.gcd(lc, t - lc)
    tt = _pick(tt, (128, 64, 32, 16, 8))
    halo = 8
    zeros = jnp.zeros((halo, HEAD_DIM), F32)
    post = jnp.where(c * heads < n_qk_tiles, HEAD_DIM ** -0.5, 1.0)
    is_qk = c * heads < 2 * n_qk_tiles
    for r0 in range(0, t, tt):
        for hh in range(heads):
            cols = slice(hh * HEAD_DIM, (hh + 1) * HEAD_DIM)
            top = zeros if r0 in (0, lc) else p_ref[0, r0 - halo:r0, cols]
            bot = zeros if r0 + tt in (lc, t) else p_ref[0, r0 + tt:r0 + tt + halo, cols]
            win = jnp.concatenate([top, p_ref[0, r0:r0 + tt, cols], bot], axis=0)
            acc = win[halo:halo + tt] * cw_ref[DN_CONV_W // 2:DN_CONV_W // 2 + 1, cols]
            for jj in range(DN_CONV_W):
                off = jj - DN_CONV_W // 2
                if off != 0:
                    acc = acc + (pltpu.roll(win, (-off) % (tt + 2 * halo), 0)[halo:halo + tt]
                                 * cw_ref[jj:jj + 1, cols])
            y = _silu(acc)
            nrm = lax.rsqrt(jnp.sum(y * y, axis=-1, keepdims=True) + NORM_EPS)
            o_ref[0, r0:r0 + tt, cols] = y * jnp.where(is_qk, nrm * post, 1.0)


def _dn_prep(p, conv_w, lc, n_qk_tiles):
    b, t, _ = p.shape
    nch = conv_w.shape[1]
    cw = _pick(n_qk_tiles, (4, 2, 1)) * HEAD_DIM
    return pl.pallas_call(
        functools.partial(_dn_prep_kernel, lc=lc, t=t, n_qk_tiles=n_qk_tiles),
        grid=(b, nch // cw),
        in_specs=[pl.BlockSpec((1, t, cw), lambda bi, c: (bi, 0, c)),
                  pl.BlockSpec((DN_CONV_W, cw), lambda bi, c: (0, c))],
        out_specs=pl.BlockSpec((1, t, cw), lambda bi, c: (bi, 0, c)),
        out_shape=jax.ShapeDtypeStruct((b, t, nch), F32),
        compiler_params=_cparams("parallel", "parallel"),
        name="dn_prep",
    )(p, conv_w)


def _dn_gates_kernel(ba_ref, alog_ref, dtb_ref, o_ref, *, t, hv):
    x = ba_ref[0]
    lane = lax.broadcasted_iota(jnp.int32, (DN_CHUNK, 4 * hv), 1)
    g = -jnp.exp(alog_ref[...]) * _softplus(x + dtb_ref[...])
    ii = lax.broadcasted_iota(jnp.int32, (DN_CHUNK, DN_CHUNK), 0)
    jj = lax.broadcasted_iota(jnp.int32, (DN_CHUNK, DN_CHUNK), 1)
    lower = (ii >= jj).astype(F32)
    upper = (ii <= jj).astype(F32)
    for c in range(t // DN_CHUNK):
        rows = slice(c * DN_CHUNK, (c + 1) * DN_CHUNK)
        gch = g[rows]
        pre = jnp.dot(lower, gch, precision=lax.Precision.HIGHEST, preferred_element_type=F32)
        suf = jnp.dot(upper, gch, precision=lax.Precision.HIGHEST, preferred_element_type=F32)
        o_ref[0, rows, :] = jnp.where(lane < 2 * hv, jax.nn.sigmoid(x[rows]),
                                      jnp.where(lane < 3 * hv, pre, suf))


def _dn_gates(ba, a_log, dt_bias):
    b, t, n = ba.shape
    hv = n // 4
    zeros = jnp.zeros((1, 2 * hv), F32)
    alog = jnp.concatenate([zeros, a_log.reshape(1, 2 * hv)], axis=1)
    dtb = jnp.concatenate([zeros, dt_bias.reshape(1, 2 * hv)], axis=1)
    return pl.pallas_call(
        functools.partial(_dn_gates_kernel, t=t, hv=hv),
        grid=(b,),
        in_specs=[pl.BlockSpec((1, t, n), lambda bi: (bi, 0, 0)),
                  pl.BlockSpec((1, n), lambda bi: (0, 0)),
                  pl.BlockSpec((1, n), lambda bi: (0, 0))],
        out_specs=pl.BlockSpec((1, t, n), lambda bi: (bi, 0, 0)),
        out_shape=jax.ShapeDtypeStruct((b, t, n), F32),
        compiler_params=_cparams("parallel"),
        name="dn_gates",
    )(ba, alog, dtb)


def _dn_chunk_prep(kk, qk_raw, qc, kc, vc, beta_r, gc_r, reverse):
    c = DN_CHUNK
    ii = lax.broadcasted_iota(jnp.int32, (c, c), 0)
    jj = lax.broadcasted_iota(jnp.int32, (c, c), 1)
    incl = (ii <= jj) if reverse else (ii >= jj)
    strict = (ii < jj) if reverse else (ii > jj)
    beta_c = _row_to_col(beta_r, c)
    gc_c = _row_to_col(gc_r, c)
    g_end = gc_r[:, 0:1] if reverse else gc_r[:, c - 1:c]
    decay = jnp.exp(jnp.where(incl, gc_c - gc_r, -jnp.inf))
    m = jnp.where(strict, beta_c * kk * decay, 0.0)
    eg = jnp.exp(gc_c)
    x = jnp.concatenate([vc * beta_c, kc * (beta_c * eg)], axis=1)
    qk = jnp.where(incl, qk_raw * decay, 0.0)
    return (m, x, qk.astype(BF16), qc * eg, (kc * jnp.exp(g_end - gc_c)).astype(BF16), jnp.exp(g_end))


def _unit_triangular_solve(ms, xs, between=()):
    pws = ms
    ns = [-m for m in ms]
    between = list(between)
    for _ in range(int(math.log2(DN_CHUNK)) - 1):
        pws = [_bdot(p, p) for p in pws]
        ns = [n + p + _bdot(p, n) for p, n in zip(pws, ns)]
        if between:
            between.pop(0)()
    xs = [x + _bdot(n, x) for n, x in zip(ns, xs)]
    for fn in between:
        fn()
    return xs


def _dn_core_kernel(q_ref, k_ref, v_ref, z_ref, gt_ref, ng_ref, o_ref,
                    a_ref, b_ref, c_ref, ge_ref, s_ref, acc_ref, *, lc, t, hv):
    qh = pl.program_id(1)
    c64 = DN_CHUNK
    nc = t // c64
    ncc = lc // c64
    combos = [(sub, d) for sub in range(2) for d in range(2)]

    cpi = _pick(nc, (6, 4, 3, 2))

    def chunk_of(d, n):
        return n if d == 0 else jnp.where(n < ncc, ncc - 1 - n, nc - 1 - (n - ncc))

    def phase_b_step(n):
        cs = [chunk_of(d, n) for _, d in combos]
        ss = [s_ref[ci] for ci in range(4)]
        sb = [s.astype(BF16) for s in ss]
        m1 = [jnp.dot(a_ref[ci, cs[ci]], sb[ci], preferred_element_type=F32) for ci in range(4)]
        m2 = [jnp.dot(c_ref[ci, pl.ds(pl.multiple_of(cs[ci] * c64, c64), c64), :], sb[ci],
                      preferred_element_type=F32) for ci in range(4)]
        for ci, (sub, d) in enumerate(combos):
            s_ref[ci] = ss[ci] * ge_ref[ci, pl.ds(cs[ci], 1), :][:, 0:1] - m1[ci] + b_ref[ci, cs[ci]]
            acc_ref[d, pl.ds(pl.multiple_of(cs[ci] * c64, c64), c64), sub * HEAD_DIM:(sub + 1) * HEAD_DIM] += m2[ci]

    def phase_a(it, between=()):
        prepped, where = [], []
        for j in range(cpi):
            for d in range(2):
                c = chunk_of(d, it * cpi + j)
                r0 = pl.multiple_of(c * c64, c64)
                rows = pl.ds(r0, c64)
                qc, kc = q_ref[0, rows, :], k_ref[0, rows, :]
                gram = _bdot_nt(jnp.concatenate([kc, qc], axis=0), kc)
                kk, qk_raw = gram[:c64], gram[c64:]
                for sub in range(2):
                    head = qh * 2 + sub
                    beta_r = gt_ref[0, d * hv + head, pl.ds(c, 1), :]
                    gc_r = gt_ref[0, (2 + d) * hv + head, pl.ds(c, 1), :]
                    vc = v_ref[0, rows, sub * HEAD_DIM:(sub + 1) * HEAD_DIM]
                    prepped.append(_dn_chunk_prep(kk, qk_raw, qc, kc, vc, beta_r, gc_r, d == 1))
                    where.append((sub * 2 + d, c, rows, sub, d))
        sols = _unit_triangular_solve([p[0] for p in prepped], [p[1] for p in prepped], between)
        kdx = [_bdot_tn(p[4], x) for p, x in zip(prepped, sols)]
        qkx = [_bdot(p[2], x) for p, x in zip(prepped, sols)]
        for (ci, c, rows, sub, d), kx, qx, (_, _, _, qd, _, ge) in zip(where, kdx, qkx, prepped):
            b_ref[ci, c] = kx[:, :HEAD_DIM]
            a_ref[ci, c] = kx[:, HEAD_DIM:].astype(BF16)
            c_ref[ci, rows, :] = (qd - qx[:, HEAD_DIM:]).astype(BF16)
            acc_ref[d, rows, sub * HEAD_DIM:(sub + 1) * HEAD_DIM] = qx[:, :HEAD_DIM]
            ge_ref[ci, pl.ds(c, 1), :] = jnp.broadcast_to(ge, (1, HEAD_DIM))

    s_ref[...] = jnp.zeros_like(s_ref)
    phase_a(0)

    def pipelined(it, carry):
        phase_a(it, [functools.partial(phase_b_step, (it - 1) * cpi + j) for j in range(cpi)])
        return carry

    lax.fori_loop(1, nc // cpi, pipelined, 0)
    for j in range(cpi):
        phase_b_step(nc - cpi + j)

    for sub in range(2):
        cols = slice(sub * HEAD_DIM, (sub + 1) * HEAD_DIM)
        o = acc_ref[0, :, cols] + acc_ref[1, :, cols]
        o = o * lax.rsqrt(jnp.mean(o * o, axis=-1, keepdims=True) + NORM_EPS) * ng_ref[...]
        o_ref[0, :, cols] = (o * _silu(z_ref[0, :, cols])).astype(o_ref.dtype)


def _dn_core(qkv, pz, gt, norm_g, lc, n_qk_heads, z_col0):
    b, t, _ = qkv.shape
    hv = 2 * n_qk_heads
    vw = 2 * HEAD_DIM
    nc = t // DN_CHUNK
    return pl.pallas_call(
        functools.partial(_dn_core_kernel, lc=lc, t=t, hv=hv),
        grid=(b, n_qk_heads),
        in_specs=[pl.BlockSpec((1, t, HEAD_DIM), lambda bi, h: (bi, 0, h)),
                  pl.BlockSpec((1, t, HEAD_DIM), lambda bi, h: (bi, 0, n_qk_heads + h)),
                  pl.BlockSpec((1, t, vw), lambda bi, h: (bi, 0, n_qk_heads + h)),
                  pl.BlockSpec((1, t, vw), lambda bi, h: (bi, 0, z_col0 // vw + h)),
                  pl.BlockSpec((1, 4 * hv, nc, DN_CHUNK), lambda bi, h: (bi, 0, 0, 0)),
                  pl.BlockSpec((1, HEAD_DIM), lambda bi, h: (0, 0))],
        out_specs=pl.BlockSpec((1, t, vw), lambda bi, h: (bi, 0, h)),
        out_shape=jax.ShapeDtypeStruct((b, t, hv * HEAD_DIM), BF16),
        scratch_shapes=[pltpu.VMEM((4, nc, HEAD_DIM, HEAD_DIM), BF16),
                        pltpu.VMEM((4, nc, HEAD_DIM, HEAD_DIM), F32),
                        pltpu.VMEM((4, t, HEAD_DIM), BF16),
                        pltpu.VMEM((4, -(-nc // 8) * 8, HEAD_DIM), F32),
                        pltpu.VMEM((4, HEAD_DIM, HEAD_DIM), F32),
                        pltpu.VMEM((2, t, vw), F32)],
        compiler_params=_cparams("parallel", "parallel"),
        name="dn_core",
    )(qkv, qkv, qkv, pz, gt, norm_g)


def _hg_kernel(q_ref, i_ref, gate_ref, ff_ref, fb_ref, lb_ref, ng_ref, o_ref,
               lf_ref, k_ref, st_ref, acc_ref, *, lc, t):
    cs, c16 = HG_SUPER, HG_CHUNK
    nsub = cs // c16
    nc = t // cs
    ncc = lc // cs
    ii = lax.broadcasted_iota(jnp.int32, (cs, cs), 0)
    jj = lax.broadcasted_iota(jnp.int32, (cs, cs), 1)
    krow = lax.broadcasted_iota(jnp.int32, (cs, HEAD_DIM), 0)
    incl = [ii >= jj, ii <= jj]
    sums = [jnp.concatenate([m.astype(BF16), (m & (ii // c16 == jj // c16)).astype(BF16),
                             jnp.ones((8, cs), BF16)], axis=0) for m in incl]
    for d in range(2):
        f_ref = ff_ref if d == 0 else fb_ref
        lb = lb_ref[d:d + 1, :]
        fg = lb + (1.0 - lb) * _sigmoid(f_ref[0])
        k_ref[d] = 1.0 - fg
        lf_ref[d] = jnp.log(fg)
    st_ref[...] = jnp.zeros_like(st_ref)
    per_it = _pick(nc, (4, 3, 2))

    def scan_stage(jobs):
        scans = []
        for d, r0 in jobs:
            lf = lf_ref[d, pl.ds(r0, cs), :]
            hi = lf.astype(BF16)
            r1 = lf - hi.astype(F32)
            mid = r1.astype(BF16)
            lo = (r1 - mid.astype(F32)).astype(BF16)
            sc = jnp.dot(sums[d], jnp.concatenate([hi, mid, lo], axis=1), preferred_element_type=F32)
            scans.append(sc[:, :HEAD_DIM] + sc[:, HEAD_DIM:2 * HEAD_DIM] + sc[:, 2 * HEAD_DIM:])
        return scans

    def score_stage(jobs, scans):
        kcs = [k_ref[d, pl.ds(r0, cs), :] for d, r0 in jobs]
        qcs = [q_ref[0, pl.ds(r0, cs), :] for d, r0 in jobs]
        vvs = [i_ref[0, pl.ds(r0, cs), :].astype(BF16) for d, r0 in jobs]
        gcs = [sc[:cs] for sc in scans]
        tots = [sc[2 * cs:2 * cs + 1] for sc in scans]
        qas = [q * jnp.exp(sc[cs:2 * cs]) for q, sc in zip(qcs, scans)]
        qds = [q * jnp.exp(g) for q, g in zip(qcs, gcs)]
        kds = [k * jnp.exp(tt - g) for k, tt, g in zip(kcs, tots, gcs)]
        upd = [_bdot_tn(v, kd) for v, kd in zip(vvs, kds)]
        scaled = []
        for (d, _), gc, kc in zip(jobs, gcs, kcs):
            for a in range(nsub):
                if d == 0:
                    ref = gc[a * c16 - 1:a * c16, :] if a > 0 else None
                    allowed = krow < (a + 1) * c16
                else:
                    ref = gc[(a + 1) * c16:(a + 1) * c16 + 1, :] if a < nsub - 1 else None
                    allowed = krow >= a * c16
                expo = -gc if ref is None else ref - gc
                scaled.append(kc * jnp.exp(jnp.where(allowed, expo, -jnp.inf)))
        parts = [_bdot_nt(qas[j][a * c16:(a + 1) * c16], scaled[j * nsub + a])
                 for j in range(len(jobs)) for a in range(nsub)]
        return vvs, tots, qds, upd, parts

    def output_stage(jobs, staged, sts):
        vvs, tots, qds, upd, parts = staged
        ps = [jnp.where(incl[d], jnp.concatenate(parts[j * nsub:(j + 1) * nsub], axis=0), 0.0)
              for j, (d, _) in enumerate(jobs)]
        intra = [_bdot(p, v) for p, v in zip(ps, vvs)]
        for j, (d, r0) in enumerate(jobs):
            acc_ref[d, pl.ds(r0, cs), :] = intra[j] + _bdot_nt(qds[j], sts[d])
            sts[d] = sts[d] * jnp.exp(tots[j]) + upd[j]

    def body(it, carry):
        jobs = []
        for u in range(per_it):
            n = it * per_it + u
            jobs.append((0, pl.multiple_of(n * cs, cs)))
            jobs.append((1, pl.multiple_of(jnp.where(n < ncc, ncc - 1 - n, nc - 1 - (n - ncc)) * cs, cs)))
        half = len(jobs) // 2 if len(jobs) >= 4 else len(jobs)
        first, second = jobs[:half], jobs[half:]
        scans1 = scan_stage(first)
        scans2 = scan_stage(second)
        staged1 = score_stage(first, scans1)
        staged2 = score_stage(second, scans2) if second else None
        sts = [st_ref[d] for d in range(2)]
        output_stage(first, staged1, sts)
        if second:
            output_stage(second, staged2, sts)
        for d in range(2):
            st_ref[d] = sts[d]
        return carry

    lax.fori_loop(0, nc // per_it, body, 0)
    o = acc_ref[0] + acc_ref[1]
    o = o * lax.rsqrt(jnp.mean(o * o, axis=-1, keepdims=True) + NORM_EPS) * ng_ref[...]
    o_ref[0] = (o * _sigmoid(gate_ref[0])).astype(o_ref.dtype)


def _hgrn2(p, lb, norm_g, lc, n_heads):
    b, t, _ = p.shape
    assert lc % HG_SUPER == 0 and t % HG_SUPER == 0
    blk = lambda k: pl.BlockSpec((1, t, HEAD_DIM), lambda bi, h, k=k: (bi, 0, k * n_heads + h))
    seq = lambda dt: pltpu.VMEM((2, t, HEAD_DIM), dt)
    return pl.pallas_call(
        functools.partial(_hg_kernel, lc=lc, t=t),
        grid=(b, n_heads),
        in_specs=[blk(0), blk(1), blk(2), blk(3), blk(4),
                  pl.BlockSpec((2, HEAD_DIM), lambda bi, h: (0, h)),
                  pl.BlockSpec((1, HEAD_DIM), lambda bi, h: (0, 0))],
        out_specs=pl.BlockSpec((1, t, HEAD_DIM), lambda bi, h: (bi, 0, h)),
        out_shape=jax.ShapeDtypeStruct((b, t, n_heads * HEAD_DIM), BF16),
        scratch_shapes=[seq(F32), seq(F32), pltpu.VMEM((2, HEAD_DIM, HEAD_DIM), F32), seq(F32)],
        compiler_params=_cparams("parallel", "parallel"),
        name="hgrn2",
    )(p, p, p, p, p, lb, norm_g)


def _split_bf16(x):
    hi = x.astype(BF16)
    return hi, (x - hi.astype(F32)).astype(BF16)


def _router_kernel(h_ref, w_ref, b_ref, e_ref, g_ref):
    h_hi, h_lo = _split_bf16(h_ref[...])
    w_hi, w_lo = _split_bf16(w_ref[...])
    nt = (((1,), (1,)), ((), ()))
    logits = (lax.dot_general(w_hi, h_hi, nt, preferred_element_type=F32)
              + lax.dot_general(w_hi, h_lo, nt, preferred_element_type=F32)
              + lax.dot_general(w_lo, h_hi, nt, preferred_element_type=F32))
    scores = jax.nn.sigmoid(logits)
    sel = scores + b_ref[...]
    epg = EXPERTS_PER_GROUP
    rows = lambda a, e: a[e:e + 1, :]
    best_gs = None
    for g in range(N_GROUPS):
        gs = None
        for a in range(epg):
            for c in range(a + 1, epg):
                pair = rows(sel, g * epg + a) + rows(sel, g * epg + c)
                gs = pair if gs is None else jnp.maximum(gs, pair)
        if best_gs is None:
            best_gs, group = gs, jnp.zeros(gs.shape, jnp.int32)
        else:
            upd = gs > best_gs
            best_gs = jnp.where(upd, gs, best_gs)
            group = jnp.where(upd, g, group)

    def in_group(a, l):
        v = rows(a, l)
        for g in range(1, N_GROUPS):
            v = jnp.where(group == g, rows(a, g * epg + l), v)
        return v

    sel_l = [in_group(sel, l) for l in range(epg)]
    sc_l = [in_group(scores, l) for l in range(epg)]

    def top1(exclude):
        best, idx, sc = None, None, None
        for l in range(epg):
            v = sel_l[l] if exclude is None else jnp.where(exclude == l, -jnp.inf, sel_l[l])
            if best is None:
                best, idx, sc = v, jnp.zeros(v.shape, jnp.int32), sc_l[l]
            else:
                upd = v > best
                best = jnp.where(upd, v, best)
                idx = jnp.where(upd, l, idx)
                sc = jnp.where(upd, sc_l[l], sc)
        return idx, sc

    l0, s0 = top1(None)
    l1, s1 = top1(l0)
    e_ref[0:1, :] = group * epg + l0
    e_ref[1:2, :] = group * epg + l1
    den = s0 + s1
    g_ref[0:1, :] = s0 / den
    g_ref[1:2, :] = s1 / den


def _router(h, router_w, router_b):
    m, d = h.shape
    tm = _pick(m, (512, 256, 128))
    ne = router_w.shape[1]
    return pl.pallas_call(
        _router_kernel,
        grid=(m // tm,),
        in_specs=[pl.BlockSpec((tm, d), lambda i: (i, 0)),
                  pl.BlockSpec((ne, d), lambda i: (0, 0)),
                  pl.BlockSpec((ne, 1), lambda i: (0, 0))],
        out_specs=[pl.BlockSpec((TOP_K, tm), lambda i: (0, i)),
                   pl.BlockSpec((TOP_K, tm), lambda i: (0, i))],
        out_shape=[jax.ShapeDtypeStruct((TOP_K, m), jnp.int32), jax.ShapeDtypeStruct((TOP_K, m), F32)],
        compiler_params=_cparams("parallel"),
        name="router",
    )(h, router_w.T, router_b.reshape(ne, 1))


def _moe_ffn_kernel(bexp_ref, code_ref, h_hbm, wg_ref, wu_ref, wd_ref, f_hbm,
                    x0, x1, x2, y0, y1, y2, sem_in, sem_out, *, n_tok):
    i = pl.program_id(0)
    last = pl.num_programs(0) - 1
    eb = EXPERT_BLOCK
    xbufs, ybufs = (x0, x1, x2), (y0, y1, y2)
    ns = len(xbufs)

    def gather(blk, sl, r):
        tok = code_ref[(blk + 1) * eb + r] & (CODE_K_UNIT - 1)
        return pltpu.make_async_copy(h_hbm.at[pl.ds(tok, 1)], xbufs[sl].at[pl.ds(r, 1)], sem_in.at[sl])

    def scatter(blk, sl, r):
        code = code_ref[(blk + 1) * eb + r]
        dst = lax.shift_right_logical(code, CODE_K_SHIFT) * n_tok + (code & (CODE_K_UNIT - 1))
        return pltpu.make_async_copy(ybufs[sl].at[pl.ds(r, 1)], f_hbm.at[pl.ds(dst, 1)], sem_out.at[sl])

    def for_rows(fn):
        def body(r, carry):
            fn(r)
            return carry
        lax.fori_loop(0, eb, body, 0, unroll=8)

    @pl.when(i == 0)
    def _():
        for sl in range(ns):
            ybufs[sl][...] = jnp.zeros_like(ybufs[sl])
        for sl in range(2):
            spare = pltpu.make_async_copy(ybufs[sl], f_hbm.at[pl.ds(TOP_K * n_tok + sl * eb, eb)],
                                          sem_out.at[sl])
            spare.start()
            spare.wait()
        for_rows(lambda r: gather(0, 0, r).start())
        for_rows(lambda r: gather(1, 1, r).start())

    def step(sl):
        ahead, behind = (sl + 2) % ns, (sl + 2) % ns
        @pl.when(i >= 2)
        def _():
            for_rows(lambda r: scatter(i - 3, sl, r).wait())
        for_rows(lambda r: gather(i, sl, r).wait())
        x = xbufs[sl][...].astype(BF16)
        for r in range(eb):
            gather(i + 2, ahead, r).start()
        a = _silu(jnp.dot(x, wg_ref[0, 0], preferred_element_type=F32)) \
            * jnp.dot(x, wu_ref[0, 0], preferred_element_type=F32)
        for r in range(eb):
            scatter(i - 1, behind, r).start(priority=r % 2)
        ybufs[sl][...] = jnp.dot(a.astype(BF16), wd_ref[0, 0], preferred_element_type=F32)

    for sl in range(ns):
        @pl.when(i % ns == sl)
        def _(sl=sl):
            step(sl)

    @pl.when(i == last)
    def _():
        for sl in range(ns):
            @pl.when(i % ns == sl)
            def _(sl=sl):
                for_rows(lambda r: scatter(i - 2, (sl + 1) % ns, r).wait())
                for_rows(lambda r: scatter(i - 1, (sl + 2) % ns, r).wait())
                for_rows(lambda r: gather(i + 1, (sl + 1) % ns, r).wait())
                for_rows(lambda r: gather(i + 2, (sl + 2) % ns, r).wait())


def _moe_ffn(h, block_expert, code_buf, wg, wu, wd, layer):
    m, d = h.shape
    de = wg.shape[3]
    nb = block_expert.shape[0]
    wmap = lambda i, be, cb: (layer, be[jnp.minimum(i, nb - 1)], 0, 0)
    buf = pltpu.VMEM((EXPERT_BLOCK, d), F32)
    grid_spec = pltpu.PrefetchScalarGridSpec(
        num_scalar_prefetch=2,
        grid=(nb + 1,),
        in_specs=[pl.BlockSpec(memory_space=pl.ANY),
                  pl.BlockSpec((1, 1, d, de), wmap),
                  pl.BlockSpec((1, 1, d, de), wmap),
                  pl.BlockSpec((1, 1, de, d), wmap)],
        out_specs=pl.BlockSpec(memory_space=pl.ANY),
        scratch_shapes=[buf] * 6 + [pltpu.SemaphoreType.DMA((3,)), pltpu.SemaphoreType.DMA((3,))],
    )
    return pl.pallas_call(
        functools.partial(_moe_ffn_kernel, n_tok=m),
        grid_spec=grid_spec,
        out_shape=jax.ShapeDtypeStruct((TOP_K * m + 2 * EXPERT_BLOCK, d), F32),
        compiler_params=_cparams("arbitrary"),
        name="moe_ffn",
    )(block_expert, code_buf, h, wg, wu, wd)


def _moe_dispatch(expert):
    m = expert.shape[1]
    n_pairs = m * TOP_K
    e_flat = expert.T.reshape(-1)
    onehot = (e_flat[:, None] == jnp.arange(N_EXPERTS)[None, :]).astype(jnp.int32)
    csum = jnp.cumsum(onehot, axis=0)
    rank = jnp.take_along_axis(csum, e_flat[:, None], axis=1)[:, 0] - 1
    counts = csum[-1]
    padded = (counts + EXPERT_BLOCK - 1) // EXPERT_BLOCK * EXPERT_BLOCK
    pend = jnp.cumsum(padded)
    pstart = pend - padded
    dest = pstart[e_flat] + rank
    n_blocks = -(-(n_pairs + N_EXPERTS * (EXPERT_BLOCK - 1)) // EXPERT_BLOCK)
    n_rows = n_blocks * EXPERT_BLOCK
    assert 2 * EXPERT_BLOCK <= m <= CODE_K_UNIT
    pair = jnp.arange(n_pairs, dtype=jnp.int32)
    row = jnp.arange(n_rows + 4 * EXPERT_BLOCK, dtype=jnp.int32)
    pad_code = TOP_K * CODE_K_UNIT + row % (2 * EXPERT_BLOCK)
    code_buf = pad_code.at[dest + EXPERT_BLOCK].set((pair % TOP_K) * CODE_K_UNIT + pair // TOP_K)
    blk_start = jnp.arange(n_blocks) * EXPERT_BLOCK
    block_expert = jnp.minimum(jnp.searchsorted(pend, blk_start, side='right'), N_EXPERTS - 1).astype(jnp.int32)
    return block_expert, code_buf


def _moe(h2, router_w, router_b, wg, wu, wd, layer):
    expert, gate = _router(h2, router_w, router_b)
    block_expert, code_buf = _moe_dispatch(expert)
    f = _moe_ffn(h2, block_expert, code_buf, wg, wu, wd, layer)
    return f, jnp.pad(gate.T, ((0, 0), (0, HEAD_DIM - TOP_K)))


def kernel(x, c, ctx, c_ctx, ada_w, ada_b, ln_g, ln_b, attn_w_in, attn_q_g, attn_k_g, attn_w_out, dn_w_in, dn_conv, dn_a_log, dn_dt_bias, dn_norm_g, dn_w_out, hg_w_in, hg_lb, hg_norm_g, hg_w_out, router_w, router_b, moe_w_gate, moe_w_up, moe_w_down):
    b, l, d = x.shape
    lc = ctx.shape[1]
    t = lc + l
    m = b * t
    depth = ada_w.shape[0]
    alpha = (2.0 * depth) ** 0.25
    tm = _pick(math.gcd(lc, l), (256, 128, 64))
    tiles_per_seq, ctx_tiles = t // tm, lc // tm
    n_heads = d // HEAD_DIM

    mp = -(-(b + 1) // 8) * 8
    cc = jnp.concatenate([c, c_ctx[None, :], jnp.zeros((mp - b - 1, d), F32)], axis=0)
    mod = _ada_mod(cc, ada_w, ada_b)
    mod_l = mod[:, :b].reshape(depth, b, 1, N_MOD, d)
    mod_c = jnp.broadcast_to(mod[:, b].reshape(depth, 1, 1, N_MOD, d), (depth, b, 1, N_MOD, d))
    modtab = jnp.concatenate([mod_c, mod_l], axis=2).reshape(depth, 2 * b, N_MOD, d)

    lb_cum = jnp.cumsum(jax.nn.softmax(hg_lb.astype(F32), axis=1), axis=1)
    lb_cum = lb_cum - lb_cum[:, :1]

    xs, h = _modulate(ctx, x, modtab[0], tm, tiles_per_seq, ctx_tiles, 0, 1, BF16)
    cos, sin = _rope_tables(lc, l)
    rows = (tm, tiles_per_seq, ctx_tiles)

    moe_wg, moe_wu, moe_wd = moe_w_gate.astype(BF16), moe_w_up.astype(BF16), moe_w_down.astype(BF16)

    for i in range(depth):
        kind, j = i % 3, i // 3
        if kind == 0:
            p = _proj(h, attn_w_in[j], 0, attn_w_in.shape[2])
            o = _attention(p.reshape(b, t, -1), attn_q_g[j][None, :], attn_k_g[j][None, :], cos, sin, lc, tm)
            w_out = attn_w_out[j]
        elif kind == 1:
            n_conv = dn_conv.shape[2]
            hv = dn_a_log.shape[2]
            n_main = n_conv + hv * HEAD_DIM
            p = _proj(h, dn_w_in[j], 0, n_main).reshape(b, t, n_main)
            ba = _proj(h, dn_w_in[j], n_main, 4 * hv).reshape(b, t, 4 * hv)
            qkv = _dn_prep(p, dn_conv[j], lc, hv // 2)
            gates = _dn_gates(ba, dn_a_log[j], dn_dt_bias[j])
            gt = jnp.swapaxes(gates, 1, 2).reshape(b, 4 * hv, t // DN_CHUNK, DN_CHUNK)
            o = _dn_core(qkv, p, gt, dn_norm_g[j][None, :], lc, hv // 2, n_conv)
            w_out = dn_w_out[j]
        else:
            p = _proj(h, hg_w_in[j], 0, hg_w_in.shape[2]).reshape(b, t, -1)
            o = _hgrn2(p, lb_cum[:, i], hg_norm_g[j][None, :], lc, n_heads)
            w_out = hg_w_out[j]
        xs, h2 = _out_ln(o.reshape(m, -1), w_out.astype(BF16), xs, modtab[i], modtab[i],
                         ln_g[i, 0][None, :], ln_b[i, 0][None, :], *rows, alpha, 2, 3, 4, F32)
        f, rg = _moe(h2, router_w, router_b, moe_wg, moe_wu, moe_wd, i)
        nxt = modtab[min(i + 1, depth - 1)]
        xs, h = _comb_ln(f, rg, xs, modtab[i], nxt, ln_g[i, 1][None, :], ln_b[i, 1][None, :],
                         *rows, alpha, 5, 0, 1, BF16, latent_only=(i == depth - 1))
    return xs.reshape(b, l, d)
```
